```python
import math
import jax, jax.numpy as jnp
from jax import lax
import numpy as np

D_MODEL = 2048
BATCH = 4
SEQ = 4096
DEPTH = 2

N_MIXERS = 2
N_HGRN_LAYERS = (DEPTH + N_MIXERS - 1) // N_MIXERS
N_NSA_LAYERS = DEPTH // N_MIXERS
RMS_EPS = 1e-6

HGRN_HEAD_DIM = 128
HGRN_HEADS = D_MODEL // HGRN_HEAD_DIM
HGRN_WIDTH = HGRN_HEADS * HGRN_HEAD_DIM
HGRN_CHUNK = 64

NSA_HEAD_DIM = 128
NSA_HEADS = D_MODEL // NSA_HEAD_DIM
NSA_KV_GROUPS = 4
NSA_HPG = NSA_HEADS // NSA_KV_GROUPS
NSA_WIDTH = NSA_HEADS * NSA_HEAD_DIM
NSA_KV_WIDTH = NSA_KV_GROUPS * NSA_HEAD_DIM
N_BRANCH = 3
CMP_BLOCK = 32
CMP_STRIDE = 16
SEL_BLOCK = 64
N_SELECT = 16
WINDOW = 512
PHI_HIDDEN = 128
Q_BLOCK = 32
NSA_PROJ_SIZES = (NSA_WIDTH,) + (NSA_KV_WIDTH,) * 6 + (NSA_HEADS * N_BRANCH, NSA_WIDTH)
NSA_PROJ = sum(NSA_PROJ_SIZES)

NUM_BUCKETS = 32
MAX_DISTANCE = 128

NEG_INF = -1e30
FORCE_SCORE = 1e9

kernel_name = "hybrid_hgrn2_nsa_interleaved"


def rms_norm(x, g):
    xf = x.astype(jnp.float32)
    y = xf * lax.rsqrt(jnp.mean(xf * xf, axis=-1, keepdims=True) + RMS_EPS) * g.astype(jnp.float32)
    return y.astype(x.dtype)


def t5_bucket(dist):
    n = jnp.maximum(dist, 0)
    max_exact = NUM_BUCKETS // 2
    large = max_exact + (jnp.log(jnp.maximum(n, 1).astype(jnp.float32) / max_exact)
                         / math.log(MAX_DISTANCE / max_exact) * (NUM_BUCKETS - max_exact)).astype(jnp.int32)
    large = jnp.minimum(large, NUM_BUCKETS - 1)
    return jnp.where(n < max_exact, n, large)


def masked_softmax(logits, mask):
    p = jax.nn.softmax(jnp.where(mask, logits.astype(jnp.float32), NEG_INF), axis=-1)
    return jnp.where(mask, p, 0.0)


def chunk_gated_recurrence(q, k, v, log_f):
    B, S, H, dk = q.shape
    dv = v.shape[-1]
    C = HGRN_CHUNK
    n_chunks = S // C

    def to_chunks(a):
        return a.reshape(B, n_chunks, C, H, a.shape[-1]).transpose(1, 0, 3, 2, 4)

    causal = jnp.tril(jnp.ones((C, C), dtype=bool))[:, :, None]

    def step(state, inp):
        qc, kc, vc, gc = inp
        b = jnp.cumsum(gc, axis=2)
        o_inter = jnp.einsum('bhtk,bhkv->bhtv', qc * jnp.exp(b), state)
        diff = b[:, :, :, None, :] - b[:, :, None, :, :]
        decay = jnp.where(causal, jnp.exp(jnp.where(causal, diff, 0.0)), 0.0)
        scores = jnp.einsum('bhtk,bhtsk,bhsk->bhts', qc, decay, kc)
        o_intra = jnp.einsum('bhts,bhsv->bhtv', scores, vc)
        b_last = b[:, :, -1:, :]
        state = (jnp.exp(b_last[:, :, 0, :, None]) * state
                 + jnp.einsum('bhsk,bhsv->bhkv', kc * jnp.exp(b_last - b), vc))
        return state, o_inter + o_intra

    state0 = jnp.zeros((B, H, dk, dv), jnp.float32)
    _, o = lax.scan(step, state0, (to_chunks(q), to_chunks(k), to_chunks(v), to_chunks(log_f)))
    return o.transpose(1, 0, 3, 2, 4).reshape(B, S, H, dv)


def hgrn2_mixer(h, w_in, lb, head_gain, w_out):
    B, S, _ = h.shape
    proj = (h @ w_in).astype(jnp.float32).reshape(B, S, 4, HGRN_HEADS, HGRN_HEAD_DIM)
    q, f_pre, v, z = proj[:, :, 0], proj[:, :, 1], proj[:, :, 2], proj[:, :, 3]
    lb = lb.reshape(HGRN_HEADS, HGRN_HEAD_DIM)
    log_f = jnp.log(lb + (1.0 - lb) * jax.nn.sigmoid(f_pre))
    k = (1.0 - lb) * jax.nn.sigmoid(-f_pre)
    o = chunk_gated_recurrence(q * HGRN_HEAD_DIM ** -0.5, k, v, log_f)
    o = rms_norm(o, head_gain) * jax.nn.silu(z)
    return o.reshape(B, S, HGRN_WIDTH).astype(h.dtype) @ w_out


def compress_kv(kv, pe, w1, b1, w2):
    B, S, G, dk = kv.shape
    n_cmp = (S - CMP_BLOCK) // CMP_STRIDE + 1
    chunks = kv.reshape(B, S // CMP_STRIDE, CMP_STRIDE, G, dk)
    ratio = CMP_BLOCK // CMP_STRIDE
    blocks = jnp.concatenate([chunks[:, m:m + n_cmp] for m in range(ratio)], axis=2)
    blocks = blocks + pe[None, None, :, None, :].astype(blocks.dtype)
    flat = blocks.transpose(0, 3, 1, 2, 4).reshape(B, G, n_cmp, CMP_BLOCK * dk)
    return jax.nn.gelu(flat @ w1.astype(jnp.float32) + b1.astype(jnp.float32)) @ w2.astype(jnp.float32)


def nsa_mixer(h, w_in, pe_k, pe_v, phi_k_w1, phi_k_b1, phi_k_w2, phi_v_w1, phi_v_b1, phi_v_w2,
              rel_bias, w_out):
    B, S, _ = h.shape
    G, hpg, dk = NSA_KV_GROUPS, NSA_HPG, NSA_HEAD_DIM
    proj = (h @ w_in).astype(jnp.float32)
    split_at = [int(s) for s in np.cumsum(NSA_PROJ_SIZES)[:-1]]
    q, k_c, v_c, k_s, v_s, k_w, v_w, gate_logits, z = jnp.split(proj, split_at, axis=-1)
    q = q.reshape(B, S, G, hpg, dk) * dk ** -0.5

    def kv4(a):
        return a.reshape(B, S, G, dk)

    K_cmp = compress_kv(kv4(k_c), pe_k, phi_k_w1, phi_k_b1, phi_k_w2)
    V_cmp = compress_kv(kv4(v_c), pe_v, phi_v_w1, phi_v_b1, phi_v_w2)
    n_cmp = K_cmp.shape[2]
    n_sel = S // SEL_BLOCK
    n_top = min(N_SELECT, n_sel)

    def sel_blocks(a):
        return a.reshape(B, n_sel, SEL_BLOCK, G, dk).transpose(0, 3, 1, 2, 4)

    def win_pad(a):
        return jnp.pad(kv4(a).transpose(0, 2, 1, 3), ((0, 0), (0, 0), (WINDOW, 0), (0, 0)))

    K_sel, V_sel = sel_blocks(k_s), sel_blocks(v_s)
    K_win, V_win = win_pad(k_w), win_pad(v_w)

    tab = rel_bias.astype(jnp.float32).reshape(NUM_BUCKETS, G, hpg).transpose(1, 0, 2)
    cmp_end = CMP_STRIDE * jnp.arange(n_cmp) + CMP_BLOCK - 1
    ci = jnp.arange(n_cmp)[:, None]
    sj = jnp.arange(n_sel)[None, :]
    overlap = ((CMP_STRIDE * ci < SEL_BLOCK * (sj + 1))
               & (CMP_STRIDE * ci + CMP_BLOCK > SEL_BLOCK * sj)).astype(jnp.float32)
    bidx = jnp.arange(B)[:, None, None, None]
    gidx = jnp.arange(G)[None, :, None, None]
    n_qblk = S // Q_BLOCK
    q_blocks = q.reshape(B, n_qblk, Q_BLOCK, G, hpg, dk).transpose(1, 0, 3, 2, 4, 5)

    def bias_shared(dist):
        return tab[:, t5_bucket(dist)].transpose(0, 1, 3, 2)

    def block_fn(args):
        blk, qb = args
        start = blk * Q_BLOCK
        t = start + jnp.arange(Q_BLOCK)
        dist_c = t[:, None] - cmp_end[None, :]
        logit_c = jnp.einsum('bgqhd,bgkd->bgqhk', qb, K_cmp) + bias_shared(dist_c)
        p_c = masked_softmax(logit_c, (dist_c >= 0)[:, None, :])
        o_c = jnp.einsum('bgqhk,bgkd->bgqhd', p_c, V_cmp)
        importance = jnp.einsum('bgqhk,kn->bgqn', p_c, overlap)
        j = jnp.arange(n_sel)[None, :]
        cur = (t // SEL_BLOCK)[:, None]
        forced = (j == 0) | (j == cur) | (j == cur - 1)
        visible = j * SEL_BLOCK <= t[:, None]
        score = jnp.where(forced, FORCE_SCORE, jnp.where(visible, importance, NEG_INF))
        _, idx = lax.top_k(score, n_top)
        k_g = K_sel[bidx, gidx, idx].reshape(B, G, Q_BLOCK, n_top * SEL_BLOCK, dk)
        v_g = V_sel[bidx, gidx, idx].reshape(B, G, Q_BLOCK, n_top * SEL_BLOCK, dk)
        pos_s = (idx[..., None] * SEL_BLOCK + jnp.arange(SEL_BLOCK)).reshape(B, G, Q_BLOCK, n_top * SEL_BLOCK)
        dist_s = t[:, None] - pos_s
        bias_s = jnp.moveaxis(tab[gidx, t5_bucket(dist_s)], -1, -2)
        logit_s = jnp.einsum('bgqhd,bgqkd->bgqhk', qb, k_g) + bias_s
        p_s = masked_softmax(logit_s, (dist_s >= 0)[..., None, :])
        o_s = jnp.einsum('bgqhk,bgqkd->bgqhd', p_s, v_g)
        k_wb = lax.dynamic_slice_in_dim(K_win, start, WINDOW + Q_BLOCK, axis=2)
        v_wb = lax.dynamic_slice_in_dim(V_win, start, WINDOW + Q_BLOCK, axis=2)
        pos_w = start - WINDOW + jnp.arange(WINDOW + Q_BLOCK)
        dist_w = t[:, None] - pos_w[None, :]
        mask_w = (dist_w >= 0) & (dist_w < WINDOW) & (pos_w[None, :] >= 0)
        logit_w = jnp.einsum('bgqhd,bgkd->bgqhk', qb, k_wb) + bias_shared(dist_w)
        p_w = masked_softmax(logit_w, mask_w[:, None, :])
        o_w = jnp.einsum('bgqhk,bgkd->bgqhd', p_w, v_wb)
        return jnp.stack([o_c, o_s, o_w], axis=-2)

    out = lax.map(block_fn, (jnp.arange(n_qblk), q_blocks))
    out = out.transpose(1, 0, 3, 2, 4, 5, 6).reshape(B, S, NSA_HEADS, N_BRANCH, dk)
    gates = jax.nn.sigmoid(gate_logits.reshape(B, S, NSA_HEADS, N_BRANCH))
    o = jnp.einsum('bshc,bshcd->bshd', gates, out).reshape(B, S, NSA_WIDTH) * jax.nn.silu(z)
    return o.astype(h.dtype) @ w_out


def setup_inputs(seed: int = 0) -> dict:
    key = jax.random.key(seed)
    ks = jax.random.split(key, 20)
    f32 = jnp.float32

    def nrm(k, shape, scale):
        return jax.random.normal(k, shape, f32) * scale

    nb = N_NSA_LAYERS
    return {
        "x": nrm(ks[0], (BATCH, SEQ, D_MODEL), 1.0),
        "norm_gains": 1.0 + nrm(ks[1], (DEPTH, D_MODEL), 0.02),
        "final_gain": 1.0 + nrm(ks[2], (D_MODEL,), 0.02),
        "rel_bias": nrm(ks[3], (NUM_BUCKETS, NSA_HEADS), 0.5),
        "hgrn_lb": nrm(ks[4], (DEPTH + 1, HGRN_WIDTH), 0.5),
        "hgrn_w_in": nrm(ks[5], (N_HGRN_LAYERS, D_MODEL, 4 * HGRN_WIDTH), D_MODEL ** -0.5),
        "hgrn_head_gain": 1.0 + nrm(ks[6], (N_HGRN_LAYERS, HGRN_HEAD_DIM), 0.02),
        "hgrn_w_out": nrm(ks[7], (N_HGRN_LAYERS, HGRN_WIDTH, D_MODEL), HGRN_WIDTH ** -0.5),
        "nsa_w_in": nrm(ks[8], (nb, D_MODEL, NSA_PROJ), D_MODEL ** -0.5),
        "nsa_pe_k": nrm(ks[9], (nb, CMP_BLOCK, NSA_HEAD_DIM), 0.1),
        "nsa_pe_v": nrm(ks[10], (nb, CMP_BLOCK, NSA_HEAD_DIM), 0.1),
        "nsa_phi_k_w1": nrm(ks[11], (nb, CMP_BLOCK * NSA_HEAD_DIM, PHI_HIDDEN), (CMP_BLOCK * NSA_HEAD_DIM) ** -0.5),
        "nsa_phi_k_b1": nrm(ks[12], (nb, PHI_HIDDEN), 0.01),
        "nsa_phi_k_w2": nrm(ks[13], (nb, PHI_HIDDEN, NSA_HEAD_DIM), PHI_HIDDEN ** -0.5),
        "nsa_phi_v_w1": nrm(ks[14], (nb, CMP_BLOCK * NSA_HEAD_DIM, PHI_HIDDEN), (CMP_BLOCK * NSA_HEAD_DIM) ** -0.5),
        "nsa_phi_v_b1": nrm(ks[15], (nb, PHI_HIDDEN), 0.01),
        "nsa_phi_v_w2": nrm(ks[16], (nb, PHI_HIDDEN, NSA_HEAD_DIM), PHI_HIDDEN ** -0.5),
        "nsa_w_out": nrm(ks[17], (nb, NSA_WIDTH, D_MODEL), NSA_WIDTH ** -0.5),
    }


def reference(x, norm_gains, final_gain, rel_bias, hgrn_lb, hgrn_w_in, hgrn_head_gain, hgrn_w_out,
              nsa_w_in, nsa_pe_k, nsa_pe_v, nsa_phi_k_w1, nsa_phi_k_b1, nsa_phi_k_w2,
              nsa_phi_v_w1, nsa_phi_v_b1, nsa_phi_v_w2, nsa_w_out):
    lower_bounds = jnp.cumsum(jax.nn.softmax(hgrn_lb.astype(jnp.float32), axis=0), axis=0)
    for i in range(DEPTH):
        h = rms_norm(x, norm_gains[i])
        a = i // N_MIXERS
        if i % N_MIXERS == 0:
            y = hgrn2_mixer(h, hgrn_w_in[a], lower_bounds[i], hgrn_head_gain[a], hgrn_w_out[a])
        else:
            y = nsa_mixer(h, nsa_w_in[a], nsa_pe_k[a], nsa_pe_v[a],
                          nsa_phi_k_w1[a], nsa_phi_k_b1[a], nsa_phi_k_w2[a],
                          nsa_phi_v_w1[a], nsa_phi_v_b1[a], nsa_phi_v_w2[a],
                          rel_bias, nsa_w_out[a])
        x = x + y.astype(x.dtype)
    return rms_norm(x, final_gain)
```

```python
import functools
import math

import numpy as np
import jax
import jax.numpy as jnp
from jax import lax
from jax.experimental import pallas as pl
from jax.experimental.pallas import tpu as pltpu

F32 = jnp.float32
BF16 = jnp.bfloat16

D_MODEL = 2048
RMS_EPS = 1e-6
HEAD_DIM = 128

HGRN_HEADS = D_MODEL // HEAD_DIM
HGRN_CHUNK = 64
HGRN_SUB = 16

NSA_HEADS = D_MODEL // HEAD_DIM
NSA_GROUPS = 4
NSA_HPG = NSA_HEADS // NSA_GROUPS
N_BRANCH = 3
CMP_BLOCK = 32
CMP_STRIDE = 16
SEL_BLOCK = 64
N_SELECT = 16
WINDOW = 512
NUM_BUCKETS = 32
MAX_DISTANCE = 128
NEG_INF = -1e30
FORCE_SCORE = 1e9

ATT_TQ = 256
ATT_TK = 256
V7X_VMEM_BYTES = 64 * 1024 * 1024
VMEM_LIMIT = V7X_VMEM_BYTES * 3 // 4

NT_DIMS = (((1,), (1,)), ((), ()))
TN_DIMS = (((0,), (0,)), ((), ()))


def _sigmoid(x):
    return 1.0 / (1.0 + jnp.exp(-x))


def _split3(a):
    hi = a.astype(BF16)
    r1 = a - hi.astype(F32)
    mid = r1.astype(BF16)
    lo = (r1 - mid.astype(F32)).astype(BF16)
    return hi, mid, lo


def _norm_proj_kernel(x_ref, g_ref, w_ref, o_ref, h_scr):
    @pl.when(pl.program_id(1) == 0)
    def _():
        x = x_ref[...]
        ms = jnp.mean(x * x, axis=-1, keepdims=True)
        h_scr[...] = (x * lax.rsqrt(ms + RMS_EPS) * g_ref[...]).astype(BF16)

    o_ref[...] = jnp.dot(h_scr[...], w_ref[...], preferred_element_type=F32).astype(o_ref.dtype)


def _norm_proj(x, gain, w, *, tm=512, tn=1024):
    n, d = x.shape
    nout = w.shape[1]
    return pl.pallas_call(
        _norm_proj_kernel,
        grid=(n // tm, nout // tn),
        in_specs=[pl.BlockSpec((tm, d), lambda i, j: (i, 0)),
                  pl.BlockSpec((1, d), lambda i, j: (0, 0)),
                  pl.BlockSpec((d, tn), lambda i, j: (0, j))],
        out_specs=pl.BlockSpec((tm, tn), lambda i, j: (i, j)),
        out_shape=jax.ShapeDtypeStruct((n, nout), BF16),
        scratch_shapes=[pltpu.VMEM((tm, d), BF16)],
        compiler_params=pltpu.CompilerParams(
            dimension_semantics=("arbitrary", "arbitrary"), vmem_limit_bytes=VMEM_LIMIT),
        name="norm_proj",
    )(x, gain, w)


def _proj_gate_kernel(h_ref, w_ref, wgt_ref, o_ref, gt_ref):
    @pl.when(pl.program_id(1) == 0)
    def _():
        gt_ref[...] = lax.dot_general(wgt_ref[...], h_ref[...], NT_DIMS, preferred_element_type=F32)

    o_ref[...] = jnp.dot(h_ref[...], w_ref[...], preferred_element_type=F32).astype(o_ref.dtype)


def _proj_gate(h, w, wgt, *, tm=512, tn=1024):
    n, d = h.shape
    nout = w.shape[1]
    ng = wgt.shape[0]
    return pl.pallas_call(
        _proj_gate_kernel,
        grid=(n // tm, nout // tn),
        in_specs=[pl.BlockSpec((tm, d), lambda i, j: (i, 0)),
                  pl.BlockSpec((d, tn), lambda i, j: (0, j)),
                  pl.BlockSpec((ng, d), lambda i, j: (0, 0))],
        out_specs=[pl.BlockSpec((tm, tn), lambda i, j: (i, j)),
                   pl.BlockSpec((ng, tm), lambda i, j: (0, i))],
        out_shape=[jax.ShapeDtypeStruct((n, nout), BF16),
                   jax.ShapeDtypeStruct((ng, n), F32)],
        compiler_params=pltpu.CompilerParams(
            dimension_semantics=("arbitrary", "arbitrary"), vmem_limit_bytes=VMEM_LIMIT),
        name="proj_gate",
    )(h, w, wgt)


def _out_proj_kernel(o_ref, w_ref, res_ref, g_ref, *out_refs, final):
    xn = res_ref[...] + jnp.dot(o_ref[...], w_ref[...], preferred_element_type=F32)
    ms = jnp.mean(xn * xn, axis=-1, keepdims=True)
    y = xn * lax.rsqrt(ms + RMS_EPS) * g_ref[...]
    if final:
        out_refs[0][...] = y
    else:
        out_refs[0][...] = xn
        out_refs[1][...] = y.astype(BF16)


def _out_proj(o, w, res, gain, *, final, tm=256):
    n, d = res.shape
    row = pl.BlockSpec((tm, d), lambda i: (i, 0))
    if final:
        out_specs = [row]
        out_shape = [jax.ShapeDtypeStruct((n, d), F32)]
    else:
        out_specs = [row, row]
        out_shape = [jax.ShapeDtypeStruct((n, d), F32), jax.ShapeDtypeStruct((n, d), BF16)]
    return pl.pallas_call(
        functools.partial(_out_proj_kernel, final=final),
        grid=(n // tm,),
        in_specs=[pl.BlockSpec((tm, o.shape[1]), lambda i: (i, 0)),
                  pl.BlockSpec(w.shape, lambda i: (0, 0)),
                  row,
                  pl.BlockSpec((1, d), lambda i: (0, 0))],
        out_specs=out_specs,
        out_shape=out_shape,
        compiler_params=pltpu.CompilerParams(
            dimension_semantics=("arbitrary",), vmem_limit_bytes=VMEM_LIMIT),
        name="out_proj_final" if final else "out_proj",
    )(o, w, res, gain)


def _hgrn_kernel(q_ref, f_ref, v_ref, z_ref, lbp_ref, hg_ref, o_ref, st_scr, *, n_chunks):
    C, SUB = HGRN_CHUNK, HGRN_SUB

    @pl.when(pl.program_id(2) == 0)
    def _():
        st_scr[...] = jnp.zeros_like(st_scr)

    lbp = lbp_ref[...]
    e = jnp.exp(lbp - jnp.max(lbp, axis=0, keepdims=True))
    lb = e[0:1, :] / jnp.sum(e, axis=0, keepdims=True)
    hg = hg_ref[...]

    rr = lax.broadcasted_iota(jnp.int32, (C, C), 0)
    cc = lax.broadcasted_iota(jnp.int32, (C, C), 1)
    tril = (rr >= cc).astype(BF16)
    sub_r = lax.broadcasted_iota(jnp.int32, (SUB, C), 0)
    sub_c = lax.broadcasted_iota(jnp.int32, (SUB, C), 1)

    def chunk(c, carry):
        rows = pl.ds(pl.multiple_of(c * C, C), C)
        q = q_ref[rows, :].astype(F32)
        fp = f_ref[rows, :].astype(F32)
        v = v_ref[rows, :]
        z = z_ref[rows, :].astype(F32)

        sig = _sigmoid(fp)
        g = jnp.log(lb + (1.0 - lb) * sig)
        k = (1.0 - lb) * (1.0 - sig)

        ghi, gmid, glo = _split3(g)
        b = (jnp.dot(tril, ghi, preferred_element_type=F32)
             + jnp.dot(tril, gmid, preferred_element_type=F32)
             + jnp.dot(tril, glo, preferred_element_type=F32))

        st = st_scr[...]
        o = lax.dot_general((q * jnp.exp(b)).astype(BF16), st.astype(BF16), NT_DIMS,
                            preferred_element_type=F32)

        a_rows = []
        for i in range(C // SUB):
            lo, hi = i * SUB, (i + 1) * SUB
            bq, qq = b[lo:hi], q[lo:hi]
            if i > 0:
                r = b[lo - 1:lo]
                qi = (qq * jnp.exp(bq - r)).astype(BF16)
                ki = (k * jnp.exp(jnp.minimum(r - b, 0.0))).astype(BF16)
                a = lax.dot_general(qi, ki, NT_DIMS, preferred_element_type=F32)
                a = jnp.where(sub_c < lo, a, 0.0)
            else:
                a = jnp.zeros((SUB, C), F32)
            kk = k[lo:hi]
            for s in range(SUB):
                ed = qq * jnp.exp(jnp.minimum(bq - bq[s:s + 1], 0.0)) * kk[s:s + 1]
                col = jnp.sum(ed, axis=-1, keepdims=True)
                a = jnp.where((sub_c == lo + s) & (sub_r >= s), col, a)
            a_rows.append(a)
        scores = jnp.concatenate(a_rows, axis=0)
        o = o + jnp.dot(scores.astype(BF16), v, preferred_element_type=F32)

        bl = b[C - 1:C]
        kd = (k * jnp.exp(bl - b)).astype(BF16)
        st_scr[...] = st * jnp.exp(bl) + lax.dot_general(v, kd, TN_DIMS, preferred_element_type=F32)

        ms = jnp.mean(o * o, axis=-1, keepdims=True)
        o = o * lax.rsqrt(ms + RMS_EPS) * hg * (z * _sigmoid(z))
        o_ref[rows, :] = o.astype(o_ref.dtype)
        return carry

    lax.fori_loop(0, n_chunks, chunk, 0)


def _hgrn_recurrence(proj, lb_param, head_gain, batch, seq, *, t_blk=512):
    n = proj.shape[0]
    nh = HGRN_HEADS
    ns = seq // t_blk

    def sec(k):
        return pl.BlockSpec((t_blk, HEAD_DIM), lambda b, h, s, k=k: (b * ns + s, k * nh + h))

    return pl.pallas_call(
        functools.partial(_hgrn_kernel, n_chunks=t_blk // HGRN_CHUNK),
        grid=(batch, nh, ns),
        in_specs=[sec(0), sec(1), sec(2), sec(3),
                  pl.BlockSpec((lb_param.shape[0], HEAD_DIM), lambda b, h, s: (0, h)),
                  pl.BlockSpec((1, HEAD_DIM), lambda b, h, s: (0, 0))],
        out_specs=pl.BlockSpec((t_blk, HEAD_DIM), lambda b, h, s: (b * ns + s, h)),
        out_shape=jax.ShapeDtypeStruct((n, nh * HEAD_DIM), BF16),
        scratch_shapes=[pltpu.VMEM((HEAD_DIM, HEAD_DIM), F32)],
        compiler_params=pltpu.CompilerParams(
            dimension_semantics=("arbitrary", "arbitrary", "arbitrary")),
        name="hgrn_recurrence",
    )(proj, proj, proj, proj, lb_param, head_gain)


def _compress_kernel(kc_ref, vc_ref, pek_ref, pev_ref, w1k_ref, w1v_ref, b1k_ref, b1v_ref,
                     w2k_ref, w2v_ref, ko_ref, vo_ref, x_scr, *, nc):
    half = CMP_BLOCK // 2

    def one(src_ref, pe_ref, w1_ref, b1_ref, w2_ref, out_ref):
        x_scr[...] = src_ref[...].astype(F32)
        top = jnp.zeros((nc, HEAD_DIM), F32)
        bot = jnp.zeros((nc, HEAD_DIM), F32)
        for p in range(CMP_STRIDE):
            xp = x_scr[pl.ds(p, nc, stride=CMP_STRIDE), :]
            xa = (xp + pe_ref[p:p + 1, :]).astype(BF16)
            xb = (xp + pe_ref[half + p:half + p + 1, :]).astype(BF16)
            top = top + jnp.dot(xa, w1_ref[p * HEAD_DIM:(p + 1) * HEAD_DIM, :],
                                preferred_element_type=F32)
            bot = bot + jnp.dot(xb, w1_ref[(half + p) * HEAD_DIM:(half + p + 1) * HEAD_DIM, :],
                                preferred_element_type=F32)
        hid = top + pltpu.roll(bot, nc - 1, 0) + b1_ref[...]
        act = jax.nn.gelu(hid, approximate=True)
        out_ref[...] = jnp.dot(act.astype(BF16), w2_ref[...],
                               preferred_element_type=F32).astype(out_ref.dtype)

    one(kc_ref, pek_ref, w1k_ref, b1k_ref, w2k_ref, ko_ref)
    one(vc_ref, pev_ref, w1v_ref, b1v_ref, w2v_ref, vo_ref)


def _compress(proj, col_kc, col_vc, pe_k, pe_v, w1k, w1v, b1k, b1v, w2k, w2v, batch, seq):
    g = NSA_GROUPS
    nc = seq // CMP_STRIDE

    def full(a):
        return pl.BlockSpec(a.shape, lambda b, gg: (0,) * a.ndim)

    out_spec = pl.BlockSpec((None, None, nc, HEAD_DIM), lambda b, gg: (b, gg, 0, 0))
    out_shape = jax.ShapeDtypeStruct((batch, g, nc, HEAD_DIM), BF16)
    return pl.pallas_call(
        functools.partial(_compress_kernel, nc=nc),
        grid=(batch, g),
        in_specs=[pl.BlockSpec((seq, HEAD_DIM), lambda b, gg: (b, col_kc + gg)),
                  pl.BlockSpec((seq, HEAD_DIM), lambda b, gg: (b, col_vc + gg)),
                  full(pe_k), full(pe_v), full(w1k), full(w1v), full(b1k), full(b1v),
                  full(w2k), full(w2v)],
        out_specs=[out_spec, out_spec],
        out_shape=[out_shape, out_shape],
        scratch_shapes=[pltpu.VMEM((seq, HEAD_DIM), F32)],
        compiler_params=pltpu.CompilerParams(dimension_semantics=("arbitrary", "arbitrary")),
        name="nsa_compress",
    )(proj, proj, pe_k, pe_v, w1k, w1v, b1k, b1v, w2k, w2v)


def _nsa_attn_kernel(q_ref, kc_ref, vc_ref, ks_ref, vs_ref, kw_ref, vw_ref, z_ref, gt_ref,
                     bias_ref, cmpb_ref, ovt_ref, o_ref,
                     q_scr, vts_scr, vtw_scr, vtc_scr, selm_scr, m_scr, l_scr, acc_scr, tot_scr,
                     *, seq):
    TQ, TK = ATT_TQ, ATT_TK
    nc = seq // CMP_STRIDE
    nsel = seq // SEL_BLOCK
    ntile = seq // TK
    blk_per_tile = TK // SEL_BLOCK
    g = pl.program_id(1)
    i = pl.program_id(2)
    t0 = i * TQ

    @pl.when(i == 0)
    def _():
        def tr(u, carry):
            rows = pl.ds(pl.multiple_of(u * TK, TK), TK)
            vts_scr[u] = vs_ref[rows, :].astype(F32).T.astype(BF16)
            vtw_scr[u] = vw_ref[rows, :].astype(F32).T.astype(BF16)
            return carry
        lax.fori_loop(0, ntile, tr, 0)
        vtc_scr[...] = vc_ref[...].astype(F32).T.astype(BF16)

    for h in range(NSA_HPG):
        q_scr[h * TQ:(h + 1) * TQ, :] = q_ref[:, h * HEAD_DIM:(h + 1) * HEAD_DIM]

    def gate_row(h, c):
        r = (g * NSA_HPG + h) * N_BRANCH + c
        return _sigmoid(gt_ref[pl.ds(r, 1), :])

    cmp_row0 = pl.multiple_of(nc - i * (TQ // CMP_STRIDE), 8)
    kc = kc_ref[...]
    vtc = vtc_scr[...]
    psum = jnp.zeros((nc, TQ), F32)
    for h in range(NSA_HPG):
        lanes = slice(h * TQ, (h + 1) * TQ)
        s = lax.dot_general(kc, q_scr[lanes, :], NT_DIMS, preferred_element_type=F32)
        s = s + cmpb_ref[pl.ds(cmp_row0, nc), lanes]
        mc = jnp.max(s, axis=0, keepdims=True)
        p = jnp.exp(s - mc)
        inv = jnp.where(mc > 0.1 * NEG_INF, 1.0 / jnp.sum(p, axis=0, keepdims=True), 0.0)
        p = p * inv
        psum = psum + p
        oc = jnp.dot(vtc, p.astype(BF16), preferred_element_type=F32)
        tot_scr[:, lanes] = oc * gate_row(h, 0)

    ovt = ovt_ref[...]
    phi, pmid, plo = _split3(psum)
    imp = (jnp.dot(ovt, phi, preferred_element_type=F32)
           + jnp.dot(ovt, pmid, preferred_element_type=F32)
           + jnp.dot(ovt, plo, preferred_element_type=F32))

    jrow = lax.broadcasted_iota(jnp.int32, (nsel, TQ), 0)
    tcol = t0 + lax.broadcasted_iota(jnp.int32, (nsel, TQ), 1)
    cur = lax.shift_right_arithmetic(tcol, int(math.log2(SEL_BLOCK)))
    forced = (jrow == 0) | (jrow == cur) | (jrow == cur - 1)
    visible = jrow * SEL_BLOCK <= tcol
    score = jnp.where(forced, FORCE_SCORE, jnp.where(visible, imp, NEG_INF))
    rank = jnp.zeros((nsel, TQ), jnp.int32)
    for jp in range(nsel):
        row = score[jp:jp + 1, :]
        ahead = (row > score) | ((row == score) & (jrow > jp))
        rank = rank + ahead.astype(jnp.int32)
    selm_scr[...] = jnp.where(rank < min(N_SELECT, nsel), 0.0, NEG_INF)

    def reset():
        m_scr[...] = jnp.full_like(m_scr, NEG_INF)
        l_scr[...] = jnp.zeros_like(l_scr)
        acc_scr[...] = jnp.zeros_like(acc_scr)

    def tile(k_ref, vt_scr, u, bias_row, use_sel):
        kt = k_ref[pl.ds(pl.multiple_of(u * TK, TK), TK), :]
        vt = vt_scr[u]
        for h in range(NSA_HPG):
            lanes = slice(h * TQ, (h + 1) * TQ)
            s = lax.dot_general(kt, q_scr[lanes, :], NT_DIMS, preferred_element_type=F32)
            s = s + bias_ref[pl.ds(bias_row, TK), lanes]
            if use_sel:
                parts = []
                for r in range(blk_per_tile):
                    srow = selm_scr[pl.ds(u * blk_per_tile + r, 1), :]
                    parts.append(s[r * SEL_BLOCK:(r + 1) * SEL_BLOCK] + srow)
                s = jnp.concatenate(parts, axis=0)
            m_old = m_scr[:, lanes]
            m_new = jnp.maximum(m_old, jnp.max(s, axis=0, keepdims=True))
            alpha = jnp.exp(m_old - m_new)
            p = jnp.exp(s - m_new)
            l_scr[:, lanes] = alpha * l_scr[:, lanes] + jnp.sum(p, axis=0, keepdims=True)
            acc_scr[:, lanes] = alpha * acc_scr[:, lanes] + jnp.dot(
                vt, p.astype(BF16), preferred_element_type=F32)
            m_scr[:, lanes] = m_new

    def finish(c):
        for h in range(NSA_HPG):
            lanes = slice(h * TQ, (h + 1) * TQ)
            w = gate_row(h, c) * (1.0 / l_scr[:, lanes])
            tot_scr[:, lanes] = tot_scr[:, lanes] + acc_scr[:, lanes] * w

    reset()

    def sel_body(u, carry):
        near = u - (i - 1)
        bias_row = jnp.where(near < 0, 0, (near + 2) * TK)
        tile(ks_ref, vts_scr, u, pl.multiple_of(bias_row, TK), True)
        return carry
    lax.fori_loop(0, i + 1, sel_body, 0)
    finish(1)

    reset()
    n_win = (WINDOW + TQ) // TK

    def win_body(w, carry):
        tile(kw_ref, vtw_scr, i - (n_win - 1) + w, pl.multiple_of((w + 1) * TK, TK), False)
        return carry
    lax.fori_loop(jnp.maximum(n_win - 1 - i, 0), n_win, win_body, 0)
    finish(2)

    for h in range(NSA_HPG):
        cols = slice(h * HEAD_DIM, (h + 1) * HEAD_DIM)
        z = z_ref[:, cols].astype(F32)
        o_ref[:, cols] = (tot_scr[:, h * TQ:(h + 1) * TQ].T * (z * _sigmoid(z))).astype(o_ref.dtype)


def _nsa_attention(proj, kcmp, vcmp, gt, bias, cmpb, ovt, cols, batch, seq):
    TQ, TK = ATT_TQ, ATT_TK
    n = proj.shape[0]
    ng = NSA_GROUPS
    nq = seq // TQ
    nc = seq // CMP_STRIDE
    L = NSA_HPG * TQ
    gw = NSA_HPG * HEAD_DIM

    def seq_block(col0):
        return pl.BlockSpec((seq, HEAD_DIM), lambda b, g, i: (b, col0 + g))

    cmp_spec = pl.BlockSpec((None, None, nc, HEAD_DIM), lambda b, g, i: (b, g, 0, 0))
    return pl.pallas_call(
        functools.partial(_nsa_attn_kernel, seq=seq),
        grid=(batch, ng, nq),
        in_specs=[pl.BlockSpec((TQ, gw), lambda b, g, i: (b * nq + i, cols["q"] + g)),
                  cmp_spec, cmp_spec,
                  seq_block(cols["k_s"]), seq_block(cols["v_s"]),
                  seq_block(cols["k_w"]), seq_block(cols["v_w"]),
                  pl.BlockSpec((TQ, gw), lambda b, g, i: (b * nq + i, cols["z"] + g)),
                  pl.BlockSpec((gt.shape[0], TQ), lambda b, g, i: (0, b * nq + i)),
                  pl.BlockSpec((None,) + bias.shape[1:], lambda b, g, i: (g, 0, 0)),
                  pl.BlockSpec((None,) + cmpb.shape[1:], lambda b, g, i: (g, 0, 0)),
                  pl.BlockSpec(ovt.shape, lambda b, g, i: (0, 0))],
        out_specs=pl.BlockSpec((TQ, gw), lambda b, g, i: (b * nq + i, g)),
        out_shape=jax.ShapeDtypeStruct((n, ng * gw), BF16),
        scratch_shapes=[pltpu.VMEM((L, HEAD_DIM), BF16),
                        pltpu.VMEM((seq // TK, HEAD_DIM, TK), BF16),
                        pltpu.VMEM((seq // TK, HEAD_DIM, TK), BF16),
                        pltpu.VMEM((HEAD_DIM, nc), BF16),
                        pltpu.VMEM((seq // SEL_BLOCK, TQ), F32),
                        pltpu.VMEM((1, L), F32),
                        pltpu.VMEM((1, L), F32),
                        pltpu.VMEM((HEAD_DIM, L), F32),
                        pltpu.VMEM((HEAD_DIM, L), F32)],
        compiler_params=pltpu.CompilerParams(
            dimension_semantics=("arbitrary", "arbitrary", "arbitrary"), vmem_limit_bytes=VMEM_LIMIT),
        name="nsa_attention",
    )(proj, kcmp, vcmp, proj, proj, proj, proj, proj, gt, bias, cmpb, ovt)


def _bucket_of_distance():
    d = np.arange(MAX_DISTANCE)
    max_exact = NUM_BUCKETS // 2
    large = max_exact + (np.log(np.maximum(d, 1).astype(np.float32) / max_exact)
                         / math.log(MAX_DISTANCE / max_exact) * (NUM_BUCKETS - max_exact)).astype(np.int32)
    large = np.minimum(large, NUM_BUCKETS - 1)
    return np.where(d < max_exact, d, large)


def _bias_tiles(rel_bias, seq):
    TQ, TK = ATT_TQ, ATT_TK
    nc = seq // CMP_STRIDE
    tb = rel_bias.astype(F32)[_bucket_of_distance()].T
    tb = tb.reshape(NSA_GROUPS, NSA_HPG, MAX_DISTANCE)

    def expand(dist, valid):
        vals = tb[:, :, np.clip(dist, 0, MAX_DISTANCE - 1)]
        vals = jnp.where(valid[None, None], vals, NEG_INF)
        return vals.transpose(0, 2, 1, 3).reshape(NSA_GROUPS, dist.shape[0], NSA_HPG * TQ)

    qi = np.arange(TQ)[None, :]
    far = np.full((TK, TQ), MAX_DISTANCE - 1)
    cw = np.arange(WINDOW + TQ)[:, None]
    dist_w = WINDOW + qi - cw
    bias = jnp.concatenate([expand(far, np.ones_like(far, bool)),
                            expand(dist_w, (dist_w >= 0) & (dist_w < WINDOW))], axis=1)
    m = np.arange(2 * nc)[:, None] - nc
    dist_c = qi - CMP_STRIDE * m - (CMP_BLOCK - 1)
    cmpb = expand(dist_c, dist_c >= 0)
    return bias, cmpb


def _overlap_t(seq):
    nc = seq // CMP_STRIDE
    nsel = seq // SEL_BLOCK
    ci = np.arange(nc)[None, :]
    sj = np.arange(nsel)[:, None]
    ov = (CMP_STRIDE * ci < SEL_BLOCK * (sj + 1)) & (CMP_STRIDE * ci + CMP_BLOCK > SEL_BLOCK * sj)
    ov = ov & (ci < nc - 1)
    return jnp.asarray(ov, BF16)


def kernel(x, norm_gains, final_gain, rel_bias, hgrn_lb, hgrn_w_in, hgrn_head_gain, hgrn_w_out,
           nsa_w_in, nsa_pe_k, nsa_pe_v, nsa_phi_k_w1, nsa_phi_k_b1, nsa_phi_k_w2,
           nsa_phi_v_w1, nsa_phi_v_b1, nsa_phi_v_w2, nsa_w_out):
    batch, seq, d = x.shape
    n = batch * seq
    x2d = x.reshape(n, d)
    qscale = HEAD_DIM ** -0.5

    w_in = hgrn_w_in[0]
    w_in = jnp.concatenate([w_in[:, :d] * qscale, w_in[:, d:]], axis=1).astype(BF16)
    proj = _norm_proj(x2d, norm_gains[0:1], w_in)
    o = _hgrn_recurrence(proj, hgrn_lb, hgrn_head_gain[0:1], batch, seq)
    x1, h1 = _out_proj(o, hgrn_w_out[0].astype(BF16), x2d, norm_gains[1:2], final=False)

    kvw = NSA_GROUPS * HEAD_DIM
    w = nsa_w_in[0]
    gate0 = d + 6 * kvw
    gate1 = gate0 + NSA_HEADS * N_BRANCH
    w_main = jnp.concatenate([w[:, :d] * qscale, w[:, d:gate0], w[:, gate1:]], axis=1).astype(BF16)
    wgt = jnp.zeros((HEAD_DIM, d), F32).at[:gate1 - gate0].set(w[:, gate0:gate1].T).astype(BF16)
    proj, gt = _proj_gate(h1, w_main, wgt)

    blk = {"q": 0}
    for idx, name in enumerate(("k_c", "v_c", "k_s", "v_s", "k_w", "v_w")):
        blk[name] = (d + idx * kvw) // HEAD_DIM
    gw = NSA_HPG * HEAD_DIM
    cols = dict(blk, q=0, z=gate0 // gw)

    kcmp, vcmp = _compress(
        proj, blk["k_c"], blk["v_c"], nsa_pe_k[0], nsa_pe_v[0],
        nsa_phi_k_w1[0].astype(BF16), nsa_phi_v_w1[0].astype(BF16),
        nsa_phi_k_b1[0:1], nsa_phi_v_b1[0:1],
        nsa_phi_k_w2[0].astype(BF16), nsa_phi_v_w2[0].astype(BF16), batch, seq)

    bias, cmpb = _bias_tiles(rel_bias, seq)
    o = _nsa_attention(proj, kcmp, vcmp, gt, bias, cmpb, _overlap_t(seq), cols, batch, seq)
    out, = _out_proj(o, nsa_w_out[0].astype(BF16), x1, final_gain[None, :], final=True)
    return out.reshape(batch, seq, d)
```

```python
import functools
import math

import numpy as np
import jax
import jax.numpy as jnp
from jax import lax
from jax.experimental import pallas as pl
from jax.experimental.pallas import tpu as pltpu

F32 = jnp.float32
BF16 = jnp.bfloat16

D_MODEL = 2048
RMS_EPS = 1e-6
HEAD_DIM = 128

HGRN_HEADS = D_MODEL // HEAD_DIM
HGRN_CHUNK = 64
HGRN_SUB = 16

NSA_HEADS = D_MODEL // HEAD_DIM
NSA_GROUPS = 4
NSA_HPG = NSA_HEADS // NSA_GROUPS
N_BRANCH = 3
CMP_BLOCK = 32
CMP_STRIDE = 16
SEL_BLOCK = 64
N_SELECT = 16
WINDOW = 512
NUM_BUCKETS = 32
MAX_DISTANCE = 128
NEG_INF = -1e30
FORCE_SCORE = 1e9

ATT_TQ = 256
ATT_TK = 256
V7X_VMEM_BYTES = 64 * 1024 * 1024
VMEM_LIMIT = V7X_VMEM_BYTES * 3 // 4

NT_DIMS = (((1,), (1,)), ((), ()))
TN_DIMS = (((0,), (0,)), ((), ()))


def _sigmoid(x):
    return 1.0 / (1.0 + jnp.exp(-x))


def _split3(a):
    hi = a.astype(BF16)
    r1 = a - hi.astype(F32)
    mid = r1.astype(BF16)
    lo = (r1 - mid.astype(F32)).astype(BF16)
    return hi, mid, lo


def _norm_proj_kernel(x_ref, g_ref, w_ref, o_ref, h_scr):
    @pl.when(pl.program_id(1) == 0)
    def _():
        x = x_ref[...]
        ms = jnp.mean(x * x, axis=-1, keepdims=True)
        h_scr[...] = (x * lax.rsqrt(ms + RMS_EPS) * g_ref[...]).astype(BF16)

    o_ref[...] = jnp.dot(h_scr[...], w_ref[...], preferred_element_type=F32).astype(o_ref.dtype)


def _norm_proj(x, gain, w, *, tm=512, tn=1024):
    n, d = x.shape
    nout = w.shape[1]
    return pl.pallas_call(
        _norm_proj_kernel,
        grid=(n // tm, nout // tn),
        in_specs=[pl.BlockSpec((tm, d), lambda i, j: (i, 0)),
                  pl.BlockSpec((1, d), lambda i, j: (0, 0)),
                  pl.BlockSpec((d, tn), lambda i, j: (0, j))],
        out_specs=pl.BlockSpec((tm, tn), lambda i, j: (i, j)),
        out_shape=jax.ShapeDtypeStruct((n, nout), BF16),
        scratch_shapes=[pltpu.VMEM((tm, d), BF16)],
        compiler_params=pltpu.CompilerParams(
            dimension_semantics=("arbitrary", "arbitrary"), vmem_limit_bytes=VMEM_LIMIT),
        name="norm_proj",
    )(x, gain, w)


def _proj_gate_kernel(h_ref, w_ref, wgt_ref, o_ref, gt_ref):
    @pl.when(pl.program_id(1) == 0)
    def _():
        gt_ref[...] = lax.dot_general(wgt_ref[...], h_ref[...], NT_DIMS, preferred_element_type=F32)

    o_ref[...] = jnp.dot(h_ref[...], w_ref[...], preferred_element_type=F32).astype(o_ref.dtype)


def _proj_gate(h, w, wgt, *, tm=512, tn=1024):
    n, d = h.shape
    nout = w.shape[1]
    ng = wgt.shape[0]
    return pl.pallas_call(
        _proj_gate_kernel,
        grid=(n // tm, nout // tn),
        in_specs=[pl.BlockSpec((tm, d), lambda i, j: (i, 0)),
                  pl.BlockSpec((d, tn), lambda i, j: (0, j)),
                  pl.BlockSpec((ng, d), lambda i, j: (0, 0))],
        out_specs=[pl.BlockSpec((tm, tn), lambda i, j: (i, j)),
                   pl.BlockSpec((ng, tm), lambda i, j: (0, i))],
        out_shape=[jax.ShapeDtypeStruct((n, nout), BF16),
                   jax.ShapeDtypeStruct((ng, n), F32)],
        compiler_params=pltpu.CompilerParams(
            dimension_semantics=("arbitrary", "arbitrary"), vmem_limit_bytes=VMEM_LIMIT),
        name="proj_gate",
    )(h, w, wgt)


def _out_proj_kernel(o_ref, w_ref, res_ref, g_ref, *out_refs, final):
    xn = res_ref[...] + jnp.dot(o_ref[...], w_ref[...], preferred_element_type=F32)
    ms = jnp.mean(xn * xn, axis=-1, keepdims=True)
    y = xn * lax.rsqrt(ms + RMS_EPS) * g_ref[...]
    if final:
        out_refs[0][...] = y
    else:
        out_refs[0][...] = xn
        out_refs[1][...] = y.astype(BF16)


def _out_proj(o, w, res, gain, *, final, tm=256):
    n, d = res.shape
    row = pl.BlockSpec((tm, d), lambda i: (i, 0))
    if final:
        out_specs = [row]
        out_shape = [jax.ShapeDtypeStruct((n, d), F32)]
    else:
        out_specs = [row, row]
        out_shape = [jax.ShapeDtypeStruct((n, d), F32), jax.ShapeDtypeStruct((n, d), BF16)]
    return pl.pallas_call(
        functools.partial(_out_proj_kernel, final=final),
        grid=(n // tm,),
        in_specs=[pl.BlockSpec((tm, o.shape[1]), lambda i: (i, 0)),
                  pl.BlockSpec(w.shape, lambda i: (0, 0)),
                  row,
                  pl.BlockSpec((1, d), lambda i: (0, 0))],
        out_specs=out_specs,
        out_shape=out_shape,
        compiler_params=pltpu.CompilerParams(
            dimension_semantics=("arbitrary",), vmem_limit_bytes=VMEM_LIMIT),
        name="out_proj_final" if final else "out_proj",
    )(o, w, res, gain)


def _hgrn_kernel(q_ref, f_ref, v_ref, z_ref, lbp_ref, hg_ref, o_ref, st_scr, *, n_chunks):
    C, SUB = HGRN_CHUNK, HGRN_SUB

    @pl.when(pl.program_id(2) == 0)
    def _():
        st_scr[...] = jnp.zeros_like(st_scr)

    lbp = lbp_ref[...]
    e = jnp.exp(lbp - jnp.max(lbp, axis=0, keepdims=True))
    lb = e[0:1, :] / jnp.sum(e, axis=0, keepdims=True)
    hg = hg_ref[...]

    rr = lax.broadcasted_iota(jnp.int32, (C, C), 0)
    cc = lax.broadcasted_iota(jnp.int32, (C, C), 1)
    tril = (rr >= cc).astype(BF16)
    sub_r = lax.broadcasted_iota(jnp.int32, (SUB, C), 0)
    sub_c = lax.broadcasted_iota(jnp.int32, (SUB, C), 1)

    def chunk(c, carry):
        rows = pl.ds(pl.multiple_of(c * C, C), C)
        q = q_ref[rows, :].astype(F32)
        fp = f_ref[rows, :].astype(F32)
        v = v_ref[rows, :]
        z = z_ref[rows, :].astype(F32)

        sig = _sigmoid(fp)
        g = jnp.log(lb + (1.0 - lb) * sig)
        k = (1.0 - lb) * (1.0 - sig)

        ghi, gmid, glo = _split3(g)
        b = (jnp.dot(tril, ghi, preferred_element_type=F32)
             + jnp.dot(tril, gmid, preferred_element_type=F32)
             + jnp.dot(tril, glo, preferred_element_type=F32))

        st = st_scr[...]
        o = lax.dot_general((q * jnp.exp(b)).astype(BF16), st.astype(BF16), NT_DIMS,
                            preferred_element_type=F32)

        a_rows = []
        for i in range(C // SUB):
            lo, hi = i * SUB, (i + 1) * SUB
            bq, qq = b[lo:hi], q[lo:hi]
            if i > 0:
                r = b[lo - 1:lo]
                qi = (qq * jnp.exp(bq - r)).astype(BF16)
                ki = (k * jnp.exp(jnp.minimum(r - b, 0.0))).astype(BF16)
                a = lax.dot_general(qi, ki, NT_DIMS, preferred_element_type=F32)
                a = jnp.where(sub_c < lo, a, 0.0)
            else:
                a = jnp.zeros((SUB, C), F32)
            kk = k[lo:hi]
            for s in range(SUB):
                ed = qq * jnp.exp(jnp.minimum(bq - bq[s:s + 1], 0.0)) * kk[s:s + 1]
                col = jnp.sum(ed, axis=-1, keepdims=True)
                a = jnp.where((sub_c == lo + s) & (sub_r >= s), col, a)
            a_rows.append(a)
        scores = jnp.concatenate(a_rows, axis=0)
        o = o + jnp.dot(scores.astype(BF16), v, preferred_element_type=F32)

        bl = b[C - 1:C]
        kd = (k * jnp.exp(bl - b)).astype(BF16)
        st_scr[...] = st * jnp.exp(bl) + lax.dot_general(v, kd, TN_DIMS, preferred_element_type=F32)

        ms = jnp.mean(o * o, axis=-1, keepdims=True)
        o = o * lax.rsqrt(ms + RMS_EPS) * hg * (z * _sigmoid(z))
        o_ref[rows, :] = o.astype(o_ref.dtype)
        return carry

    lax.fori_loop(0, n_chunks, chunk, 0, unroll=True)


def _hgrn_recurrence(proj, lb_param, head_gain, batch, seq, *, t_blk=512):
    n = proj.shape[0]
    nh = HGRN_HEADS
    ns = seq // t_blk

    def sec(k):
        return pl.BlockSpec((t_blk, HEAD_DIM), lambda b, h, s, k=k: (b * ns + s, k * nh + h))

    return pl.pallas_call(
        functools.partial(_hgrn_kernel, n_chunks=t_blk // HGRN_CHUNK),
        grid=(batch, nh, ns),
        in_specs=[sec(0), sec(1), sec(2), sec(3),
                  pl.BlockSpec((lb_param.shape[0], HEAD_DIM), lambda b, h, s: (0, h)),
                  pl.BlockSpec((1, HEAD_DIM), lambda b, h, s: (0, 0))],
        out_specs=pl.BlockSpec((t_blk, HEAD_DIM), lambda b, h, s: (b * ns + s, h)),
        out_shape=jax.ShapeDtypeStruct((n, nh * HEAD_DIM), BF16),
        scratch_shapes=[pltpu.VMEM((HEAD_DIM, HEAD_DIM), F32)],
        compiler_params=pltpu.CompilerParams(
            dimension_semantics=("arbitrary", "arbitrary", "arbitrary")),
        name="hgrn_recurrence",
    )(proj, proj, proj, proj, lb_param, head_gain)


def _compress_kernel(kc_ref, vc_ref, pek_ref, pev_ref, w1k_ref, w1v_ref, b1k_ref, b1v_ref,
                     w2k_ref, w2v_ref, ko_ref, vo_ref, x_scr, *, nc):
    half = CMP_BLOCK // 2

    def one(src_ref, pe_ref, w1_ref, b1_ref, w2_ref, out_ref):
        x_scr[...] = src_ref[...].astype(F32)
        top = jnp.zeros((nc, HEAD_DIM), F32)
        bot = jnp.zeros((nc, HEAD_DIM), F32)
        for p in range(CMP_STRIDE):
            xp = x_scr[pl.ds(p, nc, stride=CMP_STRIDE), :]
            xa = (xp + pe_ref[p:p + 1, :]).astype(BF16)
            xb = (xp + pe_ref[half + p:half + p + 1, :]).astype(BF16)
            top = top + jnp.dot(xa, w1_ref[p * HEAD_DIM:(p + 1) * HEAD_DIM, :],
                                preferred_element_type=F32)
            bot = bot + jnp.dot(xb, w1_ref[(half + p) * HEAD_DIM:(half + p + 1) * HEAD_DIM, :],
                                preferred_element_type=F32)
        hid = top + pltpu.roll(bot, nc - 1, 0) + b1_ref[...]
        act = jax.nn.gelu(hid, approximate=True)
        out_ref[...] = jnp.dot(act.astype(BF16), w2_ref[...],
                               preferred_element_type=F32).astype(out_ref.dtype)

    one(kc_ref, pek_ref, w1k_ref, b1k_ref, w2k_ref, ko_ref)
    one(vc_ref, pev_ref, w1v_ref, b1v_ref, w2v_ref, vo_ref)


def _compress(proj, col_kc, col_vc, pe_k, pe_v, w1k, w1v, b1k, b1v, w2k, w2v, batch, seq):
    g = NSA_GROUPS
    nc = seq // CMP_STRIDE

    def full(a):
        return pl.BlockSpec(a.shape, lambda b, gg: (0,) * a.ndim)

    out_spec = pl.BlockSpec((None, None, nc, HEAD_DIM), lambda b, gg: (b, gg, 0, 0))
    out_shape = jax.ShapeDtypeStruct((batch, g, nc, HEAD_DIM), BF16)
    return pl.pallas_call(
        functools.partial(_compress_kernel, nc=nc),
        grid=(batch, g),
        in_specs=[pl.BlockSpec((seq, HEAD_DIM), lambda b, gg: (b, col_kc + gg)),
                  pl.BlockSpec((seq, HEAD_DIM), lambda b, gg: (b, col_vc + gg)),
                  full(pe_k), full(pe_v), full(w1k), full(w1v), full(b1k), full(b1v),
                  full(w2k), full(w2v)],
        out_specs=[out_spec, out_spec],
        out_shape=[out_shape, out_shape],
        scratch_shapes=[pltpu.VMEM((seq, HEAD_DIM), F32)],
        compiler_params=pltpu.CompilerParams(dimension_semantics=("arbitrary", "arbitrary")),
        name="nsa_compress",
    )(proj, proj, pe_k, pe_v, w1k, w1v, b1k, b1v, w2k, w2v)


def _nsa_attn_kernel(q_ref, kc_ref, vc_ref, ks_ref, vs_ref, kw_ref, vw_ref, z_ref, gt_ref,
                     bias_ref, cmpb_ref, ovt_ref, o_ref,
                     q_scr, vts_scr, vtw_scr, vtc_scr, selm_scr, m_scr, l_scr, acc_scr, tot_scr,
                     *, seq):
    TQ, TK = ATT_TQ, ATT_TK
    nc = seq // CMP_STRIDE
    nsel = seq // SEL_BLOCK
    ntile = seq // TK
    blk_per_tile = TK // SEL_BLOCK
    g = pl.program_id(1)
    i = pl.program_id(2)
    t0 = i * TQ

    @pl.when(i == 0)
    def _():
        def tr(u, carry):
            rows = pl.ds(pl.multiple_of(u * TK, TK), TK)
            vts_scr[u] = vs_ref[rows, :].astype(F32).T.astype(BF16)
            vtw_scr[u] = vw_ref[rows, :].astype(F32).T.astype(BF16)
            return carry
        lax.fori_loop(0, ntile, tr, 0)
        vtc_scr[...] = vc_ref[...].astype(F32).T.astype(BF16)

    for h in range(NSA_HPG):
        q_scr[h * TQ:(h + 1) * TQ, :] = q_ref[:, h * HEAD_DIM:(h + 1) * HEAD_DIM]

    def gate_row(h, c):
        r = (g * NSA_HPG + h) * N_BRANCH + c
        return _sigmoid(gt_ref[pl.ds(r, 1), :])

    cmp_row0 = pl.multiple_of(nc - i * (TQ // CMP_STRIDE), 8)
    kc = kc_ref[...]
    vtc = vtc_scr[...]
    psum = jnp.zeros((nc, TQ), F32)
    for h in range(NSA_HPG):
        lanes = slice(h * TQ, (h + 1) * TQ)
        s = lax.dot_general(kc, q_scr[lanes, :], NT_DIMS, preferred_element_type=F32)
        s = s + cmpb_ref[pl.ds(cmp_row0, nc), lanes]
        mc = jnp.max(s, axis=0, keepdims=True)
        p = jnp.exp(s - mc)
        inv = jnp.where(mc > 0.1 * NEG_INF, 1.0 / jnp.sum(p, axis=0, keepdims=True), 0.0)
        p = p * inv
        psum = psum + p
        oc = jnp.dot(vtc, p.astype(BF16), preferred_element_type=F32)
        tot_scr[:, lanes] = oc * gate_row(h, 0)

    ovt = ovt_ref[...]
    phi, pmid, plo = _split3(psum)
    imp = (jnp.dot(ovt, phi, preferred_element_type=F32)
           + jnp.dot(ovt, pmid, preferred_element_type=F32)
           + jnp.dot(ovt, plo, preferred_element_type=F32))

    jrow = lax.broadcasted_iota(jnp.int32, (nsel, TQ), 0)
    tcol = t0 + lax.broadcasted_iota(jnp.int32, (nsel, TQ), 1)
    cur = lax.shift_right_arithmetic(tcol, int(math.log2(SEL_BLOCK)))
    forced = (jrow == 0) | (jrow == cur) | (jrow == cur - 1)
    visible = jrow * SEL_BLOCK <= tcol
    score = jnp.where(forced, FORCE_SCORE, jnp.where(visible, imp, NEG_INF))
    rank = jnp.zeros((nsel, TQ), jnp.int32)
    for jp in range(nsel):
        row = score[jp:jp + 1, :]
        ahead = (row > score) | ((row == score) & (jrow > jp))
        rank = rank + ahead.astype(jnp.int32)
    selm_scr[...] = jnp.where(rank < min(N_SELECT, nsel), 0.0, NEG_INF)

    def reset():
        m_scr[...] = jnp.full_like(m_scr, NEG_INF)
        l_scr[...] = jnp.zeros_like(l_scr)
        acc_scr[...] = jnp.zeros_like(acc_scr)

    def tile(k_ref, vt_scr, u, bias_row, use_sel):
        kt = k_ref[pl.ds(pl.multiple_of(u * TK, TK), TK), :]
        vt = vt_scr[u]
        for h in range(NSA_HPG):
            lanes = slice(h * TQ, (h + 1) * TQ)
            s = lax.dot_general(kt, q_scr[lanes, :], NT_DIMS, preferred_element_type=F32)
            s = s + bias_ref[pl.ds(bias_row, TK), lanes]
            if use_sel:
                parts = []
                for r in range(blk_per_tile):
                    srow = selm_scr[pl.ds(u * blk_per_tile + r, 1), :]
                    parts.append(s[r * SEL_BLOCK:(r + 1) * SEL_BLOCK] + srow)
                s = jnp.concatenate(parts, axis=0)
            m_old = m_scr[:, lanes]
            m_new = jnp.maximum(m_old, jnp.max(s, axis=0, keepdims=True))
            alpha = jnp.exp(m_old - m_new)
            p = jnp.exp(s - m_new)
            l_scr[:, lanes] = alpha * l_scr[:, lanes] + jnp.sum(p, axis=0, keepdims=True)
            acc_scr[:, lanes] = alpha * acc_scr[:, lanes] + jnp.dot(
                vt, p.astype(BF16), preferred_element_type=F32)
            m_scr[:, lanes] = m_new

    def finish(c):
        for h in range(NSA_HPG):
            lanes = slice(h * TQ, (h + 1) * TQ)
            w = gate_row(h, c) * (1.0 / l_scr[:, lanes])
            tot_scr[:, lanes] = tot_scr[:, lanes] + acc_scr[:, lanes] * w

    reset()

    def sel_body(u, carry):
        near = u - (i - 1)
        bias_row = jnp.where(near < 0, 0, (near + 2) * TK)
        tile(ks_ref, vts_scr, u, pl.multiple_of(bias_row, TK), True)
        return carry
    lax.fori_loop(0, i + 1, sel_body, 0)
    finish(1)

    reset()
    n_win = (WINDOW + TQ) // TK

    def win_body(w, carry):
        tile(kw_ref, vtw_scr, i - (n_win - 1) + w, pl.multiple_of((w + 1) * TK, TK), False)
        return carry
    lax.fori_loop(jnp.maximum(n_win - 1 - i, 0), n_win, win_body, 0)
    finish(2)

    for h in range(NSA_HPG):
        cols = slice(h * HEAD_DIM, (h + 1) * HEAD_DIM)
        z = z_ref[:, cols].astype(F32)
        o_ref[:, cols] = (tot_scr[:, h * TQ:(h + 1) * TQ].T * (z * _sigmoid(z))).astype(o_ref.dtype)


def _nsa_attention(proj, kcmp, vcmp, gt, bias, cmpb, ovt, cols, batch, seq):
    TQ, TK = ATT_TQ, ATT_TK
    n = proj.shape[0]
    ng = NSA_GROUPS
    nq = seq // TQ
    nc = seq // CMP_STRIDE
    L = NSA_HPG * TQ
    gw = NSA_HPG * HEAD_DIM

    def seq_block(col0):
        return pl.BlockSpec((seq, HEAD_DIM), lambda b, g, i: (b, col0 + g))

    cmp_spec = pl.BlockSpec((None, None, nc, HEAD_DIM), lambda b, g, i: (b, g, 0, 0))
    return pl.pallas_call(
        functools.partial(_nsa_attn_kernel, seq=seq),
        grid=(batch, ng, nq),
        in_specs=[pl.BlockSpec((TQ, gw), lambda b, g, i: (b * nq + i, cols["q"] + g)),
                  cmp_spec, cmp_spec,
                  seq_block(cols["k_s"]), seq_block(cols["v_s"]),
                  seq_block(cols["k_w"]), seq_block(cols["v_w"]),
                  pl.BlockSpec((TQ, gw), lambda b, g, i: (b * nq + i, cols["z"] + g)),
                  pl.BlockSpec((gt.shape[0], TQ), lambda b, g, i: (0, b * nq + i)),
                  pl.BlockSpec((None,) + bias.shape[1:], lambda b, g, i: (g, 0, 0)),
                  pl.BlockSpec((None,) + cmpb.shape[1:], lambda b, g, i: (g, 0, 0)),
                  pl.BlockSpec(ovt.shape, lambda b, g, i: (0, 0))],
        out_specs=pl.BlockSpec((TQ, gw), lambda b, g, i: (b * nq + i, g)),
        out_shape=jax.ShapeDtypeStruct((n, ng * gw), BF16),
        scratch_shapes=[pltpu.VMEM((L, HEAD_DIM), BF16),
                        pltpu.VMEM((seq // TK, HEAD_DIM, TK), BF16),
                        pltpu.VMEM((seq // TK, HEAD_DIM, TK), BF16),
                        pltpu.VMEM((HEAD_DIM, nc), BF16),
                        pltpu.VMEM((seq // SEL_BLOCK, TQ), F32),
                        pltpu.VMEM((1, L), F32),
                        pltpu.VMEM((1, L), F32),
                        pltpu.VMEM((HEAD_DIM, L), F32),
                        pltpu.VMEM((HEAD_DIM, L), F32)],
        compiler_params=pltpu.CompilerParams(
            dimension_semantics=("arbitrary", "arbitrary", "arbitrary"), vmem_limit_bytes=VMEM_LIMIT),
        name="nsa_attention",
    )(proj, kcmp, vcmp, proj, proj, proj, proj, proj, gt, bias, cmpb, ovt)


def _bucket_of_distance():
    d = np.arange(MAX_DISTANCE)
    max_exact = NUM_BUCKETS // 2
    large = max_exact + (np.log(np.maximum(d, 1).astype(np.float32) / max_exact)
                         / math.log(MAX_DISTANCE / max_exact) * (NUM_BUCKETS - max_exact)).astype(np.int32)
    large = np.minimum(large, NUM_BUCKETS - 1)
    return np.where(d < max_exact, d, large)


CMP_NEAR = 16


def _bias_builder_kernel(vw_ref, vc_ref, far_ref, bias_ref, cmpb_ref, *, nc):
    TQ, TK = ATT_TQ, ATT_TK
    n_win = (WINDOW + TQ) // TK
    for h in range(NSA_HPG):
        lanes = slice(h * TQ, (h + 1) * TQ)
        far = far_ref[h:h + 1, :]
        bias_ref[0:TK, lanes] = jnp.broadcast_to(far, (TK, TQ))
        for w in range(n_win):
            v = vw_ref[h * n_win + w:h * n_win + w + 1, :]
            r = pltpu.roll(jnp.broadcast_to(v, (TK, 2 * TQ)), 0, 1, stride=1, stride_axis=0)
            bias_ref[(w + 1) * TK:(w + 2) * TK, lanes] = r[:, TQ:2 * TQ]
        cmpb_ref[0:nc - CMP_NEAR, lanes] = jnp.broadcast_to(far, (nc - CMP_NEAR, TQ))
        vc = vc_ref[h:h + 1, :]
        rc = pltpu.roll(jnp.broadcast_to(vc, (2 * CMP_NEAR, 4 * TQ)), 0, 1,
                        stride=CMP_STRIDE, stride_axis=0)
        cmpb_ref[nc - CMP_NEAR:nc + CMP_NEAR, lanes] = rc[:, 2 * TQ:3 * TQ]
        cmpb_ref[nc + CMP_NEAR:2 * nc, lanes] = jnp.full((nc - CMP_NEAR, TQ), NEG_INF, F32)


def _bias_tiles(rel_bias, seq):
    TQ, TK = ATT_TQ, ATT_TK
    assert TQ == TK and TQ >= MAX_DISTANCE
    assert CMP_STRIDE * CMP_NEAR >= TQ and 2 * TQ >= CMP_STRIDE * 2 * CMP_NEAR
    assert CMP_STRIDE * (CMP_NEAR + 1) - (CMP_BLOCK - 1) >= MAX_DISTANCE - 1
    nc = seq // CMP_STRIDE
    ng, hpg = NSA_GROUPS, NSA_HPG
    n_win = (WINDOW + TQ) // TK
    tb = rel_bias.astype(F32)[_bucket_of_distance()].T.reshape(ng, hpg, MAX_DISTANCE)

    def by_distance(dist, valid):
        vals = tb[:, :, np.clip(dist, 0, MAX_DISTANCE - 1)]
        return jnp.where(valid, vals, NEG_INF)

    x = np.arange(2 * TQ)[None, :]
    dw = WINDOW - TK * np.arange(n_win)[:, None] - TQ + x
    vw = by_distance(dw, (dw >= 0) & (dw < WINDOW)).reshape(ng, hpg * n_win, 2 * TQ)
    xc = np.arange(4 * TQ)
    dc = xc - 2 * TQ + CMP_STRIDE * CMP_NEAR - (CMP_BLOCK - 1)
    vc = by_distance(dc, dc >= 0)
    far = jnp.broadcast_to(tb[:, :, MAX_DISTANCE - 1:], (ng, hpg, TQ))

    L = hpg * TQ
    return pl.pallas_call(
        functools.partial(_bias_builder_kernel, nc=nc),
        grid=(ng,),
        in_specs=[pl.BlockSpec((None,) + vw.shape[1:], lambda g: (g, 0, 0)),
                  pl.BlockSpec((None,) + vc.shape[1:], lambda g: (g, 0, 0)),
                  pl.BlockSpec((None,) + far.shape[1:], lambda g: (g, 0, 0))],
        out_specs=[pl.BlockSpec((None, TK + WINDOW + TQ, L), lambda g: (g, 0, 0)),
                   pl.BlockSpec((None, 2 * nc, L), lambda g: (g, 0, 0))],
        out_shape=[jax.ShapeDtypeStruct((ng, TK + WINDOW + TQ, L), F32),
                   jax.ShapeDtypeStruct((ng, 2 * nc, L), F32)],
        compiler_params=pltpu.CompilerParams(dimension_semantics=("arbitrary",)),
        name="nsa_bias_tiles",
    )(vw, vc, far)


def _overlap_t(seq):
    nc = seq // CMP_STRIDE
    nsel = seq // SEL_BLOCK
    ci = np.arange(nc)[None, :]
    sj = np.arange(nsel)[:, None]
    ov = (CMP_STRIDE * ci < SEL_BLOCK * (sj + 1)) & (CMP_STRIDE * ci + CMP_BLOCK > SEL_BLOCK * sj)
    ov = ov & (ci < nc - 1)
    return jnp.asarray(ov, BF16)


def kernel(x, norm_gains, final_gain, rel_bias, hgrn_lb, hgrn_w_in, hgrn_head_gain, hgrn_w_out,
           nsa_w_in, nsa_pe_k, nsa_pe_v, nsa_phi_k_w1, nsa_phi_k_b1, nsa_phi_k_w2,
           nsa_phi_v_w1, nsa_phi_v_b1, nsa_phi_v_w2, nsa_w_out):
    batch, seq, d = x.shape
    n = batch * seq
    x2d = x.reshape(n, d)
    qscale = HEAD_DIM ** -0.5

    w_in = hgrn_w_in[0]
    w_in = jnp.concatenate([w_in[:, :d] * qscale, w_in[:, d:]], axis=1).astype(BF16)
    proj = _norm_proj(x2d, norm_gains[0:1], w_in)
    o = _hgrn_recurrence(proj, hgrn_lb, hgrn_head_gain[0:1], batch, seq)
    x1, h1 = _out_proj(o, hgrn_w_out[0].astype(BF16), x2d, norm_gains[1:2], final=False)

    kvw = NSA_GROUPS * HEAD_DIM
    w = nsa_w_in[0]
    gate0 = d + 6 * kvw
    gate1 = gate0 + NSA_HEADS * N_BRANCH
    w_main = jnp.concatenate([w[:, :d] * qscale, w[:, d:gate0], w[:, gate1:]], axis=1).astype(BF16)
    wgt = jnp.zeros((HEAD_DIM, d), F32).at[:gate1 - gate0].set(w[:, gate0:gate1].T).astype(BF16)
    proj, gt = _proj_gate(h1, w_main, wgt)

    blk = {"q": 0}
    for idx, name in enumerate(("k_c", "v_c", "k_s", "v_s", "k_w", "v_w")):
        blk[name] = (d + idx * kvw) // HEAD_DIM
    gw = NSA_HPG * HEAD_DIM
    cols = dict(blk, q=0, z=gate0 // gw)

    kcmp, vcmp = _compress(
        proj, blk["k_c"], blk["v_c"], nsa_pe_k[0], nsa_pe_v[0],
        nsa_phi_k_w1[0].astype(BF16), nsa_phi_v_w1[0].astype(BF16),
        nsa_phi_k_b1[0:1], nsa_phi_v_b1[0:1],
        nsa_phi_k_w2[0].astype(BF16), nsa_phi_v_w2[0].astype(BF16), batch, seq)

    bias, cmpb = _bias_tiles(rel_bias, seq)
    o = _nsa_attention(proj, kcmp, vcmp, gt, bias, cmpb, _overlap_t(seq), cols, batch, seq)
    out, = _out_proj(o, nsa_w_out[0].astype(BF16), x1, final_gain[None, :], final=True)
    return out.reshape(batch, seq, d)
```

```python
import functools
import math

import numpy as np
import jax
import jax.numpy as jnp
from jax import lax
from jax.experimental import pallas as pl
from jax.experimental.pallas import tpu as pltpu

F32 = jnp.float32
BF16 = jnp.bfloat16

D_MODEL = 2048
RMS_EPS = 1e-6
HEAD_DIM = 128

HGRN_HEADS = D_MODEL // HEAD_DIM
HGRN_CHUNK = 64
HGRN_SUB = 16

NSA_HEADS = D_MODEL // HEAD_DIM
NSA_GROUPS = 4
NSA_HPG = NSA_HEADS // NSA_GROUPS
N_BRANCH = 3
CMP_BLOCK = 32
CMP_STRIDE = 16
SEL_BLOCK = 64
N_SELECT = 16
WINDOW = 512
NUM_BUCKETS = 32
MAX_DISTANCE = 128
NEG_INF = -1e30
FORCE_SCORE = 1e9

ATT_TQ = 256
ATT_TK = 256
LOG2E = math.log2(math.e)
V7X_VMEM_BYTES = 64 * 1024 * 1024
VMEM_LIMIT = V7X_VMEM_BYTES * 3 // 4
ATT_VMEM_LIMIT = V7X_VMEM_BYTES * 7 // 8

NT_DIMS = (((1,), (1,)), ((), ()))
TN_DIMS = (((0,), (0,)), ((), ()))


def _sigmoid(x):
    return 1.0 / (1.0 + jnp.exp(-x))


def _split3(a):
    hi = a.astype(BF16)
    r1 = a - hi.astype(F32)
    mid = r1.astype(BF16)
    lo = (r1 - mid.astype(F32)).astype(BF16)
    return hi, mid, lo


def _norm_proj_kernel(x_ref, g_ref, w_ref, o_ref, h_scr):
    @pl.when(pl.program_id(1) == 0)
    def _():
        x = x_ref[...]
        ms = jnp.mean(x * x, axis=-1, keepdims=True)
        h_scr[...] = (x * lax.rsqrt(ms + RMS_EPS) * g_ref[...]).astype(BF16)

    o_ref[...] = jnp.dot(h_scr[...], w_ref[...], preferred_element_type=F32).astype(o_ref.dtype)


def _norm_proj(x, gain, w, *, tm=512, tn=1024):
    n, d = x.shape
    nout = w.shape[1]
    return pl.pallas_call(
        _norm_proj_kernel,
        grid=(n // tm, nout // tn),
        in_specs=[pl.BlockSpec((tm, d), lambda i, j: (i, 0)),
                  pl.BlockSpec((1, d), lambda i, j: (0, 0)),
                  pl.BlockSpec((d, tn), lambda i, j: (0, j))],
        out_specs=pl.BlockSpec((tm, tn), lambda i, j: (i, j)),
        out_shape=jax.ShapeDtypeStruct((n, nout), BF16),
        scratch_shapes=[pltpu.VMEM((tm, d), BF16)],
        compiler_params=pltpu.CompilerParams(
            dimension_semantics=("arbitrary", "arbitrary"), vmem_limit_bytes=VMEM_LIMIT),
        name="norm_proj",
    )(x, gain, w)


def _proj_gate_kernel(h_ref, w_ref, wgt_ref, o_ref, gt_ref):
    @pl.when(pl.program_id(1) == 0)
    def _():
        gt_ref[...] = lax.dot_general(wgt_ref[...], h_ref[...], NT_DIMS, preferred_element_type=F32)

    o_ref[...] = jnp.dot(h_ref[...], w_ref[...], preferred_element_type=F32).astype(o_ref.dtype)


def _proj_gate(h, w, wgt, *, tm=512, tn=1024):
    n, d = h.shape
    nout = w.shape[1]
    ng = wgt.shape[0]
    return pl.pallas_call(
        _proj_gate_kernel,
        grid=(n // tm, nout // tn),
        in_specs=[pl.BlockSpec((tm, d), lambda i, j: (i, 0)),
                  pl.BlockSpec((d, tn), lambda i, j: (0, j)),
                  pl.BlockSpec((ng, d), lambda i, j: (0, 0))],
        out_specs=[pl.BlockSpec((tm, tn), lambda i, j: (i, j)),
                   pl.BlockSpec((ng, tm), lambda i, j: (0, i))],
        out_shape=[jax.ShapeDtypeStruct((n, nout), BF16),
                   jax.ShapeDtypeStruct((ng, n), F32)],
        compiler_params=pltpu.CompilerParams(
            dimension_semantics=("arbitrary", "arbitrary"), vmem_limit_bytes=VMEM_LIMIT),
        name="proj_gate",
    )(h, w, wgt)


def _out_proj_kernel(o_ref, w_ref, res_ref, g_ref, *out_refs, final):
    xn = res_ref[...] + jnp.dot(o_ref[...], w_ref[...], preferred_element_type=F32)
    ms = jnp.mean(xn * xn, axis=-1, keepdims=True)
    y = xn * lax.rsqrt(ms + RMS_EPS) * g_ref[...]
    if final:
        out_refs[0][...] = y
    else:
        out_refs[0][...] = xn
        out_refs[1][...] = y.astype(BF16)


def _out_proj(o, w, res, gain, *, final, tm=256):
    n, d = res.shape
    row = pl.BlockSpec((tm, d), lambda i: (i, 0))
    if final:
        out_specs = [row]
        out_shape = [jax.ShapeDtypeStruct((n, d), F32)]
    else:
        out_specs = [row, row]
        out_shape = [jax.ShapeDtypeStruct((n, d), F32), jax.ShapeDtypeStruct((n, d), BF16)]
    return pl.pallas_call(
        functools.partial(_out_proj_kernel, final=final),
        grid=(n // tm,),
        in_specs=[pl.BlockSpec((tm, o.shape[1]), lambda i: (i, 0)),
                  pl.BlockSpec(w.shape, lambda i: (0, 0)),
                  row,
                  pl.BlockSpec((1, d), lambda i: (0, 0))],
        out_specs=out_specs,
        out_shape=out_shape,
        compiler_params=pltpu.CompilerParams(
            dimension_semantics=("arbitrary",), vmem_limit_bytes=VMEM_LIMIT),
        name="out_proj_final" if final else "out_proj",
    )(o, w, res, gain)


def _hgrn_kernel(q_ref, f_ref, v_ref, z_ref, lbp_ref, hg_ref, o_ref, st_scr, *, n_chunks):
    C, SUB = HGRN_CHUNK, HGRN_SUB

    @pl.when(pl.program_id(2) == 0)
    def _():
        st_scr[...] = jnp.zeros_like(st_scr)

    lbp = lbp_ref[...]
    e = jnp.exp(lbp - jnp.max(lbp, axis=0, keepdims=True))
    lb = e[0:1, :] / jnp.sum(e, axis=0, keepdims=True)
    hg = hg_ref[...]

    rr = lax.broadcasted_iota(jnp.int32, (C, C), 0)
    cc = lax.broadcasted_iota(jnp.int32, (C, C), 1)
    tril = (rr >= cc).astype(BF16)
    sub_c = lax.broadcasted_iota(jnp.int32, (SUB, C), 1)
    row8 = lax.broadcasted_iota(jnp.int32, (8, C), 0)
    col8 = lax.broadcasted_iota(jnp.int32, (8, C), 1)

    def chunk(c, carry):
        rows = pl.ds(pl.multiple_of(c * C, C), C)
        q = q_ref[rows, :].astype(F32)
        fp = f_ref[rows, :].astype(F32)
        v = v_ref[rows, :]
        z = z_ref[rows, :].astype(F32)

        sig = _sigmoid(fp)
        g = jnp.log(lb + (1.0 - lb) * sig)
        k = (1.0 - lb) * (1.0 - sig)

        ghi, gmid, glo = _split3(g)
        b = (jnp.dot(tril, ghi, preferred_element_type=F32)
             + jnp.dot(tril, gmid, preferred_element_type=F32)
             + jnp.dot(tril, glo, preferred_element_type=F32))

        b2 = b * LOG2E

        st = st_scr[...]
        o = lax.dot_general((q * jnp.exp2(b2)).astype(BF16), st.astype(BF16), NT_DIMS,
                            preferred_element_type=F32)

        a_rows = []
        for i in range(C // SUB):
            lo, hi = i * SUB, (i + 1) * SUB
            bq, qq, kk = b2[lo:hi], q[lo:hi], k[lo:hi]
            if i > 0:
                r = b2[lo - 1:lo]
                qi = (qq * jnp.exp2(bq - r)).astype(BF16)
                ki = (k * jnp.exp2(jnp.minimum(r - b2, 0.0))).astype(BF16)
                a = lax.dot_general(qi, ki, NT_DIMS, preferred_element_type=F32)
                a = jnp.where(sub_c < lo, a, 0.0)
            else:
                a = jnp.zeros((SUB, C), F32)
            a8 = [a[j * 8:(j + 1) * 8] for j in range(SUB // 8)]
            for s in range(SUB):
                for j in range(s // 8, SUB // 8):
                    grp = slice(j * 8, (j + 1) * 8)
                    ed = qq[grp] * jnp.exp2(bq[grp] - bq[s:s + 1]) * kk[s:s + 1]
                    col = jnp.sum(ed, axis=-1, keepdims=True)
                    hit = col8 == lo + s
                    if j == s // 8:
                        hit = hit & (row8 >= s - j * 8)
                    a8[j] = jnp.where(hit, col, a8[j])
            a_rows.extend(a8)
        scores = jnp.concatenate(a_rows, axis=0)
        o = o + jnp.dot(scores.astype(BF16), v, preferred_element_type=F32)

        bl = b2[C - 1:C]
        kd = (k * jnp.exp2(bl - b2)).astype(BF16)
        st_scr[...] = st * jnp.exp2(bl) + lax.dot_general(v, kd, TN_DIMS, preferred_element_type=F32)

        ms = jnp.mean(o * o, axis=-1, keepdims=True)
        o = o * lax.rsqrt(ms + RMS_EPS) * hg * (z * _sigmoid(z))
        o_ref[rows, :] = o.astype(o_ref.dtype)
        return carry

    lax.fori_loop(0, n_chunks, chunk, 0, unroll=True)


def _hgrn_recurrence(proj, lb_param, head_gain, batch, seq, *, t_blk=512):
    n = proj.shape[0]
    nh = HGRN_HEADS
    ns = seq // t_blk

    def sec(k):
        return pl.BlockSpec((t_blk, HEAD_DIM), lambda b, h, s, k=k: (b * ns + s, k * nh + h))

    return pl.pallas_call(
        functools.partial(_hgrn_kernel, n_chunks=t_blk // HGRN_CHUNK),
        grid=(batch, nh, ns),
        in_specs=[sec(0), sec(1), sec(2), sec(3),
                  pl.BlockSpec((lb_param.shape[0], HEAD_DIM), lambda b, h, s: (0, h)),
                  pl.BlockSpec((1, HEAD_DIM), lambda b, h, s: (0, 0))],
        out_specs=pl.BlockSpec((t_blk, HEAD_DIM), lambda b, h, s: (b * ns + s, h)),
        out_shape=jax.ShapeDtypeStruct((n, nh * HEAD_DIM), BF16),
        scratch_shapes=[pltpu.VMEM((HEAD_DIM, HEAD_DIM), F32)],
        compiler_params=pltpu.CompilerParams(
            dimension_semantics=("arbitrary", "arbitrary", "arbitrary")),
        name="hgrn_recurrence",
    )(proj, proj, proj, proj, lb_param, head_gain)


def _kv_prep_kernel(kc_ref, vc_ref, vs_ref, vw_ref, pek_ref, pev_ref, w1k_ref, w1v_ref,
                    b1k_ref, b1v_ref, w2k_ref, w2v_ref, ko_ref, vto_ref, vts_ref, vtw_ref, x_scr,
                    *, nc, ntile):
    half = CMP_BLOCK // 2
    TK = ATT_TK

    def phi(src_ref, pe_ref, w1_ref, b1_ref, w2_ref):
        x_scr[...] = src_ref[...].astype(F32)
        top = jnp.zeros((nc, HEAD_DIM), F32)
        bot = jnp.zeros((nc, HEAD_DIM), F32)
        for p in range(CMP_STRIDE):
            xp = x_scr[pl.ds(p, nc, stride=CMP_STRIDE), :]
            xa = (xp + pe_ref[p:p + 1, :]).astype(BF16)
            xb = (xp + pe_ref[half + p:half + p + 1, :]).astype(BF16)
            top = top + jnp.dot(xa, w1_ref[p * HEAD_DIM:(p + 1) * HEAD_DIM, :],
                                preferred_element_type=F32)
            bot = bot + jnp.dot(xb, w1_ref[(half + p) * HEAD_DIM:(half + p + 1) * HEAD_DIM, :],
                                preferred_element_type=F32)
        hid = top + pltpu.roll(bot, nc - 1, 0) + b1_ref[...]
        act = jax.nn.gelu(hid, approximate=True)
        return jnp.dot(act.astype(BF16), w2_ref[...], preferred_element_type=F32)

    ko_ref[...] = phi(kc_ref, pek_ref, w1k_ref, b1k_ref, w2k_ref).astype(BF16)
    vto_ref[...] = phi(vc_ref, pev_ref, w1v_ref, b1v_ref, w2v_ref).T.astype(BF16)

    def tr(u, carry):
        rows = pl.ds(pl.multiple_of(u * TK, TK), TK)
        vts_ref[u] = vs_ref[rows, :].astype(F32).T.astype(BF16)
        vtw_ref[u] = vw_ref[rows, :].astype(F32).T.astype(BF16)
        return carry
    lax.fori_loop(0, ntile, tr, 0)


def _kv_prep(proj, blk, pe_k, pe_v, w1k, w1v, b1k, b1v, w2k, w2v, batch, seq):
    g = NSA_GROUPS
    nc = seq // CMP_STRIDE
    ntile = seq // ATT_TK

    def full(a):
        return pl.BlockSpec(a.shape, lambda b, gg: (0,) * a.ndim)

    def seq_block(col0):
        return pl.BlockSpec((seq, HEAD_DIM), lambda b, gg: (b, col0 + gg))

    def out(*shape):
        spec = pl.BlockSpec((None, None) + shape, lambda b, gg: (b, gg) + (0,) * len(shape))
        return spec, jax.ShapeDtypeStruct((batch, g) + shape, BF16)

    specs, shapes = zip(out(nc, HEAD_DIM), out(HEAD_DIM, nc),
                        out(ntile, HEAD_DIM, ATT_TK), out(ntile, HEAD_DIM, ATT_TK))
    return pl.pallas_call(
        functools.partial(_kv_prep_kernel, nc=nc, ntile=ntile),
        grid=(batch, g),
        in_specs=[seq_block(blk["k_c"]), seq_block(blk["v_c"]), seq_block(blk["v_s"]),
                  seq_block(blk["v_w"]),
                  full(pe_k), full(pe_v), full(w1k), full(w1v), full(b1k), full(b1v),
                  full(w2k), full(w2v)],
        out_specs=list(specs),
        out_shape=list(shapes),
        scratch_shapes=[pltpu.VMEM((seq, HEAD_DIM), F32)],
        compiler_params=pltpu.CompilerParams(dimension_semantics=("arbitrary", "arbitrary")),
        name="nsa_kv_prep",
    )(proj, proj, proj, proj, pe_k, pe_v, w1k, w1v, b1k, b1v, w2k, w2v)


def _nsa_attn_kernel(q_ref, kc_ref, vtc_ref, ks_ref, vts_ref, kw_ref, vtw_ref, z_ref, gt_ref,
                     win_ref, cmpb_ref, far_ref, ovt_ref, o_ref,
                     q_scr, selm_scr, selb_scr, s_scr, m8_scr, l8_scr, acc_scr, tot_scr, *, seq):
    TQ, TK = ATT_TQ, ATT_TK
    L = NSA_HPG * TQ
    nc = seq // CMP_STRIDE
    nsel = seq // SEL_BLOCK
    blk_per_tile = TK // SEL_BLOCK
    n_win = (WINDOW + TQ) // TK
    g = pl.program_id(1)
    i = pl.program_id(2)
    t0 = i * TQ

    for h in range(NSA_HPG):
        q_scr[h * TQ:(h + 1) * TQ, :] = q_ref[:, h * HEAD_DIM:(h + 1) * HEAD_DIM]

    def gate_row(c):
        rows = [(g * NSA_HPG + h) * N_BRANCH + c for h in range(NSA_HPG)]
        return jnp.concatenate([_sigmoid(gt_ref[pl.ds(r, 1), :]) for r in rows], axis=1)

    cmp_row0 = pl.multiple_of(nc - i * (TQ // CMP_STRIDE), 8)
    s = lax.dot_general(kc_ref[...], q_scr[...], NT_DIMS, preferred_element_type=F32)
    s = s + cmpb_ref[pl.ds(cmp_row0, nc), :]
    mc = jnp.max(s, axis=0, keepdims=True)
    p = jnp.exp2(s - mc)
    inv = jnp.where(mc > 0.1 * NEG_INF, 1.0 / jnp.sum(p, axis=0, keepdims=True), 0.0)
    p = p * inv
    psum = p[:, 0:TQ]
    for h in range(1, NSA_HPG):
        psum = psum + p[:, h * TQ:(h + 1) * TQ]
    oc = jnp.dot(vtc_ref[...], p.astype(BF16), preferred_element_type=F32)
    tot_scr[...] = oc * gate_row(0)

    ovt = ovt_ref[...]
    phi, pmid, plo = _split3(psum)
    imp = (jnp.dot(ovt, phi, preferred_element_type=F32)
           + jnp.dot(ovt, pmid, preferred_element_type=F32)
           + jnp.dot(ovt, plo, preferred_element_type=F32))

    jrow = lax.broadcasted_iota(jnp.int32, (nsel, TQ), 0)
    tcol = t0 + lax.broadcasted_iota(jnp.int32, (nsel, TQ), 1)
    cur = lax.shift_right_arithmetic(tcol, int(math.log2(SEL_BLOCK)))
    forced = (jrow == 0) | (jrow == cur) | (jrow == cur - 1)
    visible = jrow * SEL_BLOCK <= tcol
    score = jnp.where(forced, FORCE_SCORE, jnp.where(visible, imp, NEG_INF))
    row8 = lax.broadcasted_iota(jnp.int32, (8, TQ), 0)
    score8 = [score[r * 8:(r + 1) * 8] for r in range(nsel // 8)]
    rank8 = [jnp.zeros((8, TQ), jnp.int32) for _ in score8]
    for jp in range(nsel):
        row = score[jp:jp + 1, :]
        for r, sc in enumerate(score8):
            if r * 8 > jp:
                ahead = row >= sc
            elif r * 8 + 7 < jp:
                ahead = row > sc
            else:
                ahead = (row > sc) | ((row == sc) & (row8 > jp - r * 8))
            rank8[r] = rank8[r] + ahead.astype(jnp.int32)
    rank = jnp.concatenate(rank8, axis=0)
    selm = jnp.where(rank < min(N_SELECT, nsel), 0.0, NEG_INF)
    selm = jnp.concatenate([selm] * NSA_HPG, axis=1)
    selm_scr[...] = selm
    selb_scr[...] = selm + far_ref[...]

    def step_rows(u0, nt):
        return pl.ds(pl.multiple_of(u0 * TK, TK), nt * TK)

    def logits_step(k_ref, u0, nt, win_row0, sel_scr):
        s = lax.dot_general(k_ref[step_rows(u0, nt), :], q_scr[...], NT_DIMS,
                            preferred_element_type=F32)
        if win_row0 is not None:
            s = s + win_ref[win_row0:win_row0 + nt * TK, :]
        if sel_scr is not None:
            parts = [s[r * SEL_BLOCK:(r + 1) * SEL_BLOCK]
                     + sel_scr[pl.ds(u0 * blk_per_tile + r, 1), :] for r in range(nt * blk_per_tile)]
            s = jnp.concatenate(parts, axis=0)
        s_scr[step_rows(u0, nt), :] = s
        m8_scr[...] = jnp.maximum(m8_scr[...], jnp.max(s.reshape(nt * TK // 8, 8, L), axis=0))

    def pv_step(vt_ref, m, u0, nt):
        p = jnp.exp2(s_scr[step_rows(u0, nt), :] - m)
        vt = jnp.concatenate([vt_ref[u0 + j] for j in range(nt)], axis=1)
        acc_scr[...] += jnp.dot(vt, p.astype(BF16), preferred_element_type=F32)
        l8_scr[...] += jnp.sum(p.reshape(nt * TK // 8, 8, L), axis=0)

    def over_tiles(u_lo, n, step):
        n4 = lax.shift_right_logical(n, 2)

        def body(j, carry):
            step(u_lo + 4 * j, 4)
            return carry
        lax.fori_loop(0, n4, body, 0)
        u2 = u_lo + 4 * n4
        pl.when((n & 2) != 0)(lambda: step(u2, 2))
        pl.when((n & 1) != 0)(lambda: step(u2 + (n & 2), 1))

    def near_diagonal(step):
        for nt in range(1, n_win + 1):
            cond = (i == nt - 1) if nt < n_win else (i >= nt - 1)
            pl.when(cond)(functools.partial(step, i - (nt - 1), nt, (n_win - nt) * TK))

    def finish(c):
        w = gate_row(c) * (1.0 / jnp.sum(l8_scr[...], axis=0, keepdims=True))
        tot_scr[...] = tot_scr[...] + acc_scr[...] * w

    def reset():
        m8_scr[...] = jnp.full_like(m8_scr, NEG_INF)
        l8_scr[...] = jnp.zeros_like(l8_scr)
        acc_scr[...] = jnp.zeros_like(acc_scr)

    reset()
    n_far = jnp.maximum(i - 1, 0)
    over_tiles(0, n_far, lambda u0, nt: logits_step(ks_ref, u0, nt, None, selb_scr))
    pl.when(i >= 1)(lambda: logits_step(ks_ref, i - 1, 2, (n_win - 2) * TK, selm_scr))
    pl.when(i == 0)(lambda: logits_step(ks_ref, 0, 1, (n_win - 1) * TK, selm_scr))
    m_sel = jnp.max(m8_scr[...], axis=0, keepdims=True)
    over_tiles(0, i + 1, functools.partial(pv_step, vts_ref, m_sel))
    finish(1)

    reset()
    near_diagonal(lambda u0, nt, row0: logits_step(kw_ref, u0, nt, row0, None))
    m_win = jnp.max(m8_scr[...], axis=0, keepdims=True)
    near_diagonal(lambda u0, nt, row0: pv_step(vtw_ref, m_win, u0, nt))
    finish(2)

    for h in range(NSA_HPG):
        cols = slice(h * HEAD_DIM, (h + 1) * HEAD_DIM)
        z = z_ref[:, cols].astype(F32)
        o_ref[:, cols] = (tot_scr[:, h * TQ:(h + 1) * TQ].T * (z * _sigmoid(z))).astype(o_ref.dtype)


def _nsa_attention(proj, kcmp, vtc, vts, vtw, gt, win, cmpb, far, ovt, cols, batch, seq):
    TQ, TK = ATT_TQ, ATT_TK
    n = proj.shape[0]
    ng = NSA_GROUPS
    nq = seq // TQ
    L = NSA_HPG * TQ
    gw = NSA_HPG * HEAD_DIM

    def seq_block(col0):
        return pl.BlockSpec((seq, HEAD_DIM), lambda b, g, i: (b, col0 + g))

    def per_bg(a):
        return pl.BlockSpec((None, None) + a.shape[2:], lambda b, g, i: (b, g) + (0,) * (a.ndim - 2))

    def per_g(a):
        return pl.BlockSpec((None,) + a.shape[1:], lambda b, g, i: (g,) + (0,) * (a.ndim - 1))

    return pl.pallas_call(
        functools.partial(_nsa_attn_kernel, seq=seq),
        grid=(batch, ng, nq),
        in_specs=[pl.BlockSpec((TQ, gw), lambda b, g, i: (b * nq + i, cols["q"] + g)),
                  per_bg(kcmp), per_bg(vtc),
                  seq_block(cols["k_s"]), per_bg(vts),
                  seq_block(cols["k_w"]), per_bg(vtw),
                  pl.BlockSpec((TQ, gw), lambda b, g, i: (b * nq + i, cols["z"] + g)),
                  pl.BlockSpec((gt.shape[0], TQ), lambda b, g, i: (0, b * nq + i)),
                  per_g(win), per_g(cmpb), per_g(far),
                  pl.BlockSpec(ovt.shape, lambda b, g, i: (0, 0))],
        out_specs=pl.BlockSpec((TQ, gw), lambda b, g, i: (b * nq + i, g)),
        out_shape=jax.ShapeDtypeStruct((n, ng * gw), BF16),
        scratch_shapes=[pltpu.VMEM((L, HEAD_DIM), BF16),
                        pltpu.VMEM((seq // SEL_BLOCK, L), F32),
                        pltpu.VMEM((seq // SEL_BLOCK, L), F32),
                        pltpu.VMEM((seq, L), F32),
                        pltpu.VMEM((8, L), F32),
                        pltpu.VMEM((8, L), F32),
                        pltpu.VMEM((HEAD_DIM, L), F32),
                        pltpu.VMEM((HEAD_DIM, L), F32)],
        compiler_params=pltpu.CompilerParams(
            dimension_semantics=("arbitrary", "arbitrary", "arbitrary"),
            vmem_limit_bytes=ATT_VMEM_LIMIT),
        name="nsa_attention",
    )(proj, kcmp, vtc, proj, vts, proj, vtw, proj, gt, win, cmpb, far, ovt)


def _bucket_of_distance():
    d = np.arange(MAX_DISTANCE)
    max_exact = NUM_BUCKETS // 2
    large = max_exact + (np.log(np.maximum(d, 1).astype(np.float32) / max_exact)
                         / math.log(MAX_DISTANCE / max_exact) * (NUM_BUCKETS - max_exact)).astype(np.int32)
    large = np.minimum(large, NUM_BUCKETS - 1)
    return np.where(d < max_exact, d, large)


CMP_NEAR = 16


def _bias_builder_kernel(vw_ref, vc_ref, far_ref, win_ref, cmpb_ref, *, nc):
    TQ, TK = ATT_TQ, ATT_TK
    n_win = (WINDOW + TQ) // TK
    for h in range(NSA_HPG):
        lanes = slice(h * TQ, (h + 1) * TQ)
        for w in range(n_win):
            v = vw_ref[h * n_win + w:h * n_win + w + 1, :]
            r = pltpu.roll(jnp.broadcast_to(v, (TK, 2 * TQ)), 0, 1, stride=1, stride_axis=0)
            win_ref[w * TK:(w + 1) * TK, lanes] = r[:, TQ:2 * TQ]
        cmpb_ref[0:nc - CMP_NEAR, lanes] = jnp.broadcast_to(far_ref[:, lanes], (nc - CMP_NEAR, TQ))
        vc = vc_ref[h:h + 1, :]
        rc = pltpu.roll(jnp.broadcast_to(vc, (2 * CMP_NEAR, 4 * TQ)), 0, 1,
                        stride=CMP_STRIDE, stride_axis=0)
        cmpb_ref[nc - CMP_NEAR:nc + CMP_NEAR, lanes] = rc[:, 2 * TQ:3 * TQ]
        cmpb_ref[nc + CMP_NEAR:2 * nc, lanes] = jnp.full((nc - CMP_NEAR, TQ), NEG_INF, F32)


def _bias_tiles(rel_bias, seq):
    TQ, TK = ATT_TQ, ATT_TK
    assert TQ == TK and TQ >= MAX_DISTANCE
    assert CMP_STRIDE * CMP_NEAR >= TQ and 2 * TQ >= CMP_STRIDE * 2 * CMP_NEAR
    assert CMP_STRIDE * (CMP_NEAR + 1) - (CMP_BLOCK - 1) >= MAX_DISTANCE - 1
    nc = seq // CMP_STRIDE
    ng, hpg = NSA_GROUPS, NSA_HPG
    n_win = (WINDOW + TQ) // TK
    tb = (rel_bias.astype(F32) * LOG2E)[_bucket_of_distance()].T.reshape(ng, hpg, MAX_DISTANCE)

    def by_distance(dist, valid):
        vals = tb[:, :, np.clip(dist, 0, MAX_DISTANCE - 1)]
        return jnp.where(valid, vals, NEG_INF)

    x = np.arange(2 * TQ)[None, :]
    dw = WINDOW - TK * np.arange(n_win)[:, None] - TQ + x
    vw = by_distance(dw, (dw >= 0) & (dw < WINDOW)).reshape(ng, hpg * n_win, 2 * TQ)
    xc = np.arange(4 * TQ)
    dc = xc - 2 * TQ + CMP_STRIDE * CMP_NEAR - (CMP_BLOCK - 1)
    vc = by_distance(dc, dc >= 0)
    L = hpg * TQ
    far = jnp.broadcast_to(tb[:, :, MAX_DISTANCE - 1:], (ng, hpg, TQ)).reshape(ng, 1, L)

    win, cmpb = pl.pallas_call(
        functools.partial(_bias_builder_kernel, nc=nc),
        grid=(ng,),
        in_specs=[pl.BlockSpec((None,) + vw.shape[1:], lambda g: (g, 0, 0)),
                  pl.BlockSpec((None,) + vc.shape[1:], lambda g: (g, 0, 0)),
                  pl.BlockSpec((None,) + far.shape[1:], lambda g: (g, 0, 0))],
        out_specs=[pl.BlockSpec((None, WINDOW + TQ, L), lambda g: (g, 0, 0)),
                   pl.BlockSpec((None, 2 * nc, L), lambda g: (g, 0, 0))],
        out_shape=[jax.ShapeDtypeStruct((ng, WINDOW + TQ, L), F32),
                   jax.ShapeDtypeStruct((ng, 2 * nc, L), F32)],
        compiler_params=pltpu.CompilerParams(dimension_semantics=("arbitrary",)),
        name="nsa_bias_tiles",
    )(vw, vc, far)
    return win, cmpb, far


def _overlap_t(seq):
    nc = seq // CMP_STRIDE
    nsel = seq // SEL_BLOCK
    ci = np.arange(nc)[None, :]
    sj = np.arange(nsel)[:, None]
    ov = (CMP_STRIDE * ci < SEL_BLOCK * (sj + 1)) & (CMP_STRIDE * ci + CMP_BLOCK > SEL_BLOCK * sj)
    ov = ov & (ci < nc - 1)
    return jnp.asarray(ov, BF16)


def kernel(x, norm_gains, final_gain, rel_bias, hgrn_lb, hgrn_w_in, hgrn_head_gain, hgrn_w_out,
           nsa_w_in, nsa_pe_k, nsa_pe_v, nsa_phi_k_w1, nsa_phi_k_b1, nsa_phi_k_w2,
           nsa_phi_v_w1, nsa_phi_v_b1, nsa_phi_v_w2, nsa_w_out):
    batch, seq, d = x.shape
    n = batch * seq
    x2d = x.reshape(n, d)
    qscale = HEAD_DIM ** -0.5

    w_in = hgrn_w_in[0]
    w_in = jnp.concatenate([w_in[:, :d] * qscale, w_in[:, d:]], axis=1).astype(BF16)
    proj = _norm_proj(x2d, norm_gains[0:1], w_in)
    o = _hgrn_recurrence(proj, hgrn_lb, hgrn_head_gain[0:1], batch, seq)
    x1, h1 = _out_proj(o, hgrn_w_out[0].astype(BF16), x2d, norm_gains[1:2], final=False)

    kvw = NSA_GROUPS * HEAD_DIM
    w = nsa_w_in[0]
    gate0 = d + 6 * kvw
    gate1 = gate0 + NSA_HEADS * N_BRANCH
    w_main = jnp.concatenate([w[:, :d] * (qscale * LOG2E), w[:, d:gate0], w[:, gate1:]],
                             axis=1).astype(BF16)
    wgt = jnp.zeros((HEAD_DIM, d), F32).at[:gate1 - gate0].set(w[:, gate0:gate1].T).astype(BF16)
    proj, gt = _proj_gate(h1, w_main, wgt)

    blk = {}
    for idx, name in enumerate(("k_c", "v_c", "k_s", "v_s", "k_w", "v_w")):
        blk[name] = (d + idx * kvw) // HEAD_DIM
    gw = NSA_HPG * HEAD_DIM
    cols = dict(blk, q=0, z=gate0 // gw)

    kcmp, vtc, vts, vtw = _kv_prep(
        proj, blk, nsa_pe_k[0], nsa_pe_v[0],
        nsa_phi_k_w1[0].astype(BF16), nsa_phi_v_w1[0].astype(BF16),
        nsa_phi_k_b1[0:1], nsa_phi_v_b1[0:1],
        nsa_phi_k_w2[0].astype(BF16), nsa_phi_v_w2[0].astype(BF16), batch, seq)

    win, cmpb, far = _bias_tiles(rel_bias, seq)
    o = _nsa_attention(proj, kcmp, vtc, vts, vtw, gt, win, cmpb, far, _overlap_t(seq), cols,
                       batch, seq)
    out, = _out_proj(o, nsa_w_out[0].astype(BF16), x1, final_gain[None, :], final=True)
    return out.reshape(batch, seq, d)
```

```python
import functools
import math

import numpy as np
import jax
import jax.numpy as jnp
from jax import lax
from jax.experimental import pallas as pl
from jax.experimental.pallas import tpu as pltpu

F32 = jnp.float32
BF16 = jnp.bfloat16

D_MODEL = 2048
RMS_EPS = 1e-6
HEAD_DIM = 128

HGRN_HEADS = D_MODEL // HEAD_DIM
HGRN_CHUNK = 64
HGRN_SUB = 16
HGRN_SAFE_LOG2 = 96.0

NSA_HEADS = D_MODEL // HEAD_DIM
NSA_GROUPS = 4
NSA_HPG = NSA_HEADS // NSA_GROUPS
N_BRANCH = 3
CMP_BLOCK = 32
CMP_STRIDE = 16
SEL_BLOCK = 64
N_SELECT = 16
WINDOW = 512
NUM_BUCKETS = 32
MAX_DISTANCE = 128
NEG_INF = -1e30
FORCE_SCORE = 1e9

ATT_TQ = 256
ATT_TK = 256
LOG2E = math.log2(math.e)
V7X_VMEM_BYTES = 64 * 1024 * 1024
VMEM_LIMIT = V7X_VMEM_BYTES * 3 // 4
ATT_VMEM_LIMIT = V7X_VMEM_BYTES * 7 // 8

NT_DIMS = (((1,), (1,)), ((), ()))
TN_DIMS = (((0,), (0,)), ((), ()))


def _sigmoid(x):
    return 1.0 / (1.0 + jnp.exp(-x))


def _split3(a):
    hi = a.astype(BF16)
    r1 = a - hi.astype(F32)
    mid = r1.astype(BF16)
    lo = (r1 - mid.astype(F32)).astype(BF16)
    return hi, mid, lo


def _norm_proj_kernel(x_ref, g_ref, w_ref, o_ref, h_scr):
    @pl.when(pl.program_id(1) == 0)
    def _():
        x = x_ref[...]
        ms = jnp.mean(x * x, axis=-1, keepdims=True)
        h_scr[...] = (x * lax.rsqrt(ms + RMS_EPS) * g_ref[...]).astype(BF16)

    o_ref[...] = jnp.dot(h_scr[...], w_ref[...], preferred_element_type=F32).astype(o_ref.dtype)


def _norm_proj(x, gain, w, *, tm=1024, tn=1024):
    n, d = x.shape
    nout = w.shape[1]
    return pl.pallas_call(
        _norm_proj_kernel,
        grid=(n // tm, nout // tn),
        in_specs=[pl.BlockSpec((tm, d), lambda i, j: (i, 0)),
                  pl.BlockSpec((1, d), lambda i, j: (0, 0)),
                  pl.BlockSpec((d, tn), lambda i, j: (0, j))],
        out_specs=pl.BlockSpec((tm, tn), lambda i, j: (i, j)),
        out_shape=jax.ShapeDtypeStruct((n, nout), BF16),
        scratch_shapes=[pltpu.VMEM((tm, d), BF16)],
        compiler_params=pltpu.CompilerParams(
            dimension_semantics=("arbitrary", "arbitrary"), vmem_limit_bytes=VMEM_LIMIT),
        name="norm_proj",
    )(x, gain, w)


def _proj_gate_kernel(h_ref, w_ref, wgt_ref, o_ref, gt_ref):
    @pl.when(pl.program_id(1) == 0)
    def _():
        gt_ref[...] = lax.dot_general(wgt_ref[...], h_ref[...], NT_DIMS, preferred_element_type=F32)

    o_ref[...] = jnp.dot(h_ref[...], w_ref[...], preferred_element_type=F32).astype(o_ref.dtype)


def _proj_gate(h, w, wgt, *, tm=1024, tn=1024):
    n, d = h.shape
    nout = w.shape[1]
    ng = wgt.shape[0]
    return pl.pallas_call(
        _proj_gate_kernel,
        grid=(n // tm, nout // tn),
        in_specs=[pl.BlockSpec((tm, d), lambda i, j: (i, 0)),
                  pl.BlockSpec((d, tn), lambda i, j: (0, j)),
                  pl.BlockSpec((ng, d), lambda i, j: (0, 0))],
        out_specs=[pl.BlockSpec((tm, tn), lambda i, j: (i, j)),
                   pl.BlockSpec((ng, tm), lambda i, j: (0, i))],
        out_shape=[jax.ShapeDtypeStruct((n, nout), BF16),
                   jax.ShapeDtypeStruct((ng, n), F32)],
        compiler_params=pltpu.CompilerParams(
            dimension_semantics=("arbitrary", "arbitrary"), vmem_limit_bytes=VMEM_LIMIT),
        name="proj_gate",
    )(h, w, wgt)


def _out_proj_kernel(o_ref, w_ref, res_ref, g_ref, *out_refs, final):
    xn = res_ref[...] + jnp.dot(o_ref[...], w_ref[...], preferred_element_type=F32)
    ms = jnp.mean(xn * xn, axis=-1, keepdims=True)
    y = xn * lax.rsqrt(ms + RMS_EPS) * g_ref[...]
    if final:
        out_refs[0][...] = y
    else:
        out_refs[0][...] = xn
        out_refs[1][...] = y.astype(BF16)


def _out_proj(o, w, res, gain, *, final, tm=512):
    n, d = res.shape
    row = pl.BlockSpec((tm, d), lambda i: (i, 0))
    if final:
        out_specs = [row]
        out_shape = [jax.ShapeDtypeStruct((n, d), F32)]
    else:
        out_specs = [row, row]
        out_shape = [jax.ShapeDtypeStruct((n, d), F32), jax.ShapeDtypeStruct((n, d), BF16)]
    return pl.pallas_call(
        functools.partial(_out_proj_kernel, final=final),
        grid=(n // tm,),
        in_specs=[pl.BlockSpec((tm, o.shape[1]), lambda i: (i, 0)),
                  pl.BlockSpec(w.shape, lambda i: (0, 0)),
                  row,
                  pl.BlockSpec((1, d), lambda i: (0, 0))],
        out_specs=out_specs,
        out_shape=out_shape,
        compiler_params=pltpu.CompilerParams(
            dimension_semantics=("arbitrary",), vmem_limit_bytes=VMEM_LIMIT),
        name="out_proj_final" if final else "out_proj",
    )(o, w, res, gain)


def _hgrn_kernel(q_ref, f_ref, v_ref, z_ref, lbp_ref, hg_ref, o_ref,
                 st_scr, k_scr, b_scr, qe_scr, oi_scr, u_scr, d_scr, stb_scr, *, n_chunks):
    C, SUB = HGRN_CHUNK, HGRN_SUB

    @pl.when(pl.program_id(2) == 0)
    def _():
        st_scr[...] = jnp.zeros_like(st_scr)

    lbp = lbp_ref[...]
    e = jnp.exp(lbp - jnp.max(lbp, axis=0, keepdims=True))
    lb = e[0:1, :] / jnp.sum(e, axis=0, keepdims=True)
    hg = hg_ref[...]

    rr = lax.broadcasted_iota(jnp.int32, (C, C), 0)
    cc = lax.broadcasted_iota(jnp.int32, (C, C), 1)
    causal = rr >= cc
    tril = causal.astype(BF16)
    sub_c = lax.broadcasted_iota(jnp.int32, (SUB, C), 1)
    row8 = lax.broadcasted_iota(jnp.int32, (8, C), 0)
    col8 = lax.broadcasted_iota(jnp.int32, (8, C), 1)

    def emit(rows, o):
        z = z_ref[rows, :].astype(F32)
        ms = jnp.mean(o * o, axis=-1, keepdims=True)
        o_ref[rows, :] = (o * lax.rsqrt(ms + RMS_EPS) * hg * (z * _sigmoid(z))).astype(o_ref.dtype)

    chunks = [slice(c * C, (c + 1) * C) for c in range(n_chunks)]
    ks, gsplit = [], []
    for rows in chunks:
        sig = _sigmoid(f_ref[rows, :].astype(F32))
        gsplit.append(_split3(jnp.log(lb + (1.0 - lb) * sig)))
        ks.append((1.0 - lb) * (1.0 - sig))
        k_scr[rows, :] = ks[-1]
    b2s = []
    for rows, (ghi, gmid, glo) in zip(chunks, gsplit):
        b = (jnp.dot(tril, ghi, preferred_element_type=F32)
             + jnp.dot(tril, gmid, preferred_element_type=F32)
             + jnp.dot(tril, glo, preferred_element_type=F32))
        b2s.append(b * LOG2E)
        b_scr[rows, :] = b2s[-1]
    span = jnp.zeros((1, HEAD_DIM), F32)
    for b2 in b2s:
        span = jnp.maximum(span, -b2[C - 1:C])
    safe = jnp.max(span) < HGRN_SAFE_LOG2

    qes, kes = [], []
    for rows, k, b2 in zip(chunks, ks, b2s):
        qes.append((q_ref[rows, :].astype(F32) * jnp.exp2(b2)).astype(BF16))
        kes.append(k * jnp.exp2(-b2))
        qe_scr[rows, :] = qes[-1]
    scores = [lax.dot_general(qe, ke.astype(BF16), NT_DIMS, preferred_element_type=F32)
              for qe, ke in zip(qes, kes)]
    vt = v_ref[...].astype(F32).T.astype(BF16)
    zero = jnp.zeros((C, HEAD_DIM), BF16)
    for c, (rows, sc, ke, b2) in enumerate(zip(chunks, scores, kes, b2s)):
        bl = b2[C - 1:C]
        oi_scr[rows, :] = jnp.dot(jnp.where(causal, sc, 0.0).astype(BF16), v_ref[rows, :],
                                  preferred_element_type=F32)
        kd = (ke * jnp.exp2(bl)).astype(BF16)
        pair = slice(c // 2 * 2 * C, (c // 2 + 1) * 2 * C)
        u_scr[c] = jnp.dot(vt[:, pair], jnp.concatenate([kd, zero] if c % 2 == 0 else [zero, kd]),
                           preferred_element_type=F32)
        d_scr[c:c + 1, :] = jnp.exp2(bl)

    @pl.when(safe)
    def _():
        st = st_scr[...]
        for c in range(n_chunks):
            stb_scr[c] = st.astype(BF16)
            st = st * d_scr[c:c + 1, :] + u_scr[c]
        st_scr[...] = st
        carried = [lax.dot_general(qe_scr[rows, :], stb_scr[c], NT_DIMS, preferred_element_type=F32)
                   for c, rows in enumerate(chunks)]
        for rows, oc in zip(chunks, carried):
            emit(rows, oi_scr[rows, :] + oc)

    def robust_chunk(c):
        rows = slice(c * C, (c + 1) * C)
        q = q_ref[rows, :].astype(F32)
        v = v_ref[rows, :]
        k, b2 = k_scr[rows, :], b_scr[rows, :]
        st = st_scr[...]
        o = lax.dot_general((q * jnp.exp2(b2)).astype(BF16), st.astype(BF16), NT_DIMS,
                            preferred_element_type=F32)
        a_rows = []
        for i in range(C // SUB):
            lo, hi = i * SUB, (i + 1) * SUB
            bq, qq, kk = b2[lo:hi], q[lo:hi], k[lo:hi]
            if i > 0:
                r = b2[lo - 1:lo]
                qi = (qq * jnp.exp2(bq - r)).astype(BF16)
                ki = (k * jnp.exp2(jnp.minimum(r - b2, 0.0))).astype(BF16)
                a = lax.dot_general(qi, ki, NT_DIMS, preferred_element_type=F32)
                a = jnp.where(sub_c < lo, a, 0.0)
            else:
                a = jnp.zeros((SUB, C), F32)
            a8 = [a[j * 8:(j + 1) * 8] for j in range(SUB // 8)]
            for s in range(SUB):
                for j in range(s // 8, SUB // 8):
                    grp = slice(j * 8, (j + 1) * 8)
                    ed = qq[grp] * jnp.exp2(bq[grp] - bq[s:s + 1]) * kk[s:s + 1]
                    col = jnp.sum(ed, axis=-1, keepdims=True)
                    hit = col8 == lo + s
                    if j == s // 8:
                        hit = hit & (row8 >= s - j * 8)
                    a8[j] = jnp.where(hit, col, a8[j])
            a_rows.extend(a8)
        scores = jnp.concatenate(a_rows, axis=0)
        o = o + jnp.dot(scores.astype(BF16), v, preferred_element_type=F32)
        bl = b2[C - 1:C]
        st_scr[...] = st * jnp.exp2(bl) + lax.dot_general(
            v, (k * jnp.exp2(bl - b2)).astype(BF16), TN_DIMS, preferred_element_type=F32)
        emit(rows, o)

    @pl.when(jnp.logical_not(safe))
    def _():
        for c in range(n_chunks):
            robust_chunk(c)


def _hgrn_recurrence(proj, lb_param, head_gain, batch, seq, *, t_blk=1024):
    n = proj.shape[0]
    nh = HGRN_HEADS
    ns = seq // t_blk

    def sec(k):
        return pl.BlockSpec((t_blk, HEAD_DIM), lambda b, h, s, k=k: (b * ns + s, k * nh + h))

    n_chunks = t_blk // HGRN_CHUNK
    return pl.pallas_call(
        functools.partial(_hgrn_kernel, n_chunks=n_chunks),
        grid=(batch, nh, ns),
        in_specs=[sec(0), sec(1), sec(2), sec(3),
                  pl.BlockSpec((lb_param.shape[0], HEAD_DIM), lambda b, h, s: (0, h)),
                  pl.BlockSpec((1, HEAD_DIM), lambda b, h, s: (0, 0))],
        out_specs=pl.BlockSpec((t_blk, HEAD_DIM), lambda b, h, s: (b * ns + s, h)),
        out_shape=jax.ShapeDtypeStruct((n, nh * HEAD_DIM), BF16),
        scratch_shapes=[pltpu.VMEM((HEAD_DIM, HEAD_DIM), F32),
                        pltpu.VMEM((t_blk, HEAD_DIM), F32),
                        pltpu.VMEM((t_blk, HEAD_DIM), F32),
                        pltpu.VMEM((t_blk, HEAD_DIM), BF16),
                        pltpu.VMEM((t_blk, HEAD_DIM), F32),
                        pltpu.VMEM((n_chunks, HEAD_DIM, HEAD_DIM), F32),
                        pltpu.VMEM((n_chunks, HEAD_DIM), F32),
                        pltpu.VMEM((n_chunks, HEAD_DIM, HEAD_DIM), BF16)],
        compiler_params=pltpu.CompilerParams(
            dimension_semantics=("arbitrary", "arbitrary", "arbitrary")),
        name="hgrn_recurrence",
    )(proj, proj, proj, proj, lb_param, head_gain)


def _kv_prep_kernel(kc_ref, vc_ref, vs_ref, vw_ref, pek_ref, pev_ref, w1k_ref, w1v_ref,
                    b1k_ref, b1v_ref, w2k_ref, w2v_ref, ko_ref, vto_ref, vts_ref, vtw_ref, x_scr,
                    *, nc, ntile):
    half = CMP_BLOCK // 2
    TK = ATT_TK

    def phi(src_ref, pe_ref, w1_ref, b1_ref, w2_ref):
        x_scr[...] = src_ref[...].astype(F32)
        top = jnp.zeros((nc, HEAD_DIM), F32)
        bot = jnp.zeros((nc, HEAD_DIM), F32)
        for p in range(CMP_STRIDE):
            xp = x_scr[pl.ds(p, nc, stride=CMP_STRIDE), :]
            xa = (xp + pe_ref[p:p + 1, :]).astype(BF16)
            xb = (xp + pe_ref[half + p:half + p + 1, :]).astype(BF16)
            top = top + jnp.dot(xa, w1_ref[p * HEAD_DIM:(p + 1) * HEAD_DIM, :],
                                preferred_element_type=F32)
            bot = bot + jnp.dot(xb, w1_ref[(half + p) * HEAD_DIM:(half + p + 1) * HEAD_DIM, :],
                                preferred_element_type=F32)
        hid = top + pltpu.roll(bot, nc - 1, 0) + b1_ref[...]
        act = jax.nn.gelu(hid, approximate=True)
        return jnp.dot(act.astype(BF16), w2_ref[...], preferred_element_type=F32)

    ko_ref[...] = phi(kc_ref, pek_ref, w1k_ref, b1k_ref, w2k_ref).astype(BF16)
    vto_ref[...] = phi(vc_ref, pev_ref, w1v_ref, b1v_ref, w2v_ref).T.astype(BF16)

    def tr(u, carry):
        rows = pl.ds(pl.multiple_of(u * TK, TK), TK)
        vts_ref[u] = vs_ref[rows, :].astype(F32).T.astype(BF16)
        vtw_ref[u] = vw_ref[rows, :].astype(F32).T.astype(BF16)
        return carry
    lax.fori_loop(0, ntile, tr, 0)


def _kv_prep(proj, blk, pe_k, pe_v, w1k, w1v, b1k, b1v, w2k, w2v, batch, seq):
    g = NSA_GROUPS
    nc = seq // CMP_STRIDE
    ntile = seq // ATT_TK

    def full(a):
        return pl.BlockSpec(a.shape, lambda b, gg: (0,) * a.ndim)

    def seq_block(col0):
        return pl.BlockSpec((seq, HEAD_DIM), lambda b, gg: (b, col0 + gg))

    def out(*shape):
        spec = pl.BlockSpec((None, None) + shape, lambda b, gg: (b, gg) + (0,) * len(shape))
        return spec, jax.ShapeDtypeStruct((batch, g) + shape, BF16)

    specs, shapes = zip(out(nc, HEAD_DIM), out(HEAD_DIM, nc),
                        out(ntile, HEAD_DIM, ATT_TK), out(ntile, HEAD_DIM, ATT_TK))
    return pl.pallas_call(
        functools.partial(_kv_prep_kernel, nc=nc, ntile=ntile),
        grid=(batch, g),
        in_specs=[seq_block(blk["k_c"]), seq_block(blk["v_c"]), seq_block(blk["v_s"]),
                  seq_block(blk["v_w"]),
                  full(pe_k), full(pe_v), full(w1k), full(w1v), full(b1k), full(b1v),
                  full(w2k), full(w2v)],
        out_specs=list(specs),
        out_shape=list(shapes),
        scratch_shapes=[pltpu.VMEM((seq, HEAD_DIM), F32)],
        compiler_params=pltpu.CompilerParams(dimension_semantics=("arbitrary", "arbitrary")),
        name="nsa_kv_prep",
    )(proj, proj, proj, proj, pe_k, pe_v, w1k, w1v, b1k, b1v, w2k, w2v)


def _nsa_attn_kernel(q_ref, kc_ref, vtc_ref, ks_ref, vts_ref, kw_ref, vtw_ref, z_ref, gt_ref,
                     win_ref, cmpb_ref, far_ref, ovt_ref, o_ref,
                     q_scr, selm_scr, selb_scr, s_scr, m8_scr, l8_scr, acc_scr, tot_scr, *, seq):
    TQ, TK = ATT_TQ, ATT_TK
    L = NSA_HPG * TQ
    nc = seq // CMP_STRIDE
    nsel = seq // SEL_BLOCK
    blk_per_tile = TK // SEL_BLOCK
    n_win = (WINDOW + TQ) // TK
    g = pl.program_id(1)
    i = pl.program_id(2)
    t0 = i * TQ

    for h in range(NSA_HPG):
        q_scr[h * TQ:(h + 1) * TQ, :] = q_ref[:, h * HEAD_DIM:(h + 1) * HEAD_DIM]

    def gate_row(c):
        rows = [(g * NSA_HPG + h) * N_BRANCH + c for h in range(NSA_HPG)]
        return jnp.concatenate([_sigmoid(gt_ref[pl.ds(r, 1), :]) for r in rows], axis=1)

    cmp_row0 = pl.multiple_of(nc - i * (TQ // CMP_STRIDE), 8)
    s = lax.dot_general(kc_ref[...], q_scr[...], NT_DIMS, preferred_element_type=F32)
    s = s + cmpb_ref[pl.ds(cmp_row0, nc), :]
    mc = jnp.max(s, axis=0, keepdims=True)
    p = jnp.exp2(s - mc)
    inv = jnp.where(mc > 0.1 * NEG_INF, 1.0 / jnp.sum(p, axis=0, keepdims=True), 0.0)
    p = p * inv
    psum = p[:, 0:TQ]
    for h in range(1, NSA_HPG):
        psum = psum + p[:, h * TQ:(h + 1) * TQ]
    oc = jnp.dot(vtc_ref[...], p.astype(BF16), preferred_element_type=F32)
    tot_scr[...] = oc * gate_row(0)

    ovt = ovt_ref[...]
    phi, pmid, plo = _split3(psum)
    imp = (jnp.dot(ovt, phi, preferred_element_type=F32)
           + jnp.dot(ovt, pmid, preferred_element_type=F32)
           + jnp.dot(ovt, plo, preferred_element_type=F32))

    jrow = lax.broadcasted_iota(jnp.int32, (nsel, TQ), 0)
    tcol = t0 + lax.broadcasted_iota(jnp.int32, (nsel, TQ), 1)
    cur = lax.shift_right_arithmetic(tcol, int(math.log2(SEL_BLOCK)))
    forced = (jrow == 0) | (jrow == cur) | (jrow == cur - 1)
    visible = jrow * SEL_BLOCK <= tcol
    score = jnp.where(forced, FORCE_SCORE, jnp.where(visible, imp, NEG_INF))
    row8 = lax.broadcasted_iota(jnp.int32, (8, TQ), 0)
    score8 = [score[r * 8:(r + 1) * 8] for r in range(nsel // 8)]
    rank8 = [jnp.zeros((8, TQ), jnp.int32) for _ in score8]
    for jp in range(nsel):
        row = score[jp:jp + 1, :]
        for r, sc in enumerate(score8):
            if r * 8 > jp:
                ahead = row >= sc
            elif r * 8 + 7 < jp:
                ahead = row > sc
            else:
                ahead = (row > sc) | ((row == sc) & (row8 > jp - r * 8))
            rank8[r] = rank8[r] + ahead.astype(jnp.int32)
    rank = jnp.concatenate(rank8, axis=0)
    selm = jnp.where(rank < min(N_SELECT, nsel), 0.0, NEG_INF)
    selm = jnp.concatenate([selm] * NSA_HPG, axis=1)
    selm_scr[...] = selm
    selb_scr[...] = selm + far_ref[...]

    def step_rows(u0, nt):
        return pl.ds(pl.multiple_of(u0 * TK, TK), nt * TK)

    def logits_step(k_ref, u0, nt, win_row0, sel_scr):
        s = lax.dot_general(k_ref[step_rows(u0, nt), :], q_scr[...], NT_DIMS,
                            preferred_element_type=F32)
        if win_row0 is not None:
            s = s + win_ref[win_row0:win_row0 + nt * TK, :]
        if sel_scr is not None:
            parts = [s[r * SEL_BLOCK:(r + 1) * SEL_BLOCK]
                     + sel_scr[pl.ds(u0 * blk_per_tile + r, 1), :] for r in range(nt * blk_per_tile)]
            s = jnp.concatenate(parts, axis=0)
        s_scr[step_rows(u0, nt), :] = s
        m8_scr[...] = jnp.maximum(m8_scr[...], jnp.max(s.reshape(nt * TK // 8, 8, L), axis=0))

    def pv_step(vt_ref, m, u0, nt):
        p = jnp.exp2(s_scr[step_rows(u0, nt), :] - m)
        vt = jnp.concatenate([vt_ref[u0 + j] for j in range(nt)], axis=1)
        acc_scr[...] += jnp.dot(vt, p.astype(BF16), preferred_element_type=F32)
        l8_scr[...] += jnp.sum(p.reshape(nt * TK // 8, 8, L), axis=0)

    def over_tiles(u_lo, n, step):
        n4 = lax.shift_right_logical(n, 2)

        def body(j, carry):
            step(u_lo + 4 * j, 4)
            return carry
        lax.fori_loop(0, n4, body, 0)
        u2 = u_lo + 4 * n4
        pl.when((n & 2) != 0)(lambda: step(u2, 2))
        pl.when((n & 1) != 0)(lambda: step(u2 + (n & 2), 1))

    def near_diagonal(step):
        for nt in range(1, n_win + 1):
            cond = (i == nt - 1) if nt < n_win else (i >= nt - 1)
            pl.when(cond)(functools.partial(step, i - (nt - 1), nt, (n_win - nt) * TK))

    def finish(c):
        w = gate_row(c) * (1.0 / jnp.sum(l8_scr[...], axis=0, keepdims=True))
        tot_scr[...] = tot_scr[...] + acc_scr[...] * w

    def reset():
        m8_scr[...] = jnp.full_like(m8_scr, NEG_INF)
        l8_scr[...] = jnp.zeros_like(l8_scr)
        acc_scr[...] = jnp.zeros_like(acc_scr)

    reset()
    n_far = jnp.maximum(i - 1, 0)
    over_tiles(0, n_far, lambda u0, nt: logits_step(ks_ref, u0, nt, None, selb_scr))
    pl.when(i >= 1)(lambda: logits_step(ks_ref, i - 1, 2, (n_win - 2) * TK, selm_scr))
    pl.when(i == 0)(lambda: logits_step(ks_ref, 0, 1, (n_win - 1) * TK, selm_scr))
    m_sel = jnp.max(m8_scr[...], axis=0, keepdims=True)
    over_tiles(0, i + 1, functools.partial(pv_step, vts_ref, m_sel))
    finish(1)

    reset()
    near_diagonal(lambda u0, nt, row0: logits_step(kw_ref, u0, nt, row0, None))
    m_win = jnp.max(m8_scr[...], axis=0, keepdims=True)
    near_diagonal(lambda u0, nt, row0: pv_step(vtw_ref, m_win, u0, nt))
    finish(2)

    for h in range(NSA_HPG):
        cols = slice(h * HEAD_DIM, (h + 1) * HEAD_DIM)
        z = z_ref[:, cols].astype(F32)
        o_ref[:, cols] = (tot_scr[:, h * TQ:(h + 1) * TQ].T * (z * _sigmoid(z))).astype(o_ref.dtype)


def _nsa_attention(proj, kcmp, vtc, vts, vtw, gt, win, cmpb, far, ovt, cols, batch, seq):
    TQ, TK = ATT_TQ, ATT_TK
    n = proj.shape[0]
    ng = NSA_GROUPS
    nq = seq // TQ
    L = NSA_HPG * TQ
    gw = NSA_HPG * HEAD_DIM

    def seq_block(col0):
        return pl.BlockSpec((seq, HEAD_DIM), lambda b, g, i: (b, col0 + g))

    def per_bg(a):
        return pl.BlockSpec((None, None) + a.shape[2:], lambda b, g, i: (b, g) + (0,) * (a.ndim - 2))

    def per_g(a):
        return pl.BlockSpec((None,) + a.shape[1:], lambda b, g, i: (g,) + (0,) * (a.ndim - 1))

    return pl.pallas_call(
        functools.partial(_nsa_attn_kernel, seq=seq),
        grid=(batch, ng, nq),
        in_specs=[pl.BlockSpec((TQ, gw), lambda b, g, i: (b * nq + i, cols["q"] + g)),
                  per_bg(kcmp), per_bg(vtc),
                  seq_block(cols["k_s"]), per_bg(vts),
                  seq_block(cols["k_w"]), per_bg(vtw),
                  pl.BlockSpec((TQ, gw), lambda b, g, i: (b * nq + i, cols["z"] + g)),
                  pl.BlockSpec((gt.shape[0], TQ), lambda b, g, i: (0, b * nq + i)),
                  per_g(win), per_g(cmpb), per_g(far),
                  pl.BlockSpec(ovt.shape, lambda b, g, i: (0, 0))],
        out_specs=pl.BlockSpec((TQ, gw), lambda b, g, i: (b * nq + i, g)),
        out_shape=jax.ShapeDtypeStruct((n, ng * gw), BF16),
        scratch_shapes=[pltpu.VMEM((L, HEAD_DIM), BF16),
                        pltpu.VMEM((seq // SEL_BLOCK, L), F32),
                        pltpu.VMEM((seq // SEL_BLOCK, L), F32),
                        pltpu.VMEM((seq, L), F32),
                        pltpu.VMEM((8, L), F32),
                        pltpu.VMEM((8, L), F32),
                        pltpu.VMEM((HEAD_DIM, L), F32),
                        pltpu.VMEM((HEAD_DIM, L), F32)],
        compiler_params=pltpu.CompilerParams(
            dimension_semantics=("arbitrary", "arbitrary", "arbitrary"),
            vmem_limit_bytes=ATT_VMEM_LIMIT),
        name="nsa_attention",
    )(proj, kcmp, vtc, proj, vts, proj, vtw, proj, gt, win, cmpb, far, ovt)


def _bucket_of_distance():
    d = np.arange(MAX_DISTANCE)
    max_exact = NUM_BUCKETS // 2
    large = max_exact + (np.log(np.maximum(d, 1).astype(np.float32) / max_exact)
                         / math.log(MAX_DISTANCE / max_exact) * (NUM_BUCKETS - max_exact)).astype(np.int32)
    large = np.minimum(large, NUM_BUCKETS - 1)
    return np.where(d < max_exact, d, large)


CMP_NEAR = 16


def _bias_builder_kernel(vw_ref, vc_ref, far_ref, win_ref, cmpb_ref, *, nc):
    TQ, TK = ATT_TQ, ATT_TK
    n_win = (WINDOW + TQ) // TK
    for h in range(NSA_HPG):
        lanes = slice(h * TQ, (h + 1) * TQ)
        for w in range(n_win):
            v = vw_ref[h * n_win + w:h * n_win + w + 1, :]
            r = pltpu.roll(jnp.broadcast_to(v, (TK, 2 * TQ)), 0, 1, stride=1, stride_axis=0)
            win_ref[w * TK:(w + 1) * TK, lanes] = r[:, TQ:2 * TQ]
        cmpb_ref[0:nc - CMP_NEAR, lanes] = jnp.broadcast_to(far_ref[:, lanes], (nc - CMP_NEAR, TQ))
        vc = vc_ref[h:h + 1, :]
        rc = pltpu.roll(jnp.broadcast_to(vc, (2 * CMP_NEAR, 4 * TQ)), 0, 1,
                        stride=CMP_STRIDE, stride_axis=0)
        cmpb_ref[nc - CMP_NEAR:nc + CMP_NEAR, lanes] = rc[:, 2 * TQ:3 * TQ]
        cmpb_ref[nc + CMP_NEAR:2 * nc, lanes] = jnp.full((nc - CMP_NEAR, TQ), NEG_INF, F32)


def _bias_tiles(rel_bias, seq):
    TQ, TK = ATT_TQ, ATT_TK
    assert TQ == TK and TQ >= MAX_DISTANCE
    assert CMP_STRIDE * CMP_NEAR >= TQ and 2 * TQ >= CMP_STRIDE * 2 * CMP_NEAR
    assert CMP_STRIDE * (CMP_NEAR + 1) - (CMP_BLOCK - 1) >= MAX_DISTANCE - 1
    nc = seq // CMP_STRIDE
    ng, hpg = NSA_GROUPS, NSA_HPG
    n_win = (WINDOW + TQ) // TK
    tb = (rel_bias.astype(F32) * LOG2E)[_bucket_of_distance()].T.reshape(ng, hpg, MAX_DISTANCE)

    def by_distance(dist, valid):
        vals = tb[:, :, np.clip(dist, 0, MAX_DISTANCE - 1)]
        return jnp.where(valid, vals, NEG_INF)

    x = np.arange(2 * TQ)[None, :]
    dw = WINDOW - TK * np.arange(n_win)[:, None] - TQ + x
    vw = by_distance(dw, (dw >= 0) & (dw < WINDOW)).reshape(ng, hpg * n_win, 2 * TQ)
    xc = np.arange(4 * TQ)
    dc = xc - 2 * TQ + CMP_STRIDE * CMP_NEAR - (CMP_BLOCK - 1)
    vc = by_distance(dc, dc >= 0)
    L = hpg * TQ
    far = jnp.broadcast_to(tb[:, :, MAX_DISTANCE - 1:], (ng, hpg, TQ)).reshape(ng, 1, L)

    win, cmpb = pl.pallas_call(
        functools.partial(_bias_builder_kernel, nc=nc),
        grid=(ng,),
        in_specs=[pl.BlockSpec((None,) + vw.shape[1:], lambda g: (g, 0, 0)),
                  pl.BlockSpec((None,) + vc.shape[1:], lambda g: (g, 0, 0)),
                  pl.BlockSpec((None,) + far.shape[1:], lambda g: (g, 0, 0))],
        out_specs=[pl.BlockSpec((None, WINDOW + TQ, L), lambda g: (g, 0, 0)),
                   pl.BlockSpec((None, 2 * nc, L), lambda g: (g, 0, 0))],
        out_shape=[jax.ShapeDtypeStruct((ng, WINDOW + TQ, L), F32),
                   jax.ShapeDtypeStruct((ng, 2 * nc, L), F32)],
        compiler_params=pltpu.CompilerParams(dimension_semantics=("arbitrary",)),
        name="nsa_bias_tiles",
    )(vw, vc, far)
    return win, cmpb, far


def _overlap_t(seq):
    nc = seq // CMP_STRIDE
    nsel = seq // SEL_BLOCK
    ci = np.arange(nc)[None, :]
    sj = np.arange(nsel)[:, None]
    ov = (CMP_STRIDE * ci < SEL_BLOCK * (sj + 1)) & (CMP_STRIDE * ci + CMP_BLOCK > SEL_BLOCK * sj)
    ov = ov & (ci < nc - 1)
    return jnp.asarray(ov, BF16)


def kernel(x, norm_gains, final_gain, rel_bias, hgrn_lb, hgrn_w_in, hgrn_head_gain, hgrn_w_out,
           nsa_w_in, nsa_pe_k, nsa_pe_v, nsa_phi_k_w1, nsa_phi_k_b1, nsa_phi_k_w2,
           nsa_phi_v_w1, nsa_phi_v_b1, nsa_phi_v_w2, nsa_w_out):
    batch, seq, d = x.shape
    n = batch * seq
    x2d = x.reshape(n, d)
    qscale = HEAD_DIM ** -0.5

    def scale_q_columns(w, scale):
        col = np.ones((1, w.shape[1]), np.float32)
        col[:, :d] = scale
        return (w * col).astype(BF16)

    proj = _norm_proj(x2d, norm_gains[0:1], scale_q_columns(hgrn_w_in[0], qscale))
    o = _hgrn_recurrence(proj, hgrn_lb, hgrn_head_gain[0:1], batch, seq)
    x1, h1 = _out_proj(o, hgrn_w_out[0].astype(BF16), x2d, norm_gains[1:2], final=False)

    kvw = NSA_GROUPS * HEAD_DIM
    w = nsa_w_in[0]
    gate0 = d + 6 * kvw
    gate1 = gate0 + NSA_HEADS * N_BRANCH
    ws = scale_q_columns(w, qscale * LOG2E)
    w_main = jnp.concatenate([ws[:, :gate0], ws[:, gate1:]], axis=1)
    wgt = jnp.zeros((HEAD_DIM, d), F32).at[:gate1 - gate0].set(w[:, gate0:gate1].T).astype(BF16)
    proj, gt = _proj_gate(h1, w_main, wgt)

    blk = {}
    for idx, name in enumerate(("k_c", "v_c", "k_s", "v_s", "k_w", "v_w")):
        blk[name] = (d + idx * kvw) // HEAD_DIM
    gw = NSA_HPG * HEAD_DIM
    cols = dict(blk, q=0, z=gate0 // gw)

    kcmp, vtc, vts, vtw = _kv_prep(
        proj, blk, nsa_pe_k[0], nsa_pe_v[0],
        nsa_phi_k_w1[0].astype(BF16), nsa_phi_v_w1[0].astype(BF16),
        nsa_phi_k_b1[0:1], nsa_phi_v_b1[0:1],
        nsa_phi_k_w2[0].astype(BF16), nsa_phi_v_w2[0].astype(BF16), batch, seq)

    win, cmpb, far = _bias_tiles(rel_bias, seq)
    o = _nsa_attention(proj, kcmp, vtc, vts, vtw, gt, win, cmpb, far, _overlap_t(seq), cols,
                       batch, seq)
    out, = _out_proj(o, nsa_w_out[0].astype(BF16), x1, final_gain[None, :], final=True)
    return out.reshape(batch, seq, d)
```

```python
import functools
import math

import numpy as np
import jax
import jax.numpy as jnp
from jax import lax
from jax.experimental import pallas as pl
from jax.experimental.pallas import tpu as pltpu

F32 = jnp.float32
BF16 = jnp.bfloat16

D_MODEL = 2048
RMS_EPS = 1e-6
HEAD_DIM = 128

HGRN_HEADS = D_MODEL // HEAD_DIM
HGRN_CHUNK = 64
HGRN_SUB = 16
HGRN_SAFE_LOG2 = 96.0

NSA_HEADS = D_MODEL // HEAD_DIM
NSA_GROUPS = 4
NSA_HPG = NSA_HEADS // NSA_GROUPS
N_BRANCH = 3
CMP_BLOCK = 32
CMP_STRIDE = 16
SEL_BLOCK = 64
N_SELECT = 16
WINDOW = 512
NUM_BUCKETS = 32
MAX_DISTANCE = 128
NEG_INF = -1e30
FORCE_SCORE = 1e9

ATT_TQ = 256
ATT_TK = 256
V_PAD = 16
ATT_STEP_TILES = 8
CMP_ROWS = 64
LOG2E = math.log2(math.e)
V7X_VMEM_BYTES = 64 * 1024 * 1024
VMEM_LIMIT = V7X_VMEM_BYTES * 3 // 4
ATT_VMEM_LIMIT = V7X_VMEM_BYTES * 7 // 8

NT_DIMS = (((1,), (1,)), ((), ()))
TN_DIMS = (((0,), (0,)), ((), ()))


def _sigmoid(x):
    return 1.0 / (1.0 + jnp.exp(-x))


def _split3(a):
    hi = a.astype(BF16)
    r1 = a - hi.astype(F32)
    mid = r1.astype(BF16)
    lo = (r1 - mid.astype(F32)).astype(BF16)
    return hi, mid, lo


def _norm_proj_kernel(x_ref, g_ref, w_ref, o_ref, h_scr):
    @pl.when(pl.program_id(1) == 0)
    def _():
        x = x_ref[...]
        ms = jnp.mean(x * x, axis=-1, keepdims=True)
        h_scr[...] = (x * lax.rsqrt(ms + RMS_EPS) * g_ref[...]).astype(BF16)

    o_ref[...] = jnp.dot(h_scr[...], w_ref[...], preferred_element_type=F32).astype(o_ref.dtype)


def _norm_proj(x, gain, w, *, tm=1024, tn=1024):
    n, d = x.shape
    nout = w.shape[1]
    return pl.pallas_call(
        _norm_proj_kernel,
        grid=(n // tm, nout // tn),
        in_specs=[pl.BlockSpec((tm, d), lambda i, j: (i, 0)),
                  pl.BlockSpec((1, d), lambda i, j: (0, 0)),
                  pl.BlockSpec((d, tn), lambda i, j: (0, j))],
        out_specs=pl.BlockSpec((tm, tn), lambda i, j: (i, j)),
        out_shape=jax.ShapeDtypeStruct((n, nout), BF16),
        scratch_shapes=[pltpu.VMEM((tm, d), BF16)],
        compiler_params=pltpu.CompilerParams(
            dimension_semantics=("arbitrary", "arbitrary"), vmem_limit_bytes=VMEM_LIMIT),
        name="norm_proj",
    )(x, gain, w)


def _proj_gate_kernel(h_ref, w_ref, wgt_ref, o_ref, gt_ref):
    @pl.when(pl.program_id(1) == 0)
    def _():
        gt_ref[...] = lax.dot_general(wgt_ref[...], h_ref[...], NT_DIMS, preferred_element_type=F32)

    o_ref[...] = jnp.dot(h_ref[...], w_ref[...], preferred_element_type=F32).astype(o_ref.dtype)


def _proj_gate(h, w, wgt, *, tm=1024, tn=1024):
    n, d = h.shape
    nout = w.shape[1]
    ng = wgt.shape[0]
    return pl.pallas_call(
        _proj_gate_kernel,
        grid=(n // tm, nout // tn),
        in_specs=[pl.BlockSpec((tm, d), lambda i, j: (i, 0)),
                  pl.BlockSpec((d, tn), lambda i, j: (0, j)),
                  pl.BlockSpec((ng, d), lambda i, j: (0, 0))],
        out_specs=[pl.BlockSpec((tm, tn), lambda i, j: (i, j)),
                   pl.BlockSpec((ng, tm), lambda i, j: (0, i))],
        out_shape=[jax.ShapeDtypeStruct((n, nout), BF16),
                   jax.ShapeDtypeStruct((ng, n), F32)],
        compiler_params=pltpu.CompilerParams(
            dimension_semantics=("arbitrary", "arbitrary"), vmem_limit_bytes=VMEM_LIMIT),
        name="proj_gate",
    )(h, w, wgt)


def _out_proj_kernel(o_ref, w_ref, res_ref, g_ref, *out_refs, final):
    xn = res_ref[...] + jnp.dot(o_ref[...], w_ref[...], preferred_element_type=F32)
    ms = jnp.mean(xn * xn, axis=-1, keepdims=True)
    y = xn * lax.rsqrt(ms + RMS_EPS) * g_ref[...]
    if final:
        out_refs[0][...] = y
    else:
        out_refs[0][...] = xn
        out_refs[1][...] = y.astype(BF16)


def _out_proj(o, w, res, gain, *, final, tm=512):
    n, d = res.shape
    row = pl.BlockSpec((tm, d), lambda i: (i, 0))
    if final:
        out_specs = [row]
        out_shape = [jax.ShapeDtypeStruct((n, d), F32)]
    else:
        out_specs = [row, row]
        out_shape = [jax.ShapeDtypeStruct((n, d), F32), jax.ShapeDtypeStruct((n, d), BF16)]
    return pl.pallas_call(
        functools.partial(_out_proj_kernel, final=final),
        grid=(n // tm,),
        in_specs=[pl.BlockSpec((tm, o.shape[1]), lambda i: (i, 0)),
                  pl.BlockSpec(w.shape, lambda i: (0, 0)),
                  row,
                  pl.BlockSpec((1, d), lambda i: (0, 0))],
        out_specs=out_specs,
        out_shape=out_shape,
        compiler_params=pltpu.CompilerParams(
            dimension_semantics=("arbitrary",), vmem_limit_bytes=VMEM_LIMIT),
        name="out_proj_final" if final else "out_proj",
    )(o, w, res, gain)


def _hgrn_kernel(q_ref, f_ref, v_ref, z_ref, lbp_ref, hg_ref, o_ref,
                 st_scr, k_scr, b_scr, qe_scr, oi_scr, u_scr, d_scr, stb_scr, *, n_chunks):
    C, SUB = HGRN_CHUNK, HGRN_SUB

    @pl.when(pl.program_id(2) == 0)
    def _():
        st_scr[...] = jnp.zeros_like(st_scr)

    lbp = lbp_ref[...]
    e = jnp.exp(lbp - jnp.max(lbp, axis=0, keepdims=True))
    lb = e[0:1, :] / jnp.sum(e, axis=0, keepdims=True)
    hg = hg_ref[...]

    rr = lax.broadcasted_iota(jnp.int32, (C, C), 0)
    cc = lax.broadcasted_iota(jnp.int32, (C, C), 1)
    causal = rr >= cc
    tril = causal.astype(BF16)
    sub_c = lax.broadcasted_iota(jnp.int32, (SUB, C), 1)
    row8 = lax.broadcasted_iota(jnp.int32, (8, C), 0)
    col8 = lax.broadcasted_iota(jnp.int32, (8, C), 1)

    def emit(rows, o):
        z = z_ref[rows, :].astype(F32)
        ms = jnp.mean(o * o, axis=-1, keepdims=True)
        o_ref[rows, :] = (o * lax.rsqrt(ms + RMS_EPS) * hg * (z * _sigmoid(z))).astype(o_ref.dtype)

    chunks = [slice(c * C, (c + 1) * C) for c in range(n_chunks)]
    ks, gsplit = [], []
    for rows in chunks:
        sig = _sigmoid(f_ref[rows, :].astype(F32))
        gsplit.append(_split3(jnp.log(lb + (1.0 - lb) * sig)))
        ks.append((1.0 - lb) * (1.0 - sig))
        k_scr[rows, :] = ks[-1]
    b2s = []
    for rows, (ghi, gmid, glo) in zip(chunks, gsplit):
        b = (jnp.dot(tril, ghi, preferred_element_type=F32)
             + jnp.dot(tril, gmid, preferred_element_type=F32)
             + jnp.dot(tril, glo, preferred_element_type=F32))
        b2s.append(b * LOG2E)
        b_scr[rows, :] = b2s[-1]
    span = jnp.zeros((1, HEAD_DIM), F32)
    for b2 in b2s:
        span = jnp.maximum(span, -b2[C - 1:C])
    safe = jnp.max(span) < HGRN_SAFE_LOG2

    qes, kes = [], []
    for rows, k, b2 in zip(chunks, ks, b2s):
        qes.append((q_ref[rows, :].astype(F32) * jnp.exp2(b2)).astype(BF16))
        kes.append(k * jnp.exp2(-b2))
        qe_scr[rows, :] = qes[-1]
    scores = [lax.dot_general(qe, ke.astype(BF16), NT_DIMS, preferred_element_type=F32)
              for qe, ke in zip(qes, kes)]
    vt = v_ref[...].astype(F32).T.astype(BF16)
    zero = jnp.zeros((C, HEAD_DIM), BF16)
    for c, (rows, sc, ke, b2) in enumerate(zip(chunks, scores, kes, b2s)):
        bl = b2[C - 1:C]
        oi_scr[rows, :] = jnp.dot(jnp.where(causal, sc, 0.0).astype(BF16), v_ref[rows, :],
                                  preferred_element_type=F32)
        kd = (ke * jnp.exp2(bl)).astype(BF16)
        pair = slice(c // 2 * 2 * C, (c // 2 + 1) * 2 * C)
        u_scr[c] = jnp.dot(vt[:, pair], jnp.concatenate([kd, zero] if c % 2 == 0 else [zero, kd]),
                           preferred_element_type=F32)
        d_scr[c:c + 1, :] = jnp.exp2(bl)

    @pl.when(safe)
    def _():
        st = st_scr[...]
        for c in range(n_chunks):
            stb_scr[c] = st.astype(BF16)
            st = st * d_scr[c:c + 1, :] + u_scr[c]
        st_scr[...] = st
        carried = [lax.dot_general(qe_scr[rows, :], stb_scr[c], NT_DIMS, preferred_element_type=F32)
                   for c, rows in enumerate(chunks)]
        for rows, oc in zip(chunks, carried):
            emit(rows, oi_scr[rows, :] + oc)

    def robust_chunk(c):
        rows = slice(c * C, (c + 1) * C)
        q = q_ref[rows, :].astype(F32)
        v = v_ref[rows, :]
        k, b2 = k_scr[rows, :], b_scr[rows, :]
        st = st_scr[...]
        o = lax.dot_general((q * jnp.exp2(b2)).astype(BF16), st.astype(BF16), NT_DIMS,
                            preferred_element_type=F32)
        a_rows = []
        for i in range(C // SUB):
            lo, hi = i * SUB, (i + 1) * SUB
            bq, qq, kk = b2[lo:hi], q[lo:hi], k[lo:hi]
            if i > 0:
                r = b2[lo - 1:lo]
                qi = (qq * jnp.exp2(bq - r)).astype(BF16)
                ki = (k * jnp.exp2(jnp.minimum(r - b2, 0.0))).astype(BF16)
                a = lax.dot_general(qi, ki, NT_DIMS, preferred_element_type=F32)
                a = jnp.where(sub_c < lo, a, 0.0)
            else:
                a = jnp.zeros((SUB, C), F32)
            a8 = [a[j * 8:(j + 1) * 8] for j in range(SUB // 8)]
            for s in range(SUB):
                for j in range(s // 8, SUB // 8):
                    grp = slice(j * 8, (j + 1) * 8)
                    ed = qq[grp] * jnp.exp2(bq[grp] - bq[s:s + 1]) * kk[s:s + 1]
                    col = jnp.sum(ed, axis=-1, keepdims=True)
                    hit = col8 == lo + s
                    if j == s // 8:
                        hit = hit & (row8 >= s - j * 8)
                    a8[j] = jnp.where(hit, col, a8[j])
            a_rows.extend(a8)
        scores = jnp.concatenate(a_rows, axis=0)
        o = o + jnp.dot(scores.astype(BF16), v, preferred_element_type=F32)
        bl = b2[C - 1:C]
        st_scr[...] = st * jnp.exp2(bl) + lax.dot_general(
            v, (k * jnp.exp2(bl - b2)).astype(BF16), TN_DIMS, preferred_element_type=F32)
        emit(rows, o)

    @pl.when(jnp.logical_not(safe))
    def _():
        for c in range(n_chunks):
            robust_chunk(c)


def _hgrn_recurrence(proj, lb_param, head_gain, batch, seq, *, t_blk=1024):
    n = proj.shape[0]
    nh = HGRN_HEADS
    ns = seq // t_blk

    def sec(k):
        return pl.BlockSpec((t_blk, HEAD_DIM), lambda b, h, s, k=k: (b * ns + s, k * nh + h))

    n_chunks = t_blk // HGRN_CHUNK
    return pl.pallas_call(
        functools.partial(_hgrn_kernel, n_chunks=n_chunks),
        grid=(batch, nh, ns),
        in_specs=[sec(0), sec(1), sec(2), sec(3),
                  pl.BlockSpec((lb_param.shape[0], HEAD_DIM), lambda b, h, s: (0, h)),
                  pl.BlockSpec((1, HEAD_DIM), lambda b, h, s: (0, 0))],
        out_specs=pl.BlockSpec((t_blk, HEAD_DIM), lambda b, h, s: (b * ns + s, h)),
        out_shape=jax.ShapeDtypeStruct((n, nh * HEAD_DIM), BF16),
        scratch_shapes=[pltpu.VMEM((HEAD_DIM, HEAD_DIM), F32),
                        pltpu.VMEM((t_blk, HEAD_DIM), F32),
                        pltpu.VMEM((t_blk, HEAD_DIM), F32),
                        pltpu.VMEM((t_blk, HEAD_DIM), BF16),
                        pltpu.VMEM((t_blk, HEAD_DIM), F32),
                        pltpu.VMEM((n_chunks, HEAD_DIM, HEAD_DIM), F32),
                        pltpu.VMEM((n_chunks, HEAD_DIM), F32),
                        pltpu.VMEM((n_chunks, HEAD_DIM, HEAD_DIM), BF16)],
        compiler_params=pltpu.CompilerParams(
            dimension_semantics=("arbitrary", "arbitrary", "arbitrary")),
        name="hgrn_recurrence",
    )(proj, proj, proj, proj, lb_param, head_gain)


def _kv_prep_kernel(kc_ref, vc_ref, vs_ref, vw_ref, pek_ref, pev_ref, w1k_ref, w1v_ref,
                    b1k_ref, b1v_ref, w2k_ref, w2v_ref, ko_ref, vto_ref, vts_ref, vtw_ref, x_scr,
                    *, nc, ntile):
    half = CMP_BLOCK // 2
    TK = ATT_TK

    def phi(src_ref, pe_ref, w1_ref, b1_ref, w2_ref):
        x_scr[...] = src_ref[...].astype(F32)
        top = jnp.zeros((nc, HEAD_DIM), F32)
        bot = jnp.zeros((nc, HEAD_DIM), F32)
        for p in range(CMP_STRIDE):
            xp = x_scr[pl.ds(p, nc, stride=CMP_STRIDE), :]
            xa = (xp + pe_ref[p:p + 1, :]).astype(BF16)
            xb = (xp + pe_ref[half + p:half + p + 1, :]).astype(BF16)
            top = top + jnp.dot(xa, w1_ref[p * HEAD_DIM:(p + 1) * HEAD_DIM, :],
                                preferred_element_type=F32)
            bot = bot + jnp.dot(xb, w1_ref[(half + p) * HEAD_DIM:(half + p + 1) * HEAD_DIM, :],
                                preferred_element_type=F32)
        hid = top + pltpu.roll(bot, nc - 1, 0) + b1_ref[...]
        act = jax.nn.gelu(hid, approximate=True)
        return jnp.dot(act.astype(BF16), w2_ref[...], preferred_element_type=F32)

    ko_ref[...] = phi(kc_ref, pek_ref, w1k_ref, b1k_ref, w2k_ref).astype(BF16)
    vto_ref[...] = phi(vc_ref, pev_ref, w1v_ref, b1v_ref, w2v_ref).T.astype(BF16)

    ones_row = (lax.broadcasted_iota(jnp.int32, (V_PAD, TK), 0) == 0).astype(BF16)

    def tr(u, carry):
        rows = pl.ds(pl.multiple_of(u * TK, TK), TK)
        vts_ref[u] = jnp.concatenate([vs_ref[rows, :].astype(F32).T.astype(BF16), ones_row], axis=0)
        vtw_ref[u] = jnp.concatenate([vw_ref[rows, :].astype(F32).T.astype(BF16), ones_row], axis=0)
        return carry
    lax.fori_loop(0, ntile, tr, 0)


def _kv_prep(proj, blk, pe_k, pe_v, w1k, w1v, b1k, b1v, w2k, w2v, batch, seq):
    g = NSA_GROUPS
    nc = seq // CMP_STRIDE
    ntile = seq // ATT_TK

    def full(a):
        return pl.BlockSpec(a.shape, lambda b, gg: (0,) * a.ndim)

    def seq_block(col0):
        return pl.BlockSpec((seq, HEAD_DIM), lambda b, gg: (b, col0 + gg))

    def out(*shape):
        spec = pl.BlockSpec((None, None) + shape, lambda b, gg: (b, gg) + (0,) * len(shape))
        return spec, jax.ShapeDtypeStruct((batch, g) + shape, BF16)

    specs, shapes = zip(out(nc, HEAD_DIM), out(HEAD_DIM, nc),
                        out(ntile, HEAD_DIM + V_PAD, ATT_TK), out(ntile, HEAD_DIM + V_PAD, ATT_TK))
    return pl.pallas_call(
        functools.partial(_kv_prep_kernel, nc=nc, ntile=ntile),
        grid=(batch, g),
        in_specs=[seq_block(blk["k_c"]), seq_block(blk["v_c"]), seq_block(blk["v_s"]),
                  seq_block(blk["v_w"]),
                  full(pe_k), full(pe_v), full(w1k), full(w1v), full(b1k), full(b1v),
                  full(w2k), full(w2v)],
        out_specs=list(specs),
        out_shape=list(shapes),
        scratch_shapes=[pltpu.VMEM((seq, HEAD_DIM), F32)],
        compiler_params=pltpu.CompilerParams(dimension_semantics=("arbitrary", "arbitrary")),
        name="nsa_kv_prep",
    )(proj, proj, proj, proj, pe_k, pe_v, w1k, w1v, b1k, b1v, w2k, w2v)


def _nsa_attn_kernel(q_ref, kc_ref, vtc_ref, ks_ref, vts_ref, kw_ref, vtw_ref, z_ref, gt_ref,
                     win_ref, cmpb_ref, far_ref, ovt_ref, o_ref,
                     q_scr, psum_scr, rank_scr, selm_scr, selb_scr, s_scr, m8_scr, acc_scr, tot_scr,
                     *, seq):
    TQ, TK = ATT_TQ, ATT_TK
    L = NSA_HPG * TQ
    nc = seq // CMP_STRIDE
    nsel = seq // SEL_BLOCK
    blk_per_tile = TK // SEL_BLOCK
    n_win = (WINDOW + TQ) // TK
    g = pl.program_id(1)
    i = pl.program_id(2)
    t0 = i * TQ

    for h in range(NSA_HPG):
        q_scr[h * TQ:(h + 1) * TQ, :] = q_ref[:, h * HEAD_DIM:(h + 1) * HEAD_DIM]

    def gate_row(c):
        rows = [(g * NSA_HPG + h) * N_BRANCH + c for h in range(NSA_HPG)]
        return jnp.concatenate([_sigmoid(gt_ref[pl.ds(r, 1), :]) for r in rows], axis=1)

    cmp_row0 = pl.multiple_of(nc - i * (TQ // CMP_STRIDE), 8)

    def cmp_branch(rows):
        s = lax.dot_general(kc_ref[0:rows, :], q_scr[...], NT_DIMS, preferred_element_type=F32)
        s = s + cmpb_ref[pl.ds(cmp_row0, rows), :]
        mc = jnp.max(s, axis=0, keepdims=True)
        p = jnp.exp2(s - mc)
        inv = jnp.where(mc > 0.1 * NEG_INF, 1.0 / jnp.sum(p, axis=0, keepdims=True), 0.0)
        p = p * inv
        psum = p[:, 0:TQ]
        for h in range(1, NSA_HPG):
            psum = psum + p[:, h * TQ:(h + 1) * TQ]
        pb = p.astype(BF16)
        if rows < nc:
            psum = jnp.concatenate([psum, jnp.zeros((nc - rows, TQ), F32)], axis=0)
            pb = jnp.concatenate([pb, jnp.zeros((nc - rows, L), BF16)], axis=0)
        psum_scr[...] = psum
        oc = jnp.dot(vtc_ref[...], pb, preferred_element_type=F32)
        tot_scr[...] = oc * gate_row(0)

    qblk_per_rows = CMP_ROWS // (TQ // CMP_STRIDE)
    for k in range(1, nc // CMP_ROWS + 1):
        pl.when(i // qblk_per_rows + 1 == k)(functools.partial(cmp_branch, k * CMP_ROWS))

    ovt = ovt_ref[...]
    phi, pmid, plo = _split3(psum_scr[...])
    imp = (jnp.dot(ovt, phi, preferred_element_type=F32)
           + jnp.dot(ovt, pmid, preferred_element_type=F32)
           + jnp.dot(ovt, plo, preferred_element_type=F32))

    jrow = lax.broadcasted_iota(jnp.int32, (nsel, TQ), 0)
    tcol = t0 + lax.broadcasted_iota(jnp.int32, (nsel, TQ), 1)
    cur = lax.shift_right_arithmetic(tcol, int(math.log2(SEL_BLOCK)))
    forced = (jrow == 0) | (jrow == cur) | (jrow == cur - 1)
    visible = jrow * SEL_BLOCK <= tcol
    score = jnp.where(forced, FORCE_SCORE, jnp.where(visible, imp, NEG_INF))
    row8 = lax.broadcasted_iota(jnp.int32, (8, TQ), 0)
    score8 = [score[r * 8:(r + 1) * 8] for r in range(nsel // 8)]
    rank_scr[...] = jnp.zeros_like(rank_scr)
    n_visible = (i + 1) * (TQ // SEL_BLOCK)

    def count_ahead(grp):
        cnt = [jnp.zeros((8, TQ), jnp.int32) for _ in score8]
        for jp in range(grp * 8, grp * 8 + 8):
            row = score[jp:jp + 1, :]
            for r, sc in enumerate(score8):
                if r * 8 > jp:
                    ahead = row >= sc
                elif r * 8 + 7 < jp:
                    ahead = row > sc
                else:
                    ahead = (row > sc) | ((row == sc) & (row8 > jp - r * 8))
                cnt[r] = cnt[r] + ahead.astype(jnp.int32)
        for r, c in enumerate(cnt):
            rank_scr[r * 8:(r + 1) * 8, :] += c

    for grp in range(nsel // 8):
        pl.when(grp * 8 < n_visible)(functools.partial(count_ahead, grp))
    rank = rank_scr[...]
    selm = jnp.where(rank < min(N_SELECT, nsel), 0.0, NEG_INF)
    selm = jnp.concatenate([selm] * NSA_HPG, axis=1)
    selm_scr[...] = selm
    selb_scr[...] = selm + far_ref[...]

    def step_rows(u0, nt):
        return pl.ds(pl.multiple_of(u0 * TK, TK), nt * TK)

    def logits_step(k_ref, u0, nt, win_row0, sel_scr):
        s = lax.dot_general(k_ref[step_rows(u0, nt), :], q_scr[...], NT_DIMS,
                            preferred_element_type=F32)
        if win_row0 is not None:
            s = s + win_ref[win_row0:win_row0 + nt * TK, :]
        if sel_scr is not None:
            parts = [s[r * SEL_BLOCK:(r + 1) * SEL_BLOCK]
                     + sel_scr[pl.ds(u0 * blk_per_tile + r, 1), :] for r in range(nt * blk_per_tile)]
            s = jnp.concatenate(parts, axis=0)
        s_scr[step_rows(u0, nt), :] = s
        m8_scr[...] = jnp.maximum(m8_scr[...], jnp.max(s.reshape(nt * TK // 8, 8, L), axis=0))

    def pv_step(vt_ref, m, u0, nt):
        p = jnp.exp2((s_scr[step_rows(u0, nt), :] - m).astype(BF16))
        vt = jnp.concatenate([vt_ref[u0 + j] for j in range(nt)], axis=1)
        acc_scr[...] += jnp.dot(vt, p, preferred_element_type=F32)

    def over_tiles(u_lo, n, step):
        big = ATT_STEP_TILES
        nbig = lax.shift_right_logical(n, int(math.log2(big)))

        def body(j, carry):
            step(u_lo + big * j, big)
            return carry
        lax.fori_loop(0, nbig, body, 0)
        u = u_lo + big * nbig
        nt = big // 2
        while nt >= 1:
            pl.when((n & nt) != 0)(functools.partial(step, u, nt))
            u = u + (n & nt)
            nt //= 2

    def near_diagonal(step):
        for nt in range(1, n_win + 1):
            cond = (i == nt - 1) if nt < n_win else (i >= nt - 1)
            pl.when(cond)(functools.partial(step, i - (nt - 1), nt, (n_win - nt) * TK))

    def finish(c):
        w = gate_row(c) * (1.0 / acc_scr[HEAD_DIM:HEAD_DIM + 1, :])
        tot_scr[...] = tot_scr[...] + acc_scr[0:HEAD_DIM, :] * w

    def reset():
        m8_scr[...] = jnp.full_like(m8_scr, NEG_INF)
        acc_scr[...] = jnp.zeros_like(acc_scr)

    reset()
    n_far = jnp.maximum(i - 1, 0)
    over_tiles(0, n_far, lambda u0, nt: logits_step(ks_ref, u0, nt, None, selb_scr))
    pl.when(i >= 1)(lambda: logits_step(ks_ref, i - 1, 2, (n_win - 2) * TK, selm_scr))
    pl.when(i == 0)(lambda: logits_step(ks_ref, 0, 1, (n_win - 1) * TK, selm_scr))
    m_sel = jnp.max(m8_scr[...], axis=0, keepdims=True)
    over_tiles(0, i + 1, functools.partial(pv_step, vts_ref, m_sel))
    finish(1)

    reset()
    near_diagonal(lambda u0, nt, row0: logits_step(kw_ref, u0, nt, row0, None))
    m_win = jnp.max(m8_scr[...], axis=0, keepdims=True)
    near_diagonal(lambda u0, nt, row0: pv_step(vtw_ref, m_win, u0, nt))
    finish(2)

    for h in range(NSA_HPG):
        cols = slice(h * HEAD_DIM, (h + 1) * HEAD_DIM)
        z = z_ref[:, cols].astype(F32)
        o_ref[:, cols] = (tot_scr[:, h * TQ:(h + 1) * TQ].T * (z * _sigmoid(z))).astype(o_ref.dtype)


def _nsa_attention(proj, kcmp, vtc, vts, vtw, gt, win, cmpb, far, ovt, cols, batch, seq):
    TQ, TK = ATT_TQ, ATT_TK
    n = proj.shape[0]
    ng = NSA_GROUPS
    nq = seq // TQ
    L = NSA_HPG * TQ
    gw = NSA_HPG * HEAD_DIM

    def seq_block(col0):
        return pl.BlockSpec((seq, HEAD_DIM), lambda b, g, i: (b, col0 + g))

    def per_bg(a):
        return pl.BlockSpec((None, None) + a.shape[2:], lambda b, g, i: (b, g) + (0,) * (a.ndim - 2))

    def per_g(a):
        return pl.BlockSpec((None,) + a.shape[1:], lambda b, g, i: (g,) + (0,) * (a.ndim - 1))

    return pl.pallas_call(
        functools.partial(_nsa_attn_kernel, seq=seq),
        grid=(batch, ng, nq),
        in_specs=[pl.BlockSpec((TQ, gw), lambda b, g, i: (b * nq + i, cols["q"] + g)),
                  per_bg(kcmp), per_bg(vtc),
                  seq_block(cols["k_s"]), per_bg(vts),
                  seq_block(cols["k_w"]), per_bg(vtw),
                  pl.BlockSpec((TQ, gw), lambda b, g, i: (b * nq + i, cols["z"] + g)),
                  pl.BlockSpec((gt.shape[0], TQ), lambda b, g, i: (0, b * nq + i)),
                  per_g(win), per_g(cmpb), per_g(far),
                  pl.BlockSpec(ovt.shape, lambda b, g, i: (0, 0))],
        out_specs=pl.BlockSpec((TQ, gw), lambda b, g, i: (b * nq + i, g)),
        out_shape=jax.ShapeDtypeStruct((n, ng * gw), BF16),
        scratch_shapes=[pltpu.VMEM((L, HEAD_DIM), BF16),
                        pltpu.VMEM((seq // CMP_STRIDE, TQ), F32),
                        pltpu.VMEM((seq // SEL_BLOCK, TQ), jnp.int32),
                        pltpu.VMEM((seq // SEL_BLOCK, L), F32),
                        pltpu.VMEM((seq // SEL_BLOCK, L), F32),
                        pltpu.VMEM((seq, L), F32),
                        pltpu.VMEM((8, L), F32),
                        pltpu.VMEM((HEAD_DIM + V_PAD, L), F32),
                        pltpu.VMEM((HEAD_DIM, L), F32)],
        compiler_params=pltpu.CompilerParams(
            dimension_semantics=("arbitrary", "arbitrary", "arbitrary"),
            vmem_limit_bytes=ATT_VMEM_LIMIT),
        name="nsa_attention",
    )(proj, kcmp, vtc, proj, vts, proj, vtw, proj, gt, win, cmpb, far, ovt)


def _bucket_of_distance():
    d = np.arange(MAX_DISTANCE)
    max_exact = NUM_BUCKETS // 2
    large = max_exact + (np.log(np.maximum(d, 1).astype(np.float32) / max_exact)
                         / math.log(MAX_DISTANCE / max_exact) * (NUM_BUCKETS - max_exact)).astype(np.int32)
    large = np.minimum(large, NUM_BUCKETS - 1)
    return np.where(d < max_exact, d, large)


CMP_NEAR = 16


def _bias_builder_kernel(vw_ref, vc_ref, far_ref, win_ref, cmpb_ref, *, nc):
    TQ, TK = ATT_TQ, ATT_TK
    n_win = (WINDOW + TQ) // TK
    for h in range(NSA_HPG):
        lanes = slice(h * TQ, (h + 1) * TQ)
        for w in range(n_win):
            v = vw_ref[h * n_win + w:h * n_win + w + 1, :]
            r = pltpu.roll(jnp.broadcast_to(v, (TK, 2 * TQ)), 0, 1, stride=1, stride_axis=0)
            win_ref[w * TK:(w + 1) * TK, lanes] = r[:, TQ:2 * TQ]
        cmpb_ref[0:nc - CMP_NEAR, lanes] = jnp.broadcast_to(far_ref[:, lanes], (nc - CMP_NEAR, TQ))
        vc = vc_ref[h:h + 1, :]
        rc = pltpu.roll(jnp.broadcast_to(vc, (2 * CMP_NEAR, 4 * TQ)), 0, 1,
                        stride=CMP_STRIDE, stride_axis=0)
        cmpb_ref[nc - CMP_NEAR:nc + CMP_NEAR, lanes] = rc[:, 2 * TQ:3 * TQ]
        cmpb_ref[nc + CMP_NEAR:2 * nc, lanes] = jnp.full((nc - CMP_NEAR, TQ), NEG_INF, F32)


def _bias_tiles(rel_bias, seq):
    TQ, TK = ATT_TQ, ATT_TK
    assert TQ == TK and TQ >= MAX_DISTANCE
    assert CMP_STRIDE * CMP_NEAR >= TQ and 2 * TQ >= CMP_STRIDE * 2 * CMP_NEAR
    assert CMP_STRIDE * (CMP_NEAR + 1) - (CMP_BLOCK - 1) >= MAX_DISTANCE - 1
    nc = seq // CMP_STRIDE
    ng, hpg = NSA_GROUPS, NSA_HPG
    n_win = (WINDOW + TQ) // TK
    tb = (rel_bias.astype(F32) * LOG2E)[_bucket_of_distance()].T.reshape(ng, hpg, MAX_DISTANCE)

    def by_distance(dist, valid):
        vals = tb[:, :, np.clip(dist, 0, MAX_DISTANCE - 1)]
        return jnp.where(valid, vals, NEG_INF)

    x = np.arange(2 * TQ)[None, :]
    dw = WINDOW - TK * np.arange(n_win)[:, None] - TQ + x
    vw = by_distance(dw, (dw >= 0) & (dw < WINDOW)).reshape(ng, hpg * n_win, 2 * TQ)
    xc = np.arange(4 * TQ)
    dc = xc - 2 * TQ + CMP_STRIDE * CMP_NEAR - (CMP_BLOCK - 1)
    vc = by_distance(dc, dc >= 0)
    L = hpg * TQ
    far = jnp.broadcast_to(tb[:, :, MAX_DISTANCE - 1:], (ng, hpg, TQ)).reshape(ng, 1, L)

    win, cmpb = pl.pallas_call(
        functools.partial(_bias_builder_kernel, nc=nc),
        grid=(ng,),
        in_specs=[pl.BlockSpec((None,) + vw.shape[1:], lambda g: (g, 0, 0)),
                  pl.BlockSpec((None,) + vc.shape[1:], lambda g: (g, 0, 0)),
                  pl.BlockSpec((None,) + far.shape[1:], lambda g: (g, 0, 0))],
        out_specs=[pl.BlockSpec((None, WINDOW + TQ, L), lambda g: (g, 0, 0)),
                   pl.BlockSpec((None, 2 * nc, L), lambda g: (g, 0, 0))],
        out_shape=[jax.ShapeDtypeStruct((ng, WINDOW + TQ, L), F32),
                   jax.ShapeDtypeStruct((ng, 2 * nc, L), F32)],
        compiler_params=pltpu.CompilerParams(dimension_semantics=("arbitrary",)),
        name="nsa_bias_tiles",
    )(vw, vc, far)
    return win, cmpb, far


def _overlap_t(seq):
    nc = seq // CMP_STRIDE
    nsel = seq // SEL_BLOCK
    ci = np.arange(nc)[None, :]
    sj = np.arange(nsel)[:, None]
    ov = (CMP_STRIDE * ci < SEL_BLOCK * (sj + 1)) & (CMP_STRIDE * ci + CMP_BLOCK > SEL_BLOCK * sj)
    ov = ov & (ci < nc - 1)
    return jnp.asarray(ov, BF16)


def kernel(x, norm_gains, final_gain, rel_bias, hgrn_lb, hgrn_w_in, hgrn_head_gain, hgrn_w_out,
           nsa_w_in, nsa_pe_k, nsa_pe_v, nsa_phi_k_w1, nsa_phi_k_b1, nsa_phi_k_w2,
           nsa_phi_v_w1, nsa_phi_v_b1, nsa_phi_v_w2, nsa_w_out):
    batch, seq, d = x.shape
    n = batch * seq
    x2d = x.reshape(n, d)
    qscale = HEAD_DIM ** -0.5

    def scale_q_columns(w, scale):
        col = np.ones((1, w.shape[1]), np.float32)
        col[:, :d] = scale
        return (w * col).astype(BF16)

    proj = _norm_proj(x2d, norm_gains[0:1], scale_q_columns(hgrn_w_in[0], qscale))
    o = _hgrn_recurrence(proj, hgrn_lb, hgrn_head_gain[0:1], batch, seq)
    x1, h1 = _out_proj(o, hgrn_w_out[0].astype(BF16), x2d, norm_gains[1:2], final=False)

    kvw = NSA_GROUPS * HEAD_DIM
    w = nsa_w_in[0]
    gate0 = d + 6 * kvw
    gate1 = gate0 + NSA_HEADS * N_BRANCH
    ws = scale_q_columns(w, qscale * LOG2E)
    w_main = jnp.concatenate([ws[:, :gate0], ws[:, gate1:]], axis=1)
    wgt = jnp.zeros((HEAD_DIM, d), F32).at[:gate1 - gate0].set(w[:, gate0:gate1].T).astype(BF16)
    proj, gt = _proj_gate(h1, w_main, wgt)

    blk = {}
    for idx, name in enumerate(("k_c", "v_c", "k_s", "v_s", "k_w", "v_w")):
        blk[name] = (d + idx * kvw) // HEAD_DIM
    gw = NSA_HPG * HEAD_DIM
    cols = dict(blk, q=0, z=gate0 // gw)

    kcmp, vtc, vts, vtw = _kv_prep(
        proj, blk, nsa_pe_k[0], nsa_pe_v[0],
        nsa_phi_k_w1[0].astype(BF16), nsa_phi_v_w1[0].astype(BF16),
        nsa_phi_k_b1[0:1], nsa_phi_v_b1[0:1],
        nsa_phi_k_w2[0].astype(BF16), nsa_phi_v_w2[0].astype(BF16), batch, seq)

    win, cmpb, far = _bias_tiles(rel_bias, seq)
    o = _nsa_attention(proj, kcmp, vtc, vts, vtw, gt, win, cmpb, far, _overlap_t(seq), cols,
                       batch, seq)
    out, = _out_proj(o, nsa_w_out[0].astype(BF16), x1, final_gain[None, :], final=True)
    return out.reshape(batch, seq, d)
```

```python
import functools
import math

import numpy as np
import jax
import jax.numpy as jnp
from jax import lax
from jax.experimental import pallas as pl
from jax.experimental.pallas import tpu as pltpu

F32 = jnp.float32
BF16 = jnp.bfloat16

D_MODEL = 2048
RMS_EPS = 1e-6
HEAD_DIM = 128

HGRN_HEADS = D_MODEL // HEAD_DIM
HGRN_CHUNK = 64
HGRN_SUB = 16
HGRN_SAFE_LOG2 = 96.0

NSA_HEADS = D_MODEL // HEAD_DIM
NSA_GROUPS = 4
NSA_HPG = NSA_HEADS // NSA_GROUPS
N_BRANCH = 3
CMP_BLOCK = 32
CMP_STRIDE = 16
SEL_BLOCK = 64
N_SELECT = 16
WINDOW = 512
NUM_BUCKETS = 32
MAX_DISTANCE = 128
NEG_INF = -1e30
FORCE_SCORE = 1e9

ATT_TQ = 256
ATT_TK = 256
V_PAD = 16
ATT_STEP_TILES = 8
CMP_ROWS = 64
LOG2E = math.log2(math.e)
V7X_VMEM_BYTES = 64 * 1024 * 1024
VMEM_LIMIT = V7X_VMEM_BYTES * 3 // 4
ATT_VMEM_LIMIT = V7X_VMEM_BYTES * 7 // 8

NT_DIMS = (((1,), (1,)), ((), ()))
TN_DIMS = (((0,), (0,)), ((), ()))


def _sigmoid(x):
    return 1.0 / (1.0 + jnp.exp(-x))


def _split3(a):
    hi = a.astype(BF16)
    r1 = a - hi.astype(F32)
    mid = r1.astype(BF16)
    lo = (r1 - mid.astype(F32)).astype(BF16)
    return hi, mid, lo


def _norm_proj_kernel(x_ref, g_ref, w_ref, o_ref, h_scr):
    @pl.when(pl.program_id(1) == 0)
    def _():
        x = x_ref[...]
        ms = jnp.mean(x * x, axis=-1, keepdims=True)
        h_scr[...] = (x * lax.rsqrt(ms + RMS_EPS) * g_ref[...]).astype(BF16)

    o_ref[...] = jnp.dot(h_scr[...], w_ref[...], preferred_element_type=F32).astype(o_ref.dtype)


def _norm_proj(x, gain, w, *, tm=1024, tn=1024):
    n, d = x.shape
    nout = w.shape[1]
    return pl.pallas_call(
        _norm_proj_kernel,
        grid=(n // tm, nout // tn),
        in_specs=[pl.BlockSpec((tm, d), lambda i, j: (i, 0)),
                  pl.BlockSpec((1, d), lambda i, j: (0, 0)),
                  pl.BlockSpec((d, tn), lambda i, j: (0, j))],
        out_specs=pl.BlockSpec((tm, tn), lambda i, j: (i, j)),
        out_shape=jax.ShapeDtypeStruct((n, nout), BF16),
        scratch_shapes=[pltpu.VMEM((tm, d), BF16)],
        compiler_params=pltpu.CompilerParams(
            dimension_semantics=("arbitrary", "arbitrary"), vmem_limit_bytes=VMEM_LIMIT),
        name="norm_proj",
    )(x, gain, w)


def _proj_gate_kernel(h_ref, w_ref, wg_ref, o_ref, gt_ref):
    @pl.when(pl.program_id(1) == 0)
    def _():
        gt_ref[...] = jnp.dot(h_ref[...], wg_ref[...], preferred_element_type=F32).T

    o_ref[...] = jnp.dot(h_ref[...], w_ref[...], preferred_element_type=F32).astype(o_ref.dtype)


def _proj_gate(h, w, wg, *, tm=1024, tn=1024):
    n, d = h.shape
    nout = w.shape[1]
    ng = wg.shape[1]
    return pl.pallas_call(
        _proj_gate_kernel,
        grid=(n // tm, nout // tn),
        in_specs=[pl.BlockSpec((tm, d), lambda i, j: (i, 0)),
                  pl.BlockSpec((d, tn), lambda i, j: (0, j)),
                  pl.BlockSpec((d, ng), lambda i, j: (0, 0))],
        out_specs=[pl.BlockSpec((tm, tn), lambda i, j: (i, j)),
                   pl.BlockSpec((ng, tm), lambda i, j: (0, i))],
        out_shape=[jax.ShapeDtypeStruct((n, nout), BF16),
                   jax.ShapeDtypeStruct((ng, n), F32)],
        compiler_params=pltpu.CompilerParams(
            dimension_semantics=("arbitrary", "arbitrary"), vmem_limit_bytes=VMEM_LIMIT),
        name="proj_gate",
    )(h, w, wg)


def _out_proj_kernel(o_ref, w_ref, res_ref, g_ref, *out_refs, final):
    xn = res_ref[...] + jnp.dot(o_ref[...], w_ref[...], preferred_element_type=F32)
    ms = jnp.mean(xn * xn, axis=-1, keepdims=True)
    y = xn * lax.rsqrt(ms + RMS_EPS) * g_ref[...]
    if final:
        out_refs[0][...] = y
    else:
        out_refs[0][...] = xn
        out_refs[1][...] = y.astype(BF16)


def _out_proj(o, w, res, gain, *, final, tm=512):
    n, d = res.shape
    row = pl.BlockSpec((tm, d), lambda i: (i, 0))
    if final:
        out_specs = [row]
        out_shape = [jax.ShapeDtypeStruct((n, d), F32)]
    else:
        out_specs = [row, row]
        out_shape = [jax.ShapeDtypeStruct((n, d), F32), jax.ShapeDtypeStruct((n, d), BF16)]
    return pl.pallas_call(
        functools.partial(_out_proj_kernel, final=final),
        grid=(n // tm,),
        in_specs=[pl.BlockSpec((tm, o.shape[1]), lambda i: (i, 0)),
                  pl.BlockSpec(w.shape, lambda i: (0, 0)),
                  row,
                  pl.BlockSpec((1, d), lambda i: (0, 0))],
        out_specs=out_specs,
        out_shape=out_shape,
        compiler_params=pltpu.CompilerParams(
            dimension_semantics=("arbitrary",), vmem_limit_bytes=VMEM_LIMIT),
        name="out_proj_final" if final else "out_proj",
    )(o, w, res, gain)


def _hgrn_kernel(q_ref, f_ref, v_ref, z_ref, lbp_ref, hg_ref, o_ref,
                 st_scr, k_scr, b_scr, qe_scr, oi_scr, u_scr, d_scr, stb_scr, *, n_chunks):
    C, SUB = HGRN_CHUNK, HGRN_SUB

    @pl.when(pl.program_id(2) == 0)
    def _():
        st_scr[...] = jnp.zeros_like(st_scr)

    lbp = lbp_ref[...]
    e = jnp.exp(lbp - jnp.max(lbp, axis=0, keepdims=True))
    lb = e[0:1, :] / jnp.sum(e, axis=0, keepdims=True)
    hg = hg_ref[...]

    rr = lax.broadcasted_iota(jnp.int32, (C, C), 0)
    cc = lax.broadcasted_iota(jnp.int32, (C, C), 1)
    causal = rr >= cc
    tril = causal.astype(BF16)
    sub_c = lax.broadcasted_iota(jnp.int32, (SUB, C), 1)
    row8 = lax.broadcasted_iota(jnp.int32, (8, C), 0)
    col8 = lax.broadcasted_iota(jnp.int32, (8, C), 1)

    def emit(rows, o):
        z = z_ref[rows, :].astype(F32)
        ms = jnp.mean(o * o, axis=-1, keepdims=True)
        o_ref[rows, :] = (o * lax.rsqrt(ms + RMS_EPS) * hg * (z * _sigmoid(z))).astype(o_ref.dtype)

    chunks = [slice(c * C, (c + 1) * C) for c in range(n_chunks)]
    ks, gsplit = [], []
    for rows in chunks:
        sig = _sigmoid(f_ref[rows, :].astype(F32))
        gsplit.append(_split3(jnp.log(lb + (1.0 - lb) * sig)))
        ks.append((1.0 - lb) * (1.0 - sig))
        k_scr[rows, :] = ks[-1]
    b2s = []
    for rows, (ghi, gmid, glo) in zip(chunks, gsplit):
        b = (jnp.dot(tril, ghi, preferred_element_type=F32)
             + jnp.dot(tril, gmid, preferred_element_type=F32)
             + jnp.dot(tril, glo, preferred_element_type=F32))
        b2s.append(b * LOG2E)
        b_scr[rows, :] = b2s[-1]
    span = jnp.zeros((1, HEAD_DIM), F32)
    for b2 in b2s:
        span = jnp.maximum(span, -b2[C - 1:C])
    safe = jnp.max(span) < HGRN_SAFE_LOG2

    qes, kes = [], []
    for rows, k, b2 in zip(chunks, ks, b2s):
        qes.append((q_ref[rows, :].astype(F32) * jnp.exp2(b2)).astype(BF16))
        kes.append(k * jnp.exp2(-b2))
        qe_scr[rows, :] = qes[-1]
    scores = [lax.dot_general(qe, ke.astype(BF16), NT_DIMS, preferred_element_type=F32)
              for qe, ke in zip(qes, kes)]
    vt = v_ref[...].astype(F32).T.astype(BF16)
    zero = jnp.zeros((C, HEAD_DIM), BF16)
    for c, (rows, sc, ke, b2) in enumerate(zip(chunks, scores, kes, b2s)):
        bl = b2[C - 1:C]
        oi_scr[rows, :] = jnp.dot(jnp.where(causal, sc, 0.0).astype(BF16), v_ref[rows, :],
                                  preferred_element_type=F32)
        kd = (ke * jnp.exp2(bl)).astype(BF16)
        pair = slice(c // 2 * 2 * C, (c // 2 + 1) * 2 * C)
        u_scr[c] = jnp.dot(vt[:, pair], jnp.concatenate([kd, zero] if c % 2 == 0 else [zero, kd]),
                           preferred_element_type=F32)
        d_scr[c:c + 1, :] = jnp.exp2(bl)

    @pl.when(safe)
    def _():
        st = st_scr[...]
        for c in range(n_chunks):
            stb_scr[c] = st.astype(BF16)
            st = st * d_scr[c:c + 1, :] + u_scr[c]
        st_scr[...] = st
        carried = [lax.dot_general(qe_scr[rows, :], stb_scr[c], NT_DIMS, preferred_element_type=F32)
                   for c, rows in enumerate(chunks)]
        for rows, oc in zip(chunks, carried):
            emit(rows, oi_scr[rows, :] + oc)

    def robust_chunk(c):
        rows = slice(c * C, (c + 1) * C)
        q = q_ref[rows, :].astype(F32)
        v = v_ref[rows, :]
        k, b2 = k_scr[rows, :], b_scr[rows, :]
        st = st_scr[...]
        o = lax.dot_general((q * jnp.exp2(b2)).astype(BF16), st.astype(BF16), NT_DIMS,
                            preferred_element_type=F32)
        a_rows = []
        for i in range(C // SUB):
            lo, hi = i * SUB, (i + 1) * SUB
            bq, qq, kk = b2[lo:hi], q[lo:hi], k[lo:hi]
            if i > 0:
                r = b2[lo - 1:lo]
                qi = (qq * jnp.exp2(bq - r)).astype(BF16)
                ki = (k * jnp.exp2(jnp.minimum(r - b2, 0.0))).astype(BF16)
                a = lax.dot_general(qi, ki, NT_DIMS, preferred_element_type=F32)
                a = jnp.where(sub_c < lo, a, 0.0)
            else:
                a = jnp.zeros((SUB, C), F32)
            a8 = [a[j * 8:(j + 1) * 8] for j in range(SUB // 8)]
            for s in range(SUB):
                for j in range(s // 8, SUB // 8):
                    grp = slice(j * 8, (j + 1) * 8)
                    ed = qq[grp] * jnp.exp2(bq[grp] - bq[s:s + 1]) * kk[s:s + 1]
                    col = jnp.sum(ed, axis=-1, keepdims=True)
                    hit = col8 == lo + s
                    if j == s // 8:
                        hit = hit & (row8 >= s - j * 8)
                    a8[j] = jnp.where(hit, col, a8[j])
            a_rows.extend(a8)
        scores = jnp.concatenate(a_rows, axis=0)
        o = o + jnp.dot(scores.astype(BF16), v, preferred_element_type=F32)
        bl = b2[C - 1:C]
        st_scr[...] = st * jnp.exp2(bl) + lax.dot_general(
            v, (k * jnp.exp2(bl - b2)).astype(BF16), TN_DIMS, preferred_element_type=F32)
        emit(rows, o)

    @pl.when(jnp.logical_not(safe))
    def _():
        for c in range(n_chunks):
            robust_chunk(c)


def _hgrn_recurrence(proj, lb_param, head_gain, batch, seq, *, t_blk=1024):
    n = proj.shape[0]
    nh = HGRN_HEADS
    ns = seq // t_blk

    def sec(k):
        return pl.BlockSpec((t_blk, HEAD_DIM), lambda b, h, s, k=k: (b * ns + s, k * nh + h))

    n_chunks = t_blk // HGRN_CHUNK
    return pl.pallas_call(
        functools.partial(_hgrn_kernel, n_chunks=n_chunks),
        grid=(batch, nh, ns),
        in_specs=[sec(0), sec(1), sec(2), sec(3),
                  pl.BlockSpec((lb_param.shape[0], HEAD_DIM), lambda b, h, s: (0, h)),
                  pl.BlockSpec((1, HEAD_DIM), lambda b, h, s: (0, 0))],
        out_specs=pl.BlockSpec((t_blk, HEAD_DIM), lambda b, h, s: (b * ns + s, h)),
        out_shape=jax.ShapeDtypeStruct((n, nh * HEAD_DIM), BF16),
        scratch_shapes=[pltpu.VMEM((HEAD_DIM, HEAD_DIM), F32),
                        pltpu.VMEM((t_blk, HEAD_DIM), F32),
                        pltpu.VMEM((t_blk, HEAD_DIM), F32),
                        pltpu.VMEM((t_blk, HEAD_DIM), BF16),
                        pltpu.VMEM((t_blk, HEAD_DIM), F32),
                        pltpu.VMEM((n_chunks, HEAD_DIM, HEAD_DIM), F32),
                        pltpu.VMEM((n_chunks, HEAD_DIM), F32),
                        pltpu.VMEM((n_chunks, HEAD_DIM, HEAD_DIM), BF16)],
        compiler_params=pltpu.CompilerParams(
            dimension_semantics=("arbitrary", "arbitrary", "arbitrary")),
        name="hgrn_recurrence",
    )(proj, proj, proj, proj, lb_param, head_gain)


def _kv_prep_kernel(kc_ref, vc_ref, vs_ref, vw_ref, pek_ref, pev_ref, w1k_ref, w1v_ref,
                    b1k_ref, b1v_ref, w2k_ref, w2v_ref, ko_ref, vto_ref, vts_ref, vtw_ref, x_scr,
                    *, nc, ntile):
    half = CMP_BLOCK // 2
    TK = ATT_TK

    def phi(src_ref, pe_ref, w1_ref, b1_ref, w2_ref):
        x_scr[...] = src_ref[...].astype(F32)
        top = jnp.zeros((nc, HEAD_DIM), F32)
        bot = jnp.zeros((nc, HEAD_DIM), F32)
        for p in range(CMP_STRIDE):
            xp = x_scr[pl.ds(p, nc, stride=CMP_STRIDE), :]
            xa = (xp + pe_ref[p:p + 1, :]).astype(BF16)
            xb = (xp + pe_ref[half + p:half + p + 1, :]).astype(BF16)
            top = top + jnp.dot(xa, w1_ref[p * HEAD_DIM:(p + 1) * HEAD_DIM, :],
                                preferred_element_type=F32)
            bot = bot + jnp.dot(xb, w1_ref[(half + p) * HEAD_DIM:(half + p + 1) * HEAD_DIM, :],
                                preferred_element_type=F32)
        hid = top + pltpu.roll(bot, nc - 1, 0) + b1_ref[...]
        act = jax.nn.gelu(hid, approximate=True)
        return jnp.dot(act.astype(BF16), w2_ref[...], preferred_element_type=F32)

    ko_ref[...] = phi(kc_ref, pek_ref, w1k_ref, b1k_ref, w2k_ref).astype(BF16)
    vto_ref[...] = phi(vc_ref, pev_ref, w1v_ref, b1v_ref, w2v_ref).T.astype(BF16)

    ones_row = (lax.broadcasted_iota(jnp.int32, (V_PAD, TK), 0) == 0).astype(BF16)

    def tr(u, carry):
        rows = pl.ds(pl.multiple_of(u * TK, TK), TK)
        vts_ref[u] = jnp.concatenate([vs_ref[rows, :].astype(F32).T.astype(BF16), ones_row], axis=0)
        vtw_ref[u] = jnp.concatenate([vw_ref[rows, :].astype(F32).T.astype(BF16), ones_row], axis=0)
        return carry
    lax.fori_loop(0, ntile, tr, 0)


def _kv_prep(proj, blk, pe_k, pe_v, w1k, w1v, b1k, b1v, w2k, w2v, batch, seq):
    g = NSA_GROUPS
    nc = seq // CMP_STRIDE
    ntile = seq // ATT_TK

    def full(a):
        return pl.BlockSpec(a.shape, lambda b, gg: (0,) * a.ndim)

    def seq_block(col0):
        return pl.BlockSpec((seq, HEAD_DIM), lambda b, gg: (b, col0 + gg))

    def out(*shape):
        spec = pl.BlockSpec((None, None) + shape, lambda b, gg: (b, gg) + (0,) * len(shape))
        return spec, jax.ShapeDtypeStruct((batch, g) + shape, BF16)

    specs, shapes = zip(out(nc, HEAD_DIM), out(HEAD_DIM, nc),
                        out(ntile, HEAD_DIM + V_PAD, ATT_TK), out(ntile, HEAD_DIM + V_PAD, ATT_TK))
    return pl.pallas_call(
        functools.partial(_kv_prep_kernel, nc=nc, ntile=ntile),
        grid=(batch, g),
        in_specs=[seq_block(blk["k_c"]), seq_block(blk["v_c"]), seq_block(blk["v_s"]),
                  seq_block(blk["v_w"]),
                  full(pe_k), full(pe_v), full(w1k), full(w1v), full(b1k), full(b1v),
                  full(w2k), full(w2v)],
        out_specs=list(specs),
        out_shape=list(shapes),
        scratch_shapes=[pltpu.VMEM((seq, HEAD_DIM), F32)],
        compiler_params=pltpu.CompilerParams(dimension_semantics=("arbitrary", "arbitrary")),
        name="nsa_kv_prep",
    )(proj, proj, proj, proj, pe_k, pe_v, w1k, w1v, b1k, b1v, w2k, w2v)


def _nsa_attn_kernel(q_ref, kc_ref, vtc_ref, ks_ref, vts_ref, kw_ref, vtw_ref, z_ref, gt_ref,
                     win_ref, cmpb_ref, far_ref, ovt_ref, o_ref,
                     q_scr, psum_scr, rank_scr, selm_scr, selb_scr, s_scr, m8_scr, acc_scr, tot_scr,
                     *, seq):
    TQ, TK = ATT_TQ, ATT_TK
    L = NSA_HPG * TQ
    nc = seq // CMP_STRIDE
    nsel = seq // SEL_BLOCK
    blk_per_tile = TK // SEL_BLOCK
    n_win = (WINDOW + TQ) // TK
    g = pl.program_id(1)
    i = pl.program_id(2)
    t0 = i * TQ

    for h in range(NSA_HPG):
        q_scr[h * TQ:(h + 1) * TQ, :] = q_ref[:, h * HEAD_DIM:(h + 1) * HEAD_DIM]

    def gate_row(c):
        rows = [(g * NSA_HPG + h) * N_BRANCH + c for h in range(NSA_HPG)]
        return jnp.concatenate([_sigmoid(gt_ref[pl.ds(r, 1), :]) for r in rows], axis=1)

    cmp_row0 = pl.multiple_of(nc - i * (TQ // CMP_STRIDE), 8)

    def cmp_branch(rows):
        s = lax.dot_general(kc_ref[0:rows, :], q_scr[...], NT_DIMS, preferred_element_type=F32)
        s = s + cmpb_ref[pl.ds(cmp_row0, rows), :]
        mc = jnp.max(s, axis=0, keepdims=True)
        p = jnp.exp2(s - mc)
        inv = jnp.where(mc > 0.1 * NEG_INF, 1.0 / jnp.sum(p, axis=0, keepdims=True), 0.0)
        p = p * inv
        psum = p[:, 0:TQ]
        for h in range(1, NSA_HPG):
            psum = psum + p[:, h * TQ:(h + 1) * TQ]
        pb = p.astype(BF16)
        if rows < nc:
            psum = jnp.concatenate([psum, jnp.zeros((nc - rows, TQ), F32)], axis=0)
            pb = jnp.concatenate([pb, jnp.zeros((nc - rows, L), BF16)], axis=0)
        psum_scr[...] = psum
        oc = jnp.dot(vtc_ref[...], pb, preferred_element_type=F32)
        tot_scr[...] = oc * gate_row(0)

    qblk_per_rows = CMP_ROWS // (TQ // CMP_STRIDE)
    for k in range(1, nc // CMP_ROWS + 1):
        pl.when(i // qblk_per_rows + 1 == k)(functools.partial(cmp_branch, k * CMP_ROWS))

    ovt = ovt_ref[...]
    phi, pmid, plo = _split3(psum_scr[...])
    imp = (jnp.dot(ovt, phi, preferred_element_type=F32)
           + jnp.dot(ovt, pmid, preferred_element_type=F32)
           + jnp.dot(ovt, plo, preferred_element_type=F32))

    jrow = lax.broadcasted_iota(jnp.int32, (nsel, TQ), 0)
    tcol = t0 + lax.broadcasted_iota(jnp.int32, (nsel, TQ), 1)
    cur = lax.shift_right_arithmetic(tcol, int(math.log2(SEL_BLOCK)))
    forced = (jrow == 0) | (jrow == cur) | (jrow == cur - 1)
    visible = jrow * SEL_BLOCK <= tcol
    score = jnp.where(forced, FORCE_SCORE, jnp.where(visible, imp, NEG_INF))
    row8 = lax.broadcasted_iota(jnp.int32, (8, TQ), 0)
    score8 = [score[r * 8:(r + 1) * 8] for r in range(nsel // 8)]
    rank_scr[...] = jnp.zeros_like(rank_scr)
    n_visible = (i + 1) * (TQ // SEL_BLOCK)

    def count_ahead(grp):
        cnt = [jnp.zeros((8, TQ), jnp.int32) for _ in score8]
        for jp in range(grp * 8, grp * 8 + 8):
            row = score[jp:jp + 1, :]
            for r, sc in enumerate(score8):
                if r * 8 > jp:
                    ahead = row >= sc
                elif r * 8 + 7 < jp:
                    ahead = row > sc
                else:
                    ahead = (row > sc) | ((row == sc) & (row8 > jp - r * 8))
                cnt[r] = cnt[r] + ahead.astype(jnp.int32)
        for r, c in enumerate(cnt):
            rank_scr[r * 8:(r + 1) * 8, :] += c

    for grp in range(nsel // 8):
        pl.when(grp * 8 < n_visible)(functools.partial(count_ahead, grp))
    rank = rank_scr[...]
    selm = jnp.where(rank < min(N_SELECT, nsel), 0.0, NEG_INF)
    selm = jnp.concatenate([selm] * NSA_HPG, axis=1)
    selm_scr[...] = selm
    selb_scr[...] = selm + far_ref[...]

    def step_rows(u0, nt):
        return pl.ds(pl.multiple_of(u0 * TK, TK), nt * TK)

    def logits_step(k_ref, u0, nt, win_row0, sel_scr):
        s = lax.dot_general(k_ref[step_rows(u0, nt), :], q_scr[...], NT_DIMS,
                            preferred_element_type=F32)
        if win_row0 is not None:
            s = s + win_ref[win_row0:win_row0 + nt * TK, :]
        if sel_scr is not None:
            parts = [s[r * SEL_BLOCK:(r + 1) * SEL_BLOCK]
                     + sel_scr[pl.ds(u0 * blk_per_tile + r, 1), :] for r in range(nt * blk_per_tile)]
            s = jnp.concatenate(parts, axis=0)
        s_scr[step_rows(u0, nt), :] = s
        m8_scr[...] = jnp.maximum(m8_scr[...], jnp.max(s.reshape(nt * TK // 8, 8, L), axis=0))

    def pv_step(vt_ref, m, u0, nt):
        p = jnp.exp2((s_scr[step_rows(u0, nt), :] - m).astype(BF16))
        vt = jnp.concatenate([vt_ref[u0 + j] for j in range(nt)], axis=1)
        acc_scr[...] += jnp.dot(vt, p, preferred_element_type=F32)

    def over_tiles(u_lo, n, step):
        big = ATT_STEP_TILES
        nbig = lax.shift_right_logical(n, int(math.log2(big)))

        def body(j, carry):
            step(u_lo + big * j, big)
            return carry
        lax.fori_loop(0, nbig, body, 0)
        u = u_lo + big * nbig
        nt = big // 2
        while nt >= 1:
            pl.when((n & nt) != 0)(functools.partial(step, u, nt))
            u = u + (n & nt)
            nt //= 2

    def near_diagonal(step):
        for nt in range(1, n_win + 1):
            cond = (i == nt - 1) if nt < n_win else (i >= nt - 1)
            pl.when(cond)(functools.partial(step, i - (nt - 1), nt, (n_win - nt) * TK))

    def finish(c):
        w = gate_row(c) * (1.0 / acc_scr[HEAD_DIM:HEAD_DIM + 1, :])
        tot_scr[...] = tot_scr[...] + acc_scr[0:HEAD_DIM, :] * w

    def reset():
        m8_scr[...] = jnp.full_like(m8_scr, NEG_INF)
        acc_scr[...] = jnp.zeros_like(acc_scr)

    reset()
    n_far = jnp.maximum(i - 1, 0)
    over_tiles(0, n_far, lambda u0, nt: logits_step(ks_ref, u0, nt, None, selb_scr))
    pl.when(i >= 1)(lambda: logits_step(ks_ref, i - 1, 2, (n_win - 2) * TK, selm_scr))
    pl.when(i == 0)(lambda: logits_step(ks_ref, 0, 1, (n_win - 1) * TK, selm_scr))
    m_sel = jnp.max(m8_scr[...], axis=0, keepdims=True)
    over_tiles(0, i + 1, functools.partial(pv_step, vts_ref, m_sel))
    finish(1)

    reset()
    near_diagonal(lambda u0, nt, row0: logits_step(kw_ref, u0, nt, row0, None))
    m_win = jnp.max(m8_scr[...], axis=0, keepdims=True)
    near_diagonal(lambda u0, nt, row0: pv_step(vtw_ref, m_win, u0, nt))
    finish(2)

    for h in range(NSA_HPG):
        cols = slice(h * HEAD_DIM, (h + 1) * HEAD_DIM)
        z = z_ref[:, cols].astype(F32)
        o_ref[:, cols] = (tot_scr[:, h * TQ:(h + 1) * TQ].T * (z * _sigmoid(z))).astype(o_ref.dtype)


def _nsa_attention(proj, kcmp, vtc, vts, vtw, gt, win, cmpb, far, ovt, cols, batch, seq):
    TQ, TK = ATT_TQ, ATT_TK
    n = proj.shape[0]
    ng = NSA_GROUPS
    nq = seq // TQ
    L = NSA_HPG * TQ
    gw = NSA_HPG * HEAD_DIM

    def seq_block(col0):
        return pl.BlockSpec((seq, HEAD_DIM), lambda b, g, i: (b, col0 + g))

    def per_bg(a):
        return pl.BlockSpec((None, None) + a.shape[2:], lambda b, g, i: (b, g) + (0,) * (a.ndim - 2))

    def per_g(a):
        return pl.BlockSpec((None,) + a.shape[1:], lambda b, g, i: (g,) + (0,) * (a.ndim - 1))

    return pl.pallas_call(
        functools.partial(_nsa_attn_kernel, seq=seq),
        grid=(batch, ng, nq),
        in_specs=[pl.BlockSpec((TQ, gw), lambda b, g, i: (b * nq + i, cols["q"] + g)),
                  per_bg(kcmp), per_bg(vtc),
                  seq_block(cols["k_s"]), per_bg(vts),
                  seq_block(cols["k_w"]), per_bg(vtw),
                  pl.BlockSpec((TQ, gw), lambda b, g, i: (b * nq + i, cols["z"] + g)),
                  pl.BlockSpec((gt.shape[0], TQ), lambda b, g, i: (0, b * nq + i)),
                  per_g(win), per_g(cmpb), per_g(far),
                  pl.BlockSpec(ovt.shape, lambda b, g, i: (0, 0))],
        out_specs=pl.BlockSpec((TQ, gw), lambda b, g, i: (b * nq + i, g)),
        out_shape=jax.ShapeDtypeStruct((n, ng * gw), BF16),
        scratch_shapes=[pltpu.VMEM((L, HEAD_DIM), BF16),
                        pltpu.VMEM((seq // CMP_STRIDE, TQ), F32),
                        pltpu.VMEM((seq // SEL_BLOCK, TQ), jnp.int32),
                        pltpu.VMEM((seq // SEL_BLOCK, L), F32),
                        pltpu.VMEM((seq // SEL_BLOCK, L), F32),
                        pltpu.VMEM((seq, L), F32),
                        pltpu.VMEM((8, L), F32),
                        pltpu.VMEM((HEAD_DIM + V_PAD, L), F32),
                        pltpu.VMEM((HEAD_DIM, L), F32)],
        compiler_params=pltpu.CompilerParams(
            dimension_semantics=("arbitrary", "arbitrary", "arbitrary"),
            vmem_limit_bytes=ATT_VMEM_LIMIT),
        name="nsa_attention",
    )(proj, kcmp, vtc, proj, vts, proj, vtw, proj, gt, win, cmpb, far, ovt)


def _bucket_of_distance():
    d = np.arange(MAX_DISTANCE)
    max_exact = NUM_BUCKETS // 2
    large = max_exact + (np.log(np.maximum(d, 1).astype(np.float32) / max_exact)
                         / math.log(MAX_DISTANCE / max_exact) * (NUM_BUCKETS - max_exact)).astype(np.int32)
    large = np.minimum(large, NUM_BUCKETS - 1)
    return np.where(d < max_exact, d, large)


CMP_NEAR = 16


def _bias_builder_kernel(vw_ref, vc_ref, far_ref, win_ref, cmpb_ref, *, nc):
    TQ, TK = ATT_TQ, ATT_TK
    n_win = (WINDOW + TQ) // TK
    for h in range(NSA_HPG):
        lanes = slice(h * TQ, (h + 1) * TQ)
        for w in range(n_win):
            v = vw_ref[h * n_win + w:h * n_win + w + 1, :]
            r = pltpu.roll(jnp.broadcast_to(v, (TK, 2 * TQ)), 0, 1, stride=1, stride_axis=0)
            win_ref[w * TK:(w + 1) * TK, lanes] = r[:, TQ:2 * TQ]
        cmpb_ref[0:nc - CMP_NEAR, lanes] = jnp.broadcast_to(far_ref[:, lanes], (nc - CMP_NEAR, TQ))
        vc = vc_ref[h:h + 1, :]
        rc = pltpu.roll(jnp.broadcast_to(vc, (2 * CMP_NEAR, 4 * TQ)), 0, 1,
                        stride=CMP_STRIDE, stride_axis=0)
        cmpb_ref[nc - CMP_NEAR:nc + CMP_NEAR, lanes] = rc[:, 2 * TQ:3 * TQ]
        cmpb_ref[nc + CMP_NEAR:2 * nc, lanes] = jnp.full((nc - CMP_NEAR, TQ), NEG_INF, F32)


def _bias_tiles(rel_bias, seq):
    TQ, TK = ATT_TQ, ATT_TK
    assert TQ == TK and TQ >= MAX_DISTANCE
    assert CMP_STRIDE * CMP_NEAR >= TQ and 2 * TQ >= CMP_STRIDE * 2 * CMP_NEAR
    assert CMP_STRIDE * (CMP_NEAR + 1) - (CMP_BLOCK - 1) >= MAX_DISTANCE - 1
    nc = seq // CMP_STRIDE
    ng, hpg = NSA_GROUPS, NSA_HPG
    n_win = (WINDOW + TQ) // TK
    tb = (rel_bias.astype(F32) * LOG2E)[_bucket_of_distance()].T.reshape(ng, hpg, MAX_DISTANCE)

    def by_distance(dist, valid):
        vals = tb[:, :, np.clip(dist, 0, MAX_DISTANCE - 1)]
        return jnp.where(valid, vals, NEG_INF)

    x = np.arange(2 * TQ)[None, :]
    dw = WINDOW - TK * np.arange(n_win)[:, None] - TQ + x
    vw = by_distance(dw, (dw >= 0) & (dw < WINDOW)).reshape(ng, hpg * n_win, 2 * TQ)
    xc = np.arange(4 * TQ)
    dc = xc - 2 * TQ + CMP_STRIDE * CMP_NEAR - (CMP_BLOCK - 1)
    vc = by_distance(dc, dc >= 0)
    L = hpg * TQ
    far = jnp.broadcast_to(tb[:, :, MAX_DISTANCE - 1:], (ng, hpg, TQ)).reshape(ng, 1, L)

    win, cmpb = pl.pallas_call(
        functools.partial(_bias_builder_kernel, nc=nc),
        grid=(ng,),
        in_specs=[pl.BlockSpec((None,) + vw.shape[1:], lambda g: (g, 0, 0)),
                  pl.BlockSpec((None,) + vc.shape[1:], lambda g: (g, 0, 0)),
                  pl.BlockSpec((None,) + far.shape[1:], lambda g: (g, 0, 0))],
        out_specs=[pl.BlockSpec((None, WINDOW + TQ, L), lambda g: (g, 0, 0)),
                   pl.BlockSpec((None, 2 * nc, L), lambda g: (g, 0, 0))],
        out_shape=[jax.ShapeDtypeStruct((ng, WINDOW + TQ, L), F32),
                   jax.ShapeDtypeStruct((ng, 2 * nc, L), F32)],
        compiler_params=pltpu.CompilerParams(dimension_semantics=("arbitrary",)),
        name="nsa_bias_tiles",
    )(vw, vc, far)
    return win, cmpb, far


def _overlap_t(seq):
    nc = seq // CMP_STRIDE
    nsel = seq // SEL_BLOCK
    ci = np.arange(nc)[None, :]
    sj = np.arange(nsel)[:, None]
    ov = (CMP_STRIDE * ci < SEL_BLOCK * (sj + 1)) & (CMP_STRIDE * ci + CMP_BLOCK > SEL_BLOCK * sj)
    ov = ov & (ci < nc - 1)
    return jnp.asarray(ov, BF16)


def kernel(x, norm_gains, final_gain, rel_bias, hgrn_lb, hgrn_w_in, hgrn_head_gain, hgrn_w_out,
           nsa_w_in, nsa_pe_k, nsa_pe_v, nsa_phi_k_w1, nsa_phi_k_b1, nsa_phi_k_w2,
           nsa_phi_v_w1, nsa_phi_v_b1, nsa_phi_v_w2, nsa_w_out):
    batch, seq, d = x.shape
    n = batch * seq
    x2d = x.reshape(n, d)
    qscale = HEAD_DIM ** -0.5

    def scale_q_columns(w, scale):
        col = np.ones((1, w.shape[1]), np.float32)
        col[:, :d] = scale
        return (w * col).astype(BF16)

    proj = _norm_proj(x2d, norm_gains[0:1], scale_q_columns(hgrn_w_in[0], qscale))
    o = _hgrn_recurrence(proj, hgrn_lb, hgrn_head_gain[0:1], batch, seq)
    x1, h1 = _out_proj(o, hgrn_w_out[0].astype(BF16), x2d, norm_gains[1:2], final=False)

    kvw = NSA_GROUPS * HEAD_DIM
    w = nsa_w_in[0]
    gate0 = d + 6 * kvw
    gate1 = gate0 + NSA_HEADS * N_BRANCH
    ws = scale_q_columns(w, qscale * LOG2E)
    w_main = jnp.concatenate([ws[:, :gate0], ws[:, gate1:]], axis=1)
    wg = jnp.pad(ws[:, gate0:gate1], ((0, 0), (0, HEAD_DIM - (gate1 - gate0))))
    proj, gt = _proj_gate(h1, w_main, wg)

    blk = {}
    for idx, name in enumerate(("k_c", "v_c", "k_s", "v_s", "k_w", "v_w")):
        blk[name] = (d + idx * kvw) // HEAD_DIM
    gw = NSA_HPG * HEAD_DIM
    cols = dict(blk, q=0, z=gate0 // gw)

    kcmp, vtc, vts, vtw = _kv_prep(
        proj, blk, nsa_pe_k[0], nsa_pe_v[0],
        nsa_phi_k_w1[0].astype(BF16), nsa_phi_v_w1[0].astype(BF16),
        nsa_phi_k_b1[0:1], nsa_phi_v_b1[0:1],
        nsa_phi_k_w2[0].astype(BF16), nsa_phi_v_w2[0].astype(BF16), batch, seq)

    win, cmpb, far = _bias_tiles(rel_bias, seq)
    o = _nsa_attention(proj, kcmp, vtc, vts, vtw, gt, win, cmpb, far, _overlap_t(seq), cols,
                       batch, seq)
    out, = _out_proj(o, nsa_w_out[0].astype(BF16), x1, final_gain[None, :], final=True)
    return out.reshape(batch, seq, d)
```

```python
import functools
import math

import numpy as np
import jax
import jax.numpy as jnp
from jax import lax
from jax.experimental import pallas as pl
from jax.experimental.pallas import tpu as pltpu

F32 = jnp.float32
BF16 = jnp.bfloat16

D_MODEL = 2048
RMS_EPS = 1e-6
HEAD_DIM = 128

HGRN_HEADS = D_MODEL // HEAD_DIM
HGRN_CHUNK = 64
HGRN_SUB = 16
HGRN_SAFE_LOG2 = 96.0

NSA_HEADS = D_MODEL // HEAD_DIM
NSA_GROUPS = 4
NSA_HPG = NSA_HEADS // NSA_GROUPS
N_BRANCH = 3
CMP_BLOCK = 32
CMP_STRIDE = 16
SEL_BLOCK = 64
N_SELECT = 16
WINDOW = 512
NUM_BUCKETS = 32
MAX_DISTANCE = 128
NEG_INF = -1e30
FORCE_SCORE = 1e9

ATT_TQ = 256
ATT_TK = 256
V_PAD = 16
ATT_STEP_TILES = 8
CMP_ROWS = 64
LOG2E = math.log2(math.e)
V7X_VMEM_BYTES = 64 * 1024 * 1024
VMEM_LIMIT = V7X_VMEM_BYTES * 3 // 4
ATT_VMEM_LIMIT = V7X_VMEM_BYTES * 7 // 8

NT_DIMS = (((1,), (1,)), ((), ()))
TN_DIMS = (((0,), (0,)), ((), ()))


def _sigmoid(x):
    return 1.0 / (1.0 + jnp.exp(-x))


def _split3(a):
    hi = a.astype(BF16)
    r1 = a - hi.astype(F32)
    mid = r1.astype(BF16)
    lo = (r1 - mid.astype(F32)).astype(BF16)
    return hi, mid, lo


def _norm_proj_kernel(x_ref, g_ref, w_ref, o_ref, h_scr):
    @pl.when(pl.program_id(1) == 0)
    def _():
        x = x_ref[...]
        ms = jnp.mean(x * x, axis=-1, keepdims=True)
        h_scr[...] = (x * lax.rsqrt(ms + RMS_EPS) * g_ref[...]).astype(BF16)

    o_ref[...] = jnp.dot(h_scr[...], w_ref[...], preferred_element_type=F32).astype(o_ref.dtype)


def _norm_proj(x, gain, w, *, tm=1024, tn=2048):
    n, d = x.shape
    nout = w.shape[1]
    return pl.pallas_call(
        _norm_proj_kernel,
        grid=(n // tm, nout // tn),
        in_specs=[pl.BlockSpec((tm, d), lambda i, j: (i, 0)),
                  pl.BlockSpec((1, d), lambda i, j: (0, 0)),
                  pl.BlockSpec((d, tn), lambda i, j: (0, j))],
        out_specs=pl.BlockSpec((tm, tn), lambda i, j: (i, j)),
        out_shape=jax.ShapeDtypeStruct((n, nout), BF16),
        scratch_shapes=[pltpu.VMEM((tm, d), BF16)],
        compiler_params=pltpu.CompilerParams(
            dimension_semantics=("arbitrary", "arbitrary"), vmem_limit_bytes=VMEM_LIMIT),
        name="norm_proj",
    )(x, gain, w)


def _proj_gate_kernel(h_ref, w_ref, wg_ref, o_ref, gt_ref):
    @pl.when(pl.program_id(1) == 0)
    def _():
        gt_ref[...] = jnp.dot(h_ref[...], wg_ref[...], preferred_element_type=F32).T

    o_ref[...] = jnp.dot(h_ref[...], w_ref[...], preferred_element_type=F32).astype(o_ref.dtype)


def _proj_gate(h, w, wg, *, tm=1024, tn=1792):
    n, d = h.shape
    nout = w.shape[1]
    ng = wg.shape[1]
    return pl.pallas_call(
        _proj_gate_kernel,
        grid=(n // tm, nout // tn),
        in_specs=[pl.BlockSpec((tm, d), lambda i, j: (i, 0)),
                  pl.BlockSpec((d, tn), lambda i, j: (0, j)),
                  pl.BlockSpec((d, ng), lambda i, j: (0, 0))],
        out_specs=[pl.BlockSpec((tm, tn), lambda i, j: (i, j)),
                   pl.BlockSpec((ng, tm), lambda i, j: (0, i))],
        out_shape=[jax.ShapeDtypeStruct((n, nout), BF16),
                   jax.ShapeDtypeStruct((ng, n), F32)],
        compiler_params=pltpu.CompilerParams(
            dimension_semantics=("arbitrary", "arbitrary"), vmem_limit_bytes=VMEM_LIMIT),
        name="proj_gate",
    )(h, w, wg)


def _nsa_weight_prep_kernel(wa_ref, wb_ref, o_ref, *, q_blocks, shifted_from, shift, scale):
    j = pl.program_id(1)
    tc = o_ref.shape[1]

    @pl.when(j < shifted_from)
    def _():
        o_ref[...] = (wa_ref[...] * jnp.where(j < q_blocks, scale, 1.0)).astype(BF16)

    @pl.when(j >= shifted_from)
    def _():
        both = jnp.concatenate([wa_ref[...], wb_ref[:, 0:HEAD_DIM]], axis=1)
        o_ref[...] = both[:, shift:shift + tc].astype(BF16)


def _nsa_weight_prep(w3, d, gate0, gate1, scale, *, tr=256, tc=1024):
    n_proj = w3.shape[2]
    n_out = n_proj - (gate1 - gate0)
    shift = gate1 - gate0
    assert gate0 % tc == 0 and n_out % tc == 0 and d % tc == 0 and shift <= HEAD_DIM
    last_src = (n_proj - 1) // tc
    first_b = gate0 // tc + 1
    return pl.pallas_call(
        functools.partial(_nsa_weight_prep_kernel, q_blocks=d // tc, shifted_from=gate0 // tc,
                          shift=shift, scale=scale),
        grid=(d // tr, n_out // tc),
        in_specs=[pl.BlockSpec((None, tr, tc), lambda r, j: (0, r, j)),
                  pl.BlockSpec((None, tr, tc),
                               lambda r, j: (0, r, jnp.minimum(jnp.maximum(j + 1, first_b), last_src)))],
        out_specs=pl.BlockSpec((tr, tc), lambda r, j: (r, j)),
        out_shape=jax.ShapeDtypeStruct((d, n_out), BF16),
        compiler_params=pltpu.CompilerParams(dimension_semantics=("arbitrary", "arbitrary")),
        name="nsa_weight_prep",
    )(w3, w3)


def _out_proj_kernel(o_ref, w_ref, res_ref, g_ref, *out_refs, final):
    xn = res_ref[...] + jnp.dot(o_ref[...], w_ref[...], preferred_element_type=F32)
    ms = jnp.mean(xn * xn, axis=-1, keepdims=True)
    y = xn * lax.rsqrt(ms + RMS_EPS) * g_ref[...]
    if final:
        out_refs[0][...] = y
    else:
        out_refs[0][...] = xn
        out_refs[1][...] = y.astype(BF16)


def _out_proj(o, w, res, gain, *, final, tm=512):
    n, d = res.shape
    row = pl.BlockSpec((tm, d), lambda i: (i, 0))
    if final:
        out_specs = [row]
        out_shape = [jax.ShapeDtypeStruct((n, d), F32)]
    else:
        out_specs = [row, row]
        out_shape = [jax.ShapeDtypeStruct((n, d), F32), jax.ShapeDtypeStruct((n, d), BF16)]
    return pl.pallas_call(
        functools.partial(_out_proj_kernel, final=final),
        grid=(n // tm,),
        in_specs=[pl.BlockSpec((tm, o.shape[1]), lambda i: (i, 0)),
                  pl.BlockSpec(w.shape, lambda i: (0, 0)),
                  row,
                  pl.BlockSpec((1, d), lambda i: (0, 0))],
        out_specs=out_specs,
        out_shape=out_shape,
        compiler_params=pltpu.CompilerParams(
            dimension_semantics=("arbitrary",), vmem_limit_bytes=VMEM_LIMIT),
        name="out_proj_final" if final else "out_proj",
    )(o, w, res, gain)


def _hgrn_kernel(q_ref, f_ref, v_ref, z_ref, lbp_ref, hg_ref, o_ref,
                 st_scr, k_scr, b_scr, qe_scr, oi_scr, u_scr, d_scr, stb_scr, *, n_chunks):
    C, SUB = HGRN_CHUNK, HGRN_SUB

    @pl.when(pl.program_id(2) == 0)
    def _():
        st_scr[...] = jnp.zeros_like(st_scr)

    lbp = lbp_ref[...]
    e = jnp.exp(lbp - jnp.max(lbp, axis=0, keepdims=True))
    lb = e[0:1, :] / jnp.sum(e, axis=0, keepdims=True)
    hg = hg_ref[...]

    rr = lax.broadcasted_iota(jnp.int32, (C, C), 0)
    cc = lax.broadcasted_iota(jnp.int32, (C, C), 1)
    causal = rr >= cc
    tril = causal.astype(BF16)
    sub_c = lax.broadcasted_iota(jnp.int32, (SUB, C), 1)
    row8 = lax.broadcasted_iota(jnp.int32, (8, C), 0)
    col8 = lax.broadcasted_iota(jnp.int32, (8, C), 1)

    def emit(rows, o):
        z = z_ref[rows, :].astype(F32)
        ms = jnp.mean(o * o, axis=-1, keepdims=True)
        o_ref[rows, :] = (o * lax.rsqrt(ms + RMS_EPS) * hg * (z * _sigmoid(z))).astype(o_ref.dtype)

    chunks = [slice(c * C, (c + 1) * C) for c in range(n_chunks)]
    ks, gsplit = [], []
    for rows in chunks:
        sig = _sigmoid(f_ref[rows, :].astype(F32))
        gsplit.append(_split3(jnp.log(lb + (1.0 - lb) * sig)))
        ks.append((1.0 - lb) * (1.0 - sig))
        k_scr[rows, :] = ks[-1]
    b2s = []
    for rows, (ghi, gmid, glo) in zip(chunks, gsplit):
        b = (jnp.dot(tril, ghi, preferred_element_type=F32)
             + jnp.dot(tril, gmid, preferred_element_type=F32)
             + jnp.dot(tril, glo, preferred_element_type=F32))
        b2s.append(b * LOG2E)
        b_scr[rows, :] = b2s[-1]
    span = jnp.zeros((1, HEAD_DIM), F32)
    for b2 in b2s:
        span = jnp.maximum(span, -b2[C - 1:C])
    safe = jnp.max(span) < HGRN_SAFE_LOG2

    qes, kes = [], []
    for rows, k, b2 in zip(chunks, ks, b2s):
        qes.append((q_ref[rows, :].astype(F32) * jnp.exp2(b2)).astype(BF16))
        kes.append(k * jnp.exp2(-b2))
        qe_scr[rows, :] = qes[-1]
    scores = [lax.dot_general(qe, ke.astype(BF16), NT_DIMS, preferred_element_type=F32)
              for qe, ke in zip(qes, kes)]
    vt = v_ref[...].astype(F32).T.astype(BF16)
    zero = jnp.zeros((C, HEAD_DIM), BF16)
    for c, (rows, sc, ke, b2) in enumerate(zip(chunks, scores, kes, b2s)):
        bl = b2[C - 1:C]
        oi_scr[rows, :] = jnp.dot(jnp.where(causal, sc, 0.0).astype(BF16), v_ref[rows, :],
                                  preferred_element_type=F32)
        kd = (ke * jnp.exp2(bl)).astype(BF16)
        pair = slice(c // 2 * 2 * C, (c // 2 + 1) * 2 * C)
        u_scr[c] = jnp.dot(vt[:, pair], jnp.concatenate([kd, zero] if c % 2 == 0 else [zero, kd]),
                           preferred_element_type=F32)
        d_scr[c:c + 1, :] = jnp.exp2(bl)

    @pl.when(safe)
    def _():
        st = st_scr[...]
        for c in range(n_chunks):
            stb_scr[c] = st.astype(BF16)
            st = st * d_scr[c:c + 1, :] + u_scr[c]
        st_scr[...] = st
        carried = [lax.dot_general(qe_scr[rows, :], stb_scr[c], NT_DIMS, preferred_element_type=F32)
                   for c, rows in enumerate(chunks)]
        for rows, oc in zip(chunks, carried):
            emit(rows, oi_scr[rows, :] + oc)

    def robust_chunk(c):
        rows = slice(c * C, (c + 1) * C)
        q = q_ref[rows, :].astype(F32)
        v = v_ref[rows, :]
        k, b2 = k_scr[rows, :], b_scr[rows, :]
        st = st_scr[...]
        o = lax.dot_general((q * jnp.exp2(b2)).astype(BF16), st.astype(BF16), NT_DIMS,
                            preferred_element_type=F32)
        a_rows = []
        for i in range(C // SUB):
            lo, hi = i * SUB, (i + 1) * SUB
            bq, qq, kk = b2[lo:hi], q[lo:hi], k[lo:hi]
            if i > 0:
                r = b2[lo - 1:lo]
                qi = (qq * jnp.exp2(bq - r)).astype(BF16)
                ki = (k * jnp.exp2(jnp.minimum(r - b2, 0.0))).astype(BF16)
                a = lax.dot_general(qi, ki, NT_DIMS, preferred_element_type=F32)
                a = jnp.where(sub_c < lo, a, 0.0)
            else:
                a = jnp.zeros((SUB, C), F32)
            a8 = [a[j * 8:(j + 1) * 8] for j in range(SUB // 8)]
            for s in range(SUB):
                for j in range(s // 8, SUB // 8):
                    grp = slice(j * 8, (j + 1) * 8)
                    ed = qq[grp] * jnp.exp2(bq[grp] - bq[s:s + 1]) * kk[s:s + 1]
                    col = jnp.sum(ed, axis=-1, keepdims=True)
                    hit = col8 == lo + s
                    if j == s // 8:
                        hit = hit & (row8 >= s - j * 8)
                    a8[j] = jnp.where(hit, col, a8[j])
            a_rows.extend(a8)
        scores = jnp.concatenate(a_rows, axis=0)
        o = o + jnp.dot(scores.astype(BF16), v, preferred_element_type=F32)
        bl = b2[C - 1:C]
        st_scr[...] = st * jnp.exp2(bl) + lax.dot_general(
            v, (k * jnp.exp2(bl - b2)).astype(BF16), TN_DIMS, preferred_element_type=F32)
        emit(rows, o)

    @pl.when(jnp.logical_not(safe))
    def _():
        for c in range(n_chunks):
            robust_chunk(c)


def _hgrn_recurrence(proj, lb_param, head_gain, batch, seq, *, t_blk=1024):
    n = proj.shape[0]
    nh = HGRN_HEADS
    ns = seq // t_blk

    def sec(k):
        return pl.BlockSpec((t_blk, HEAD_DIM), lambda b, h, s, k=k: (b * ns + s, k * nh + h))

    n_chunks = t_blk // HGRN_CHUNK
    return pl.pallas_call(
        functools.partial(_hgrn_kernel, n_chunks=n_chunks),
        grid=(batch, nh, ns),
        in_specs=[sec(0), sec(1), sec(2), sec(3),
                  pl.BlockSpec((lb_param.shape[0], HEAD_DIM), lambda b, h, s: (0, h)),
                  pl.BlockSpec((1, HEAD_DIM), lambda b, h, s: (0, 0))],
        out_specs=pl.BlockSpec((t_blk, HEAD_DIM), lambda b, h, s: (b * ns + s, h)),
        out_shape=jax.ShapeDtypeStruct((n, nh * HEAD_DIM), BF16),
        scratch_shapes=[pltpu.VMEM((HEAD_DIM, HEAD_DIM), F32),
                        pltpu.VMEM((t_blk, HEAD_DIM), F32),
                        pltpu.VMEM((t_blk, HEAD_DIM), F32),
                        pltpu.VMEM((t_blk, HEAD_DIM), BF16),
                        pltpu.VMEM((t_blk, HEAD_DIM), F32),
                        pltpu.VMEM((n_chunks, HEAD_DIM, HEAD_DIM), F32),
                        pltpu.VMEM((n_chunks, HEAD_DIM), F32),
                        pltpu.VMEM((n_chunks, HEAD_DIM, HEAD_DIM), BF16)],
        compiler_params=pltpu.CompilerParams(
            dimension_semantics=("arbitrary", "arbitrary", "arbitrary")),
        name="hgrn_recurrence",
    )(proj, proj, proj, proj, lb_param, head_gain)


def _kv_prep_kernel(kc_ref, vc_ref, vs_ref, vw_ref, pek_ref, pev_ref, w1k_ref, w1v_ref,
                    b1k_ref, b1v_ref, w2k_ref, w2v_ref, ko_ref, vto_ref, vts_ref, vtw_ref, x_scr,
                    *, nc, ntile):
    half = CMP_BLOCK // 2
    TK = ATT_TK

    def phi(src_ref, pe_ref, w1_ref, b1_ref, w2_ref):
        x_scr[...] = src_ref[...].astype(F32)
        top = jnp.zeros((nc, HEAD_DIM), F32)
        bot = jnp.zeros((nc, HEAD_DIM), F32)
        for p in range(CMP_STRIDE):
            xp = x_scr[pl.ds(p, nc, stride=CMP_STRIDE), :]
            xa = (xp + pe_ref[p:p + 1, :]).astype(BF16)
            xb = (xp + pe_ref[half + p:half + p + 1, :]).astype(BF16)
            top = top + jnp.dot(xa, w1_ref[p * HEAD_DIM:(p + 1) * HEAD_DIM, :],
                                preferred_element_type=F32)
            bot = bot + jnp.dot(xb, w1_ref[(half + p) * HEAD_DIM:(half + p + 1) * HEAD_DIM, :],
                                preferred_element_type=F32)
        hid = top + pltpu.roll(bot, nc - 1, 0) + b1_ref[...]
        act = jax.nn.gelu(hid, approximate=True)
        return jnp.dot(act.astype(BF16), w2_ref[...], preferred_element_type=F32)

    ko_ref[...] = phi(kc_ref, pek_ref, w1k_ref, b1k_ref, w2k_ref).astype(BF16)
    vto_ref[...] = phi(vc_ref, pev_ref, w1v_ref, b1v_ref, w2v_ref).T.astype(BF16)

    ones_row = (lax.broadcasted_iota(jnp.int32, (V_PAD, TK), 0) == 0).astype(BF16)

    def tr(u, carry):
        rows = pl.ds(pl.multiple_of(u * TK, TK), TK)
        vts_ref[u] = jnp.concatenate([vs_ref[rows, :].astype(F32).T.astype(BF16), ones_row], axis=0)
        vtw_ref[u] = jnp.concatenate([vw_ref[rows, :].astype(F32).T.astype(BF16), ones_row], axis=0)
        return carry
    lax.fori_loop(0, ntile, tr, 0)


def _kv_prep(proj, blk, pe_k, pe_v, w1k, w1v, b1k, b1v, w2k, w2v, batch, seq):
    g = NSA_GROUPS
    nc = seq // CMP_STRIDE
    ntile = seq // ATT_TK

    def full(a):
        return pl.BlockSpec(a.shape, lambda b, gg: (0,) * a.ndim)

    def seq_block(col0):
        return pl.BlockSpec((seq, HEAD_DIM), lambda b, gg: (b, col0 + gg))

    def out(*shape):
        spec = pl.BlockSpec((None, None) + shape, lambda b, gg: (b, gg) + (0,) * len(shape))
        return spec, jax.ShapeDtypeStruct((batch, g) + shape, BF16)

    specs, shapes = zip(out(nc, HEAD_DIM), out(HEAD_DIM, nc),
                        out(ntile, HEAD_DIM + V_PAD, ATT_TK), out(ntile, HEAD_DIM + V_PAD, ATT_TK))
    return pl.pallas_call(
        functools.partial(_kv_prep_kernel, nc=nc, ntile=ntile),
        grid=(batch, g),
        in_specs=[seq_block(blk["k_c"]), seq_block(blk["v_c"]), seq_block(blk["v_s"]),
                  seq_block(blk["v_w"]),
                  full(pe_k), full(pe_v), full(w1k), full(w1v), full(b1k), full(b1v),
                  full(w2k), full(w2v)],
        out_specs=list(specs),
        out_shape=list(shapes),
        scratch_shapes=[pltpu.VMEM((seq, HEAD_DIM), F32)],
        compiler_params=pltpu.CompilerParams(dimension_semantics=("arbitrary", "arbitrary")),
        name="nsa_kv_prep",
    )(proj, proj, proj, proj, pe_k, pe_v, w1k, w1v, b1k, b1v, w2k, w2v)


def _nsa_attn_kernel(q_ref, kc_ref, vtc_ref, ks_ref, vts_ref, kw_ref, vtw_ref, z_ref, gt_ref,
                     win_ref, cmpb_ref, far_ref, ovt_ref, o_ref,
                     q_scr, psum_scr, rank_scr, selm_scr, selb_scr, s_scr, m8_scr, acc_scr, tot_scr,
                     *, seq):
    TQ, TK = ATT_TQ, ATT_TK
    L = NSA_HPG * TQ
    nc = seq // CMP_STRIDE
    nsel = seq // SEL_BLOCK
    blk_per_tile = TK // SEL_BLOCK
    n_win = (WINDOW + TQ) // TK
    g = pl.program_id(1)
    i = pl.program_id(2)
    t0 = i * TQ

    for h in range(NSA_HPG):
        q_scr[h * TQ:(h + 1) * TQ, :] = q_ref[:, h * HEAD_DIM:(h + 1) * HEAD_DIM]

    def gate_row(c):
        rows = [(g * NSA_HPG + h) * N_BRANCH + c for h in range(NSA_HPG)]
        return jnp.concatenate([_sigmoid(gt_ref[pl.ds(r, 1), :]) for r in rows], axis=1)

    cmp_row0 = pl.multiple_of(nc - i * (TQ // CMP_STRIDE), 8)

    def cmp_branch(rows):
        s = lax.dot_general(kc_ref[0:rows, :], q_scr[...], NT_DIMS, preferred_element_type=F32)
        s = s + cmpb_ref[pl.ds(cmp_row0, rows), :]
        mc = jnp.max(s, axis=0, keepdims=True)
        p = jnp.exp2(s - mc)
        inv = jnp.where(mc > 0.1 * NEG_INF, 1.0 / jnp.sum(p, axis=0, keepdims=True), 0.0)
        p = p * inv
        psum = p[:, 0:TQ]
        for h in range(1, NSA_HPG):
            psum = psum + p[:, h * TQ:(h + 1) * TQ]
        pb = p.astype(BF16)
        if rows < nc:
            psum = jnp.concatenate([psum, jnp.zeros((nc - rows, TQ), F32)], axis=0)
            pb = jnp.concatenate([pb, jnp.zeros((nc - rows, L), BF16)], axis=0)
        psum_scr[...] = psum
        oc = jnp.dot(vtc_ref[...], pb, preferred_element_type=F32)
        tot_scr[...] = oc * gate_row(0)

    qblk_per_rows = CMP_ROWS // (TQ // CMP_STRIDE)
    for k in range(1, nc // CMP_ROWS + 1):
        pl.when(i // qblk_per_rows + 1 == k)(functools.partial(cmp_branch, k * CMP_ROWS))

    ovt = ovt_ref[...]
    phi, pmid, plo = _split3(psum_scr[...])
    imp = (jnp.dot(ovt, phi, preferred_element_type=F32)
           + jnp.dot(ovt, pmid, preferred_element_type=F32)
           + jnp.dot(ovt, plo, preferred_element_type=F32))

    jrow = lax.broadcasted_iota(jnp.int32, (nsel, TQ), 0)
    tcol = t0 + lax.broadcasted_iota(jnp.int32, (nsel, TQ), 1)
    cur = lax.shift_right_arithmetic(tcol, int(math.log2(SEL_BLOCK)))
    forced = (jrow == 0) | (jrow == cur) | (jrow == cur - 1)
    visible = jrow * SEL_BLOCK <= tcol
    score = jnp.where(forced, FORCE_SCORE, jnp.where(visible, imp, NEG_INF))
    row8 = lax.broadcasted_iota(jnp.int32, (8, TQ), 0)
    score8 = [score[r * 8:(r + 1) * 8] for r in range(nsel // 8)]
    rank_scr[...] = jnp.zeros_like(rank_scr)
    n_visible = (i + 1) * (TQ // SEL_BLOCK)

    def count_ahead(grp):
        cnt = [jnp.zeros((8, TQ), jnp.int32) for _ in score8]
        for jp in range(grp * 8, grp * 8 + 8):
            row = score[jp:jp + 1, :]
            for r, sc in enumerate(score8):
                if r * 8 > jp:
                    ahead = row >= sc
                elif r * 8 + 7 < jp:
                    ahead = row > sc
                else:
                    ahead = (row > sc) | ((row == sc) & (row8 > jp - r * 8))
                cnt[r] = cnt[r] + ahead.astype(jnp.int32)
        for r, c in enumerate(cnt):
            rank_scr[r * 8:(r + 1) * 8, :] += c

    for grp in range(nsel // 8):
        pl.when(grp * 8 < n_visible)(functools.partial(count_ahead, grp))
    rank = rank_scr[...]
    selm = jnp.where(rank < min(N_SELECT, nsel), 0.0, NEG_INF)
    selm = jnp.concatenate([selm] * NSA_HPG, axis=1)
    selm_scr[...] = selm
    selb_scr[...] = selm + far_ref[...]

    def step_rows(u0, nt):
        return pl.ds(pl.multiple_of(u0 * TK, TK), nt * TK)

    def logits_step(k_ref, u0, nt, win_row0, sel_scr):
        s = lax.dot_general(k_ref[step_rows(u0, nt), :], q_scr[...], NT_DIMS,
                            preferred_element_type=F32)
        if win_row0 is not None:
            s = s + win_ref[win_row0:win_row0 + nt * TK, :]
        if sel_scr is not None:
            parts = [s[r * SEL_BLOCK:(r + 1) * SEL_BLOCK]
                     + sel_scr[pl.ds(u0 * blk_per_tile + r, 1), :] for r in range(nt * blk_per_tile)]
            s = jnp.concatenate(parts, axis=0)
        s_scr[step_rows(u0, nt), :] = s
        m8_scr[...] = jnp.maximum(m8_scr[...], jnp.max(s.reshape(nt * TK // 8, 8, L), axis=0))

    def pv_step(vt_ref, m, u0, nt):
        p = jnp.exp2((s_scr[step_rows(u0, nt), :] - m).astype(BF16))
        vt = jnp.concatenate([vt_ref[u0 + j] for j in range(nt)], axis=1)
        acc_scr[...] += jnp.dot(vt, p, preferred_element_type=F32)

    def over_tiles(u_lo, n, step):
        big = ATT_STEP_TILES
        nbig = lax.shift_right_logical(n, int(math.log2(big)))

        def body(j, carry):
            step(u_lo + big * j, big)
            return carry
        lax.fori_loop(0, nbig, body, 0)
        u = u_lo + big * nbig
        nt = big // 2
        while nt >= 1:
            pl.when((n & nt) != 0)(functools.partial(step, u, nt))
            u = u + (n & nt)
            nt //= 2

    def near_diagonal(step):
        for nt in range(1, n_win + 1):
            cond = (i == nt - 1) if nt < n_win else (i >= nt - 1)
            pl.when(cond)(functools.partial(step, i - (nt - 1), nt, (n_win - nt) * TK))

    def finish(c):
        w = gate_row(c) * (1.0 / acc_scr[HEAD_DIM:HEAD_DIM + 1, :])
        tot_scr[...] = tot_scr[...] + acc_scr[0:HEAD_DIM, :] * w

    def reset():
        m8_scr[...] = jnp.full_like(m8_scr, NEG_INF)
        acc_scr[...] = jnp.zeros_like(acc_scr)

    reset()
    n_far = jnp.maximum(i - 1, 0)
    over_tiles(0, n_far, lambda u0, nt: logits_step(ks_ref, u0, nt, None, selb_scr))
    pl.when(i >= 1)(lambda: logits_step(ks_ref, i - 1, 2, (n_win - 2) * TK, selm_scr))
    pl.when(i == 0)(lambda: logits_step(ks_ref, 0, 1, (n_win - 1) * TK, selm_scr))
    m_sel = jnp.max(m8_scr[...], axis=0, keepdims=True)
    over_tiles(0, i + 1, functools.partial(pv_step, vts_ref, m_sel))
    finish(1)

    reset()
    near_diagonal(lambda u0, nt, row0: logits_step(kw_ref, u0, nt, row0, None))
    m_win = jnp.max(m8_scr[...], axis=0, keepdims=True)
    near_diagonal(lambda u0, nt, row0: pv_step(vtw_ref, m_win, u0, nt))
    finish(2)

    for h in range(NSA_HPG):
        cols = slice(h * HEAD_DIM, (h + 1) * HEAD_DIM)
        z = z_ref[:, cols].astype(F32)
        o_ref[:, cols] = (tot_scr[:, h * TQ:(h + 1) * TQ].T * (z * _sigmoid(z))).astype(o_ref.dtype)


def _nsa_attention(proj, kcmp, vtc, vts, vtw, gt, win, cmpb, far, ovt, cols, batch, seq):
    TQ, TK = ATT_TQ, ATT_TK
    n = proj.shape[0]
    ng = NSA_GROUPS
    nq = seq // TQ
    L = NSA_HPG * TQ
    gw = NSA_HPG * HEAD_DIM

    def seq_block(col0):
        return pl.BlockSpec((seq, HEAD_DIM), lambda b, g, i: (b, col0 + g))

    def per_bg(a):
        return pl.BlockSpec((None, None) + a.shape[2:], lambda b, g, i: (b, g) + (0,) * (a.ndim - 2))

    def per_g(a):
        return pl.BlockSpec((None,) + a.shape[1:], lambda b, g, i: (g,) + (0,) * (a.ndim - 1))

    return pl.pallas_call(
        functools.partial(_nsa_attn_kernel, seq=seq),
        grid=(batch, ng, nq),
        in_specs=[pl.BlockSpec((TQ, gw), lambda b, g, i: (b * nq + i, cols["q"] + g)),
                  per_bg(kcmp), per_bg(vtc),
                  seq_block(cols["k_s"]), per_bg(vts),
                  seq_block(cols["k_w"]), per_bg(vtw),
                  pl.BlockSpec((TQ, gw), lambda b, g, i: (b * nq + i, cols["z"] + g)),
                  pl.BlockSpec((gt.shape[0], TQ), lambda b, g, i: (0, b * nq + i)),
                  per_g(win), per_g(cmpb), per_g(far),
                  pl.BlockSpec(ovt.shape, lambda b, g, i: (0, 0))],
        out_specs=pl.BlockSpec((TQ, gw), lambda b, g, i: (b * nq + i, g)),
        out_shape=jax.ShapeDtypeStruct((n, ng * gw), BF16),
        scratch_shapes=[pltpu.VMEM((L, HEAD_DIM), BF16),
                        pltpu.VMEM((seq // CMP_STRIDE, TQ), F32),
                        pltpu.VMEM((seq // SEL_BLOCK, TQ), jnp.int32),
                        pltpu.VMEM((seq // SEL_BLOCK, L), F32),
                        pltpu.VMEM((seq // SEL_BLOCK, L), F32),
                        pltpu.VMEM((seq, L), F32),
                        pltpu.VMEM((8, L), F32),
                        pltpu.VMEM((HEAD_DIM + V_PAD, L), F32),
                        pltpu.VMEM((HEAD_DIM, L), F32)],
        compiler_params=pltpu.CompilerParams(
            dimension_semantics=("arbitrary", "arbitrary", "arbitrary"),
            vmem_limit_bytes=ATT_VMEM_LIMIT),
        name="nsa_attention",
    )(proj, kcmp, vtc, proj, vts, proj, vtw, proj, gt, win, cmpb, far, ovt)


def _bucket_of_distance():
    d = np.arange(MAX_DISTANCE)
    max_exact = NUM_BUCKETS // 2
    large = max_exact + (np.log(np.maximum(d, 1).astype(np.float32) / max_exact)
                         / math.log(MAX_DISTANCE / max_exact) * (NUM_BUCKETS - max_exact)).astype(np.int32)
    large = np.minimum(large, NUM_BUCKETS - 1)
    return np.where(d < max_exact, d, large)


CMP_NEAR = 16


def _bias_builder_kernel(vw_ref, vc_ref, far_ref, win_ref, cmpb_ref, *, nc):
    TQ, TK = ATT_TQ, ATT_TK
    n_win = (WINDOW + TQ) // TK
    for h in range(NSA_HPG):
        lanes = slice(h * TQ, (h + 1) * TQ)
        for w in range(n_win):
            v = vw_ref[h * n_win + w:h * n_win + w + 1, :]
            r = pltpu.roll(jnp.broadcast_to(v, (TK, 2 * TQ)), 0, 1, stride=1, stride_axis=0)
            win_ref[w * TK:(w + 1) * TK, lanes] = r[:, TQ:2 * TQ]
        cmpb_ref[0:nc - CMP_NEAR, lanes] = jnp.broadcast_to(far_ref[:, lanes], (nc - CMP_NEAR, TQ))
        vc = vc_ref[h:h + 1, :]
        rc = pltpu.roll(jnp.broadcast_to(vc, (2 * CMP_NEAR, 4 * TQ)), 0, 1,
                        stride=CMP_STRIDE, stride_axis=0)
        cmpb_ref[nc - CMP_NEAR:nc + CMP_NEAR, lanes] = rc[:, 2 * TQ:3 * TQ]
        cmpb_ref[nc + CMP_NEAR:2 * nc, lanes] = jnp.full((nc - CMP_NEAR, TQ), NEG_INF, F32)


def _bias_tiles(rel_bias, seq):
    TQ, TK = ATT_TQ, ATT_TK
    assert TQ == TK and TQ >= MAX_DISTANCE
    assert CMP_STRIDE * CMP_NEAR >= TQ and 2 * TQ >= CMP_STRIDE * 2 * CMP_NEAR
    assert CMP_STRIDE * (CMP_NEAR + 1) - (CMP_BLOCK - 1) >= MAX_DISTANCE - 1
    nc = seq // CMP_STRIDE
    ng, hpg = NSA_GROUPS, NSA_HPG
    n_win = (WINDOW + TQ) // TK
    tb = (rel_bias.astype(F32) * LOG2E)[_bucket_of_distance()].T.reshape(ng, hpg, MAX_DISTANCE)

    def by_distance(dist, valid):
        vals = tb[:, :, np.clip(dist, 0, MAX_DISTANCE - 1)]
        return jnp.where(valid, vals, NEG_INF)

    x = np.arange(2 * TQ)[None, :]
    dw = WINDOW - TK * np.arange(n_win)[:, None] - TQ + x
    vw = by_distance(dw, (dw >= 0) & (dw < WINDOW)).reshape(ng, hpg * n_win, 2 * TQ)
    xc = np.arange(4 * TQ)
    dc = xc - 2 * TQ + CMP_STRIDE * CMP_NEAR - (CMP_BLOCK - 1)
    vc = by_distance(dc, dc >= 0)
    L = hpg * TQ
    far = jnp.broadcast_to(tb[:, :, MAX_DISTANCE - 1:], (ng, hpg, TQ)).reshape(ng, 1, L)

    win, cmpb = pl.pallas_call(
        functools.partial(_bias_builder_kernel, nc=nc),
        grid=(ng,),
        in_specs=[pl.BlockSpec((None,) + vw.shape[1:], lambda g: (g, 0, 0)),
                  pl.BlockSpec((None,) + vc.shape[1:], lambda g: (g, 0, 0)),
                  pl.BlockSpec((None,) + far.shape[1:], lambda g: (g, 0, 0))],
        out_specs=[pl.BlockSpec((None, WINDOW + TQ, L), lambda g: (g, 0, 0)),
                   pl.BlockSpec((None, 2 * nc, L), lambda g: (g, 0, 0))],
        out_shape=[jax.ShapeDtypeStruct((ng, WINDOW + TQ, L), F32),
                   jax.ShapeDtypeStruct((ng, 2 * nc, L), F32)],
        compiler_params=pltpu.CompilerParams(dimension_semantics=("arbitrary",)),
        name="nsa_bias_tiles",
    )(vw, vc, far)
    return win, cmpb, far


def _overlap_t(seq):
    nc = seq // CMP_STRIDE
    nsel = seq // SEL_BLOCK
    ci = np.arange(nc)[None, :]
    sj = np.arange(nsel)[:, None]
    ov = (CMP_STRIDE * ci < SEL_BLOCK * (sj + 1)) & (CMP_STRIDE * ci + CMP_BLOCK > SEL_BLOCK * sj)
    ov = ov & (ci < nc - 1)
    return jnp.asarray(ov, BF16)


def kernel(x, norm_gains, final_gain, rel_bias, hgrn_lb, hgrn_w_in, hgrn_head_gain, hgrn_w_out,
           nsa_w_in, nsa_pe_k, nsa_pe_v, nsa_phi_k_w1, nsa_phi_k_b1, nsa_phi_k_w2,
           nsa_phi_v_w1, nsa_phi_v_b1, nsa_phi_v_w2, nsa_w_out):
    batch, seq, d = x.shape
    n = batch * seq
    x2d = x.reshape(n, d)
    qscale = HEAD_DIM ** -0.5

    def scale_q_columns(w, scale):
        col = np.ones((1, w.shape[1]), np.float32)
        col[:, :d] = scale
        return (w * col).astype(BF16)

    proj = _norm_proj(x2d, norm_gains[0:1], scale_q_columns(hgrn_w_in[0], qscale))
    o = _hgrn_recurrence(proj, hgrn_lb, hgrn_head_gain[0:1], batch, seq)
    x1, h1 = _out_proj(o, hgrn_w_out[0].astype(BF16), x2d, norm_gains[1:2], final=False)

    kvw = NSA_GROUPS * HEAD_DIM
    w = nsa_w_in[0]
    gate0 = d + 6 * kvw
    gate1 = gate0 + NSA_HEADS * N_BRANCH
    w_main = _nsa_weight_prep(nsa_w_in, d, gate0, gate1, qscale * LOG2E)
    wg = jnp.pad(w[:, gate0:gate1].astype(BF16),
                 ((0, 0), (0, HEAD_DIM - (gate1 - gate0))))
    proj, gt = _proj_gate(h1, w_main, wg)

    blk = {}
    for idx, name in enumerate(("k_c", "v_c", "k_s", "v_s", "k_w", "v_w")):
        blk[name] = (d + idx * kvw) // HEAD_DIM
    gw = NSA_HPG * HEAD_DIM
    cols = dict(blk, q=0, z=gate0 // gw)

    kcmp, vtc, vts, vtw = _kv_prep(
        proj, blk, nsa_pe_k[0], nsa_pe_v[0],
        nsa_phi_k_w1[0].astype(BF16), nsa_phi_v_w1[0].astype(BF16),
        nsa_phi_k_b1[0:1], nsa_phi_v_b1[0:1],
        nsa_phi_k_w2[0].astype(BF16), nsa_phi_v_w2[0].astype(BF16), batch, seq)

    win, cmpb, far = _bias_tiles(rel_bias, seq)
    o = _nsa_attention(proj, kcmp, vtc, vts, vtw, gt, win, cmpb, far, _overlap_t(seq), cols,
                       batch, seq)
    out, = _out_proj(o, nsa_w_out[0].astype(BF16), x1, final_gain[None, :], final=True)
    return out.reshape(batch, seq, d)
```

```python
import functools
import math

import numpy as np
import jax
import jax.numpy as jnp
from jax import lax
from jax.experimental import pallas as pl
from jax.experimental.pallas import tpu as pltpu

F32 = jnp.float32
BF16 = jnp.bfloat16

D_MODEL = 2048
RMS_EPS = 1e-6
HEAD_DIM = 128

HGRN_HEADS = D_MODEL // HEAD_DIM
HGRN_CHUNK = 64
HGRN_SUB = 16
HGRN_SAFE_LOG2 = 96.0

NSA_HEADS = D_MODEL // HEAD_DIM
NSA_GROUPS = 4
NSA_HPG = NSA_HEADS // NSA_GROUPS
N_BRANCH = 3
CMP_BLOCK = 32
CMP_STRIDE = 16
SEL_BLOCK = 64
N_SELECT = 16
WINDOW = 512
NUM_BUCKETS = 32
MAX_DISTANCE = 128
NEG_INF = -1e30
FORCE_SCORE = 1e9

ATT_TQ = 256
ATT_TK = 256
V_PAD = 16
ATT_STEP_TILES = 8
CMP_ROWS = 64
WPREP_TAIL = 64
LOG2E = math.log2(math.e)
V7X_VMEM_BYTES = 64 * 1024 * 1024
VMEM_LIMIT = V7X_VMEM_BYTES * 3 // 4
ATT_VMEM_LIMIT = V7X_VMEM_BYTES * 7 // 8

NT_DIMS = (((1,), (1,)), ((), ()))
TN_DIMS = (((0,), (0,)), ((), ()))


def _sigmoid(x):
    return 1.0 / (1.0 + jnp.exp(-x))


def _split3(a):
    hi = a.astype(BF16)
    r1 = a - hi.astype(F32)
    mid = r1.astype(BF16)
    lo = (r1 - mid.astype(F32)).astype(BF16)
    return hi, mid, lo


def _norm_proj_kernel(x_ref, g_ref, w_ref, o_ref, h_scr):
    @pl.when(pl.program_id(1) == 0)
    def _():
        x = x_ref[...]
        ms = jnp.mean(x * x, axis=-1, keepdims=True)
        h_scr[...] = (x * lax.rsqrt(ms + RMS_EPS) * g_ref[...]).astype(BF16)

    o_ref[...] = jnp.dot(h_scr[...], w_ref[...], preferred_element_type=F32).astype(o_ref.dtype)


def _norm_proj(x, gain, w, *, tm=1024, tn=2048):
    n, d = x.shape
    nout = w.shape[1]
    return pl.pallas_call(
        _norm_proj_kernel,
        grid=(n // tm, nout // tn),
        in_specs=[pl.BlockSpec((tm, d), lambda i, j: (i, 0)),
                  pl.BlockSpec((1, d), lambda i, j: (0, 0)),
                  pl.BlockSpec((d, tn), lambda i, j: (0, j))],
        out_specs=pl.BlockSpec((tm, tn), lambda i, j: (i, j)),
        out_shape=jax.ShapeDtypeStruct((n, nout), BF16),
        scratch_shapes=[pltpu.VMEM((tm, d), BF16)],
        compiler_params=pltpu.CompilerParams(
            dimension_semantics=("arbitrary", "arbitrary"), vmem_limit_bytes=VMEM_LIMIT),
        name="norm_proj",
    )(x, gain, w)


def _proj_gate_kernel(h_ref, wt_ref, wgt_ref, o_ref, gt_ref):
    @pl.when(pl.program_id(1) == 0)
    def _():
        gt_ref[...] = lax.dot_general(wgt_ref[...], h_ref[...], NT_DIMS, preferred_element_type=F32)

    o_ref[...] = lax.dot_general(h_ref[...], wt_ref[...], NT_DIMS,
                                 preferred_element_type=F32).astype(o_ref.dtype)


def _proj_gate(h, wt, wgt, *, tm=1024, tn=1792):
    n, d = h.shape
    nout = wt.shape[0]
    ng = wgt.shape[0]
    return pl.pallas_call(
        _proj_gate_kernel,
        grid=(n // tm, nout // tn),
        in_specs=[pl.BlockSpec((tm, d), lambda i, j: (i, 0)),
                  pl.BlockSpec((tn, d), lambda i, j: (j, 0)),
                  pl.BlockSpec((ng, d), lambda i, j: (0, 0))],
        out_specs=[pl.BlockSpec((tm, tn), lambda i, j: (i, j)),
                   pl.BlockSpec((ng, tm), lambda i, j: (0, i))],
        out_shape=[jax.ShapeDtypeStruct((n, nout), BF16),
                   jax.ShapeDtypeStruct((ng, n), F32)],
        compiler_params=pltpu.CompilerParams(
            dimension_semantics=("arbitrary", "arbitrary"), vmem_limit_bytes=VMEM_LIMIT),
        name="proj_gate",
    )(h, wt, wgt)


def _nsa_weight_prep_kernel(wa_ref, wb_ref, o_ref, g_ref, *, q_blocks, shifted_from, shift, scale):
    i = pl.program_id(0)
    tr, d = o_ref.shape

    @pl.when(i < shifted_from)
    def _():
        o_ref[...] = (wa_ref[...] * jnp.where(i < q_blocks, scale, 1.0)).astype(BF16)

    @pl.when(i >= shifted_from)
    def _():
        both = jnp.concatenate([wa_ref[...], wb_ref[0:WPREP_TAIL, :]], axis=0)
        o_ref[...] = both[shift:shift + tr, :].astype(BF16)

    @pl.when(i == shifted_from)
    def _():
        g_ref[...] = jnp.concatenate(
            [wa_ref[0:shift, :], jnp.zeros((HEAD_DIM - shift, d), F32)], axis=0).astype(BF16)


def _nsa_weight_prep(w3, d, gate0, gate1, scale, *, tr=512):
    wt = jnp.swapaxes(w3, 1, 2)
    n_proj = wt.shape[1]
    shift = gate1 - gate0
    n_out = n_proj - shift
    assert gate0 % tr == 0 and n_out % tr == 0 and d % tr == 0
    assert shift % 16 == 0 and shift <= WPREP_TAIL <= tr
    last_src = (n_proj - 1) // tr
    first_b = gate0 // tr + 1
    return pl.pallas_call(
        functools.partial(_nsa_weight_prep_kernel, q_blocks=d // tr, shifted_from=gate0 // tr,
                          shift=shift, scale=scale),
        grid=(n_out // tr,),
        in_specs=[pl.BlockSpec((None, tr, d), lambda i: (0, i, 0)),
                  pl.BlockSpec((None, tr, d),
                               lambda i: (0, jnp.minimum(jnp.maximum(i + 1, first_b), last_src), 0))],
        out_specs=[pl.BlockSpec((tr, d), lambda i: (i, 0)),
                   pl.BlockSpec((HEAD_DIM, d), lambda i: (0, 0))],
        out_shape=[jax.ShapeDtypeStruct((n_out, d), BF16),
                   jax.ShapeDtypeStruct((HEAD_DIM, d), BF16)],
        compiler_params=pltpu.CompilerParams(
            dimension_semantics=("arbitrary",), vmem_limit_bytes=VMEM_LIMIT),
        name="nsa_weight_prep",
    )(wt, wt)


def _out_proj_kernel(o_ref, w_ref, res_ref, g_ref, *out_refs, final):
    xn = res_ref[...] + jnp.dot(o_ref[...], w_ref[...], preferred_element_type=F32)
    ms = jnp.mean(xn * xn, axis=-1, keepdims=True)
    y = xn * lax.rsqrt(ms + RMS_EPS) * g_ref[...]
    if final:
        out_refs[0][...] = y
    else:
        out_refs[0][...] = xn
        out_refs[1][...] = y.astype(BF16)


def _out_proj(o, w, res, gain, *, final, tm=512):
    n, d = res.shape
    row = pl.BlockSpec((tm, d), lambda i: (i, 0))
    if final:
        out_specs = [row]
        out_shape = [jax.ShapeDtypeStruct((n, d), F32)]
    else:
        out_specs = [row, row]
        out_shape = [jax.ShapeDtypeStruct((n, d), F32), jax.ShapeDtypeStruct((n, d), BF16)]
    return pl.pallas_call(
        functools.partial(_out_proj_kernel, final=final),
        grid=(n // tm,),
        in_specs=[pl.BlockSpec((tm, o.shape[1]), lambda i: (i, 0)),
                  pl.BlockSpec(w.shape, lambda i: (0, 0)),
                  row,
                  pl.BlockSpec((1, d), lambda i: (0, 0))],
        out_specs=out_specs,
        out_shape=out_shape,
        compiler_params=pltpu.CompilerParams(
            dimension_semantics=("arbitrary",), vmem_limit_bytes=VMEM_LIMIT),
        name="out_proj_final" if final else "out_proj",
    )(o, w, res, gain)


def _hgrn_kernel(q_ref, f_ref, v_ref, z_ref, lbp_ref, hg_ref, o_ref,
                 st_scr, k_scr, b_scr, qe_scr, oi_scr, u_scr, d_scr, stb_scr, *, n_chunks):
    C, SUB = HGRN_CHUNK, HGRN_SUB

    @pl.when(pl.program_id(2) == 0)
    def _():
        st_scr[...] = jnp.zeros_like(st_scr)

    lbp = lbp_ref[...]
    e = jnp.exp(lbp - jnp.max(lbp, axis=0, keepdims=True))
    lb = e[0:1, :] / jnp.sum(e, axis=0, keepdims=True)
    hg = hg_ref[...]

    rr = lax.broadcasted_iota(jnp.int32, (C, C), 0)
    cc = lax.broadcasted_iota(jnp.int32, (C, C), 1)
    causal = rr >= cc
    tril = causal.astype(BF16)
    sub_c = lax.broadcasted_iota(jnp.int32, (SUB, C), 1)
    row8 = lax.broadcasted_iota(jnp.int32, (8, C), 0)
    col8 = lax.broadcasted_iota(jnp.int32, (8, C), 1)

    def emit(rows, o):
        z = z_ref[rows, :].astype(F32)
        ms = jnp.mean(o * o, axis=-1, keepdims=True)
        o_ref[rows, :] = (o * lax.rsqrt(ms + RMS_EPS) * hg * (z * _sigmoid(z))).astype(o_ref.dtype)

    chunks = [slice(c * C, (c + 1) * C) for c in range(n_chunks)]
    ks, gsplit = [], []
    for rows in chunks:
        sig = _sigmoid(f_ref[rows, :].astype(F32))
        gsplit.append(_split3(jnp.log(lb + (1.0 - lb) * sig)))
        ks.append((1.0 - lb) * (1.0 - sig))
        k_scr[rows, :] = ks[-1]
    b2s = []
    for rows, (ghi, gmid, glo) in zip(chunks, gsplit):
        b = (jnp.dot(tril, ghi, preferred_element_type=F32)
             + jnp.dot(tril, gmid, preferred_element_type=F32)
             + jnp.dot(tril, glo, preferred_element_type=F32))
        b2s.append(b * LOG2E)
        b_scr[rows, :] = b2s[-1]
    span = jnp.zeros((1, HEAD_DIM), F32)
    for b2 in b2s:
        span = jnp.maximum(span, -b2[C - 1:C])
    safe = jnp.max(span) < HGRN_SAFE_LOG2

    qes, kes = [], []
    for rows, k, b2 in zip(chunks, ks, b2s):
        qes.append((q_ref[rows, :].astype(F32) * jnp.exp2(b2)).astype(BF16))
        kes.append(k * jnp.exp2(-b2))
        qe_scr[rows, :] = qes[-1]
    scores = [lax.dot_general(qe, ke.astype(BF16), NT_DIMS, preferred_element_type=F32)
              for qe, ke in zip(qes, kes)]
    vt = v_ref[...].astype(F32).T.astype(BF16)
    zero = jnp.zeros((C, HEAD_DIM), BF16)
    for c, (rows, sc, ke, b2) in enumerate(zip(chunks, scores, kes, b2s)):
        bl = b2[C - 1:C]
        oi_scr[rows, :] = jnp.dot(jnp.where(causal, sc, 0.0).astype(BF16), v_ref[rows, :],
                                  preferred_element_type=F32)
        kd = (ke * jnp.exp2(bl)).astype(BF16)
        pair = slice(c // 2 * 2 * C, (c // 2 + 1) * 2 * C)
        u_scr[c] = jnp.dot(vt[:, pair], jnp.concatenate([kd, zero] if c % 2 == 0 else [zero, kd]),
                           preferred_element_type=F32)
        d_scr[c:c + 1, :] = jnp.exp2(bl)

    @pl.when(safe)
    def _():
        st = st_scr[...]
        for c in range(n_chunks):
            stb_scr[c] = st.astype(BF16)
            st = st * d_scr[c:c + 1, :] + u_scr[c]
        st_scr[...] = st
        carried = [lax.dot_general(qe_scr[rows, :], stb_scr[c], NT_DIMS, preferred_element_type=F32)
                   for c, rows in enumerate(chunks)]
        for rows, oc in zip(chunks, carried):
            emit(rows, oi_scr[rows, :] + oc)

    def robust_chunk(c):
        rows = slice(c * C, (c + 1) * C)
        q = q_ref[rows, :].astype(F32)
        v = v_ref[rows, :]
        k, b2 = k_scr[rows, :], b_scr[rows, :]
        st = st_scr[...]
        o = lax.dot_general((q * jnp.exp2(b2)).astype(BF16), st.astype(BF16), NT_DIMS,
                            preferred_element_type=F32)
        a_rows = []
        for i in range(C // SUB):
            lo, hi = i * SUB, (i + 1) * SUB
            bq, qq, kk = b2[lo:hi], q[lo:hi], k[lo:hi]
            if i > 0:
                r = b2[lo - 1:lo]
                qi = (qq * jnp.exp2(bq - r)).astype(BF16)
                ki = (k * jnp.exp2(jnp.minimum(r - b2, 0.0))).astype(BF16)
                a = lax.dot_general(qi, ki, NT_DIMS, preferred_element_type=F32)
                a = jnp.where(sub_c < lo, a, 0.0)
            else:
                a = jnp.zeros((SUB, C), F32)
            a8 = [a[j * 8:(j + 1) * 8] for j in range(SUB // 8)]
            for s in range(SUB):
                for j in range(s // 8, SUB // 8):
                    grp = slice(j * 8, (j + 1) * 8)
                    ed = qq[grp] * jnp.exp2(bq[grp] - bq[s:s + 1]) * kk[s:s + 1]
                    col = jnp.sum(ed, axis=-1, keepdims=True)
                    hit = col8 == lo + s
                    if j == s // 8:
                        hit = hit & (row8 >= s - j * 8)
                    a8[j] = jnp.where(hit, col, a8[j])
            a_rows.extend(a8)
        scores = jnp.concatenate(a_rows, axis=0)
        o = o + jnp.dot(scores.astype(BF16), v, preferred_element_type=F32)
        bl = b2[C - 1:C]
        st_scr[...] = st * jnp.exp2(bl) + lax.dot_general(
            v, (k * jnp.exp2(bl - b2)).astype(BF16), TN_DIMS, preferred_element_type=F32)
        emit(rows, o)

    @pl.when(jnp.logical_not(safe))
    def _():
        for c in range(n_chunks):
            robust_chunk(c)


def _hgrn_recurrence(proj, lb_param, head_gain, batch, seq, *, t_blk=1024):
    n = proj.shape[0]
    nh = HGRN_HEADS
    ns = seq // t_blk

    def sec(k):
        return pl.BlockSpec((t_blk, HEAD_DIM), lambda b, h, s, k=k: (b * ns + s, k * nh + h))

    n_chunks = t_blk // HGRN_CHUNK
    return pl.pallas_call(
        functools.partial(_hgrn_kernel, n_chunks=n_chunks),
        grid=(batch, nh, ns),
        in_specs=[sec(0), sec(1), sec(2), sec(3),
                  pl.BlockSpec((lb_param.shape[0], HEAD_DIM), lambda b, h, s: (0, h)),
                  pl.BlockSpec((1, HEAD_DIM), lambda b, h, s: (0, 0))],
        out_specs=pl.BlockSpec((t_blk, HEAD_DIM), lambda b, h, s: (b * ns + s, h)),
        out_shape=jax.ShapeDtypeStruct((n, nh * HEAD_DIM), BF16),
        scratch_shapes=[pltpu.VMEM((HEAD_DIM, HEAD_DIM), F32),
                        pltpu.VMEM((t_blk, HEAD_DIM), F32),
                        pltpu.VMEM((t_blk, HEAD_DIM), F32),
                        pltpu.VMEM((t_blk, HEAD_DIM), BF16),
                        pltpu.VMEM((t_blk, HEAD_DIM), F32),
                        pltpu.VMEM((n_chunks, HEAD_DIM, HEAD_DIM), F32),
                        pltpu.VMEM((n_chunks, HEAD_DIM), F32),
                        pltpu.VMEM((n_chunks, HEAD_DIM, HEAD_DIM), BF16)],
        compiler_params=pltpu.CompilerParams(
            dimension_semantics=("arbitrary", "arbitrary", "arbitrary")),
        name="hgrn_recurrence",
    )(proj, proj, proj, proj, lb_param, head_gain)


def _kv_prep_kernel(kc_ref, vc_ref, vs_ref, vw_ref, pek_ref, pev_ref, w1k_ref, w1v_ref,
                    b1k_ref, b1v_ref, w2k_ref, w2v_ref, ko_ref, vto_ref, vts_ref, vtw_ref, x_scr,
                    *, nc, ntile):
    half = CMP_BLOCK // 2
    TK = ATT_TK

    def phi(src_ref, pe_ref, w1_ref, b1_ref, w2_ref):
        x_scr[...] = src_ref[...].astype(F32)
        top = jnp.zeros((nc, HEAD_DIM), F32)
        bot = jnp.zeros((nc, HEAD_DIM), F32)
        for p in range(CMP_STRIDE):
            xp = x_scr[pl.ds(p, nc, stride=CMP_STRIDE), :]
            xa = (xp + pe_ref[p:p + 1, :]).astype(BF16)
            xb = (xp + pe_ref[half + p:half + p + 1, :]).astype(BF16)
            top = top + jnp.dot(xa, w1_ref[p * HEAD_DIM:(p + 1) * HEAD_DIM, :],
                                preferred_element_type=F32)
            bot = bot + jnp.dot(xb, w1_ref[(half + p) * HEAD_DIM:(half + p + 1) * HEAD_DIM, :],
                                preferred_element_type=F32)
        hid = top + pltpu.roll(bot, nc - 1, 0) + b1_ref[...]
        act = jax.nn.gelu(hid, approximate=True)
        return jnp.dot(act.astype(BF16), w2_ref[...], preferred_element_type=F32)

    ko_ref[...] = phi(kc_ref, pek_ref, w1k_ref, b1k_ref, w2k_ref).astype(BF16)
    vto_ref[...] = phi(vc_ref, pev_ref, w1v_ref, b1v_ref, w2v_ref).T.astype(BF16)

    ones_row = (lax.broadcasted_iota(jnp.int32, (V_PAD, TK), 0) == 0).astype(BF16)

    def tr(u, carry):
        rows = pl.ds(pl.multiple_of(u * TK, TK), TK)
        vts_ref[u] = jnp.concatenate([vs_ref[rows, :].astype(F32).T.astype(BF16), ones_row], axis=0)
        vtw_ref[u] = jnp.concatenate([vw_ref[rows, :].astype(F32).T.astype(BF16), ones_row], axis=0)
        return carry
    lax.fori_loop(0, ntile, tr, 0)


def _kv_prep(proj, blk, pe_k, pe_v, w1k, w1v, b1k, b1v, w2k, w2v, batch, seq):
    g = NSA_GROUPS
    nc = seq // CMP_STRIDE
    ntile = seq // ATT_TK

    def full(a):
        return pl.BlockSpec(a.shape, lambda b, gg: (0,) * a.ndim)

    def seq_block(col0):
        return pl.BlockSpec((seq, HEAD_DIM), lambda b, gg: (b, col0 + gg))

    def out(*shape):
        spec = pl.BlockSpec((None, None) + shape, lambda b, gg: (b, gg) + (0,) * len(shape))
        return spec, jax.ShapeDtypeStruct((batch, g) + shape, BF16)

    specs, shapes = zip(out(nc, HEAD_DIM), out(HEAD_DIM, nc),
                        out(ntile, HEAD_DIM + V_PAD, ATT_TK), out(ntile, HEAD_DIM + V_PAD, ATT_TK))
    return pl.pallas_call(
        functools.partial(_kv_prep_kernel, nc=nc, ntile=ntile),
        grid=(batch, g),
        in_specs=[seq_block(blk["k_c"]), seq_block(blk["v_c"]), seq_block(blk["v_s"]),
                  seq_block(blk["v_w"]),
                  full(pe_k), full(pe_v), full(w1k), full(w1v), full(b1k), full(b1v),
                  full(w2k), full(w2v)],
        out_specs=list(specs),
        out_shape=list(shapes),
        scratch_shapes=[pltpu.VMEM((seq, HEAD_DIM), F32)],
        compiler_params=pltpu.CompilerParams(dimension_semantics=("arbitrary", "arbitrary")),
        name="nsa_kv_prep",
    )(proj, proj, proj, proj, pe_k, pe_v, w1k, w1v, b1k, b1v, w2k, w2v)


def _nsa_attn_kernel(q_ref, kc_ref, vtc_ref, ks_ref, vts_ref, kw_ref, vtw_ref, z_ref, gt_ref,
                     win_ref, cmpb_ref, far_ref, ovt_ref, o_ref,
                     q_scr, psum_scr, rank_scr, selm_scr, selb_scr, s_scr, m8_scr, acc_scr, tot_scr,
                     *, seq):
    TQ, TK = ATT_TQ, ATT_TK
    L = NSA_HPG * TQ
    nc = seq // CMP_STRIDE
    nsel = seq // SEL_BLOCK
    blk_per_tile = TK // SEL_BLOCK
    n_win = (WINDOW + TQ) // TK
    g = pl.program_id(1)
    i = pl.program_id(2)
    t0 = i * TQ

    for h in range(NSA_HPG):
        q_scr[h * TQ:(h + 1) * TQ, :] = q_ref[:, h * HEAD_DIM:(h + 1) * HEAD_DIM]

    def gate_row(c):
        rows = [(g * NSA_HPG + h) * N_BRANCH + c for h in range(NSA_HPG)]
        return jnp.concatenate([_sigmoid(gt_ref[pl.ds(r, 1), :]) for r in rows], axis=1)

    cmp_row0 = pl.multiple_of(nc - i * (TQ // CMP_STRIDE), 8)

    def cmp_branch(rows):
        s = lax.dot_general(kc_ref[0:rows, :], q_scr[...], NT_DIMS, preferred_element_type=F32)
        s = s + cmpb_ref[pl.ds(cmp_row0, rows), :]
        mc = jnp.max(s, axis=0, keepdims=True)
        p = jnp.exp2(s - mc)
        inv = jnp.where(mc > 0.1 * NEG_INF, 1.0 / jnp.sum(p, axis=0, keepdims=True), 0.0)
        p = p * inv
        psum = p[:, 0:TQ]
        for h in range(1, NSA_HPG):
            psum = psum + p[:, h * TQ:(h + 1) * TQ]
        pb = p.astype(BF16)
        if rows < nc:
            psum = jnp.concatenate([psum, jnp.zeros((nc - rows, TQ), F32)], axis=0)
            pb = jnp.concatenate([pb, jnp.zeros((nc - rows, L), BF16)], axis=0)
        psum_scr[...] = psum
        oc = jnp.dot(vtc_ref[...], pb, preferred_element_type=F32)
        tot_scr[...] = oc * gate_row(0)

    qblk_per_rows = CMP_ROWS // (TQ // CMP_STRIDE)
    for k in range(1, nc // CMP_ROWS + 1):
        pl.when(i // qblk_per_rows + 1 == k)(functools.partial(cmp_branch, k * CMP_ROWS))

    ovt = ovt_ref[...]
    phi, pmid, plo = _split3(psum_scr[...])
    imp = (jnp.dot(ovt, phi, preferred_element_type=F32)
           + jnp.dot(ovt, pmid, preferred_element_type=F32)
           + jnp.dot(ovt, plo, preferred_element_type=F32))

    jrow = lax.broadcasted_iota(jnp.int32, (nsel, TQ), 0)
    tcol = t0 + lax.broadcasted_iota(jnp.int32, (nsel, TQ), 1)
    cur = lax.shift_right_arithmetic(tcol, int(math.log2(SEL_BLOCK)))
    forced = (jrow == 0) | (jrow == cur) | (jrow == cur - 1)
    visible = jrow * SEL_BLOCK <= tcol
    score = jnp.where(forced, FORCE_SCORE, jnp.where(visible, imp, NEG_INF))
    row8 = lax.broadcasted_iota(jnp.int32, (8, TQ), 0)
    score8 = [score[r * 8:(r + 1) * 8] for r in range(nsel // 8)]
    rank_scr[...] = jnp.zeros_like(rank_scr)
    n_visible = (i + 1) * (TQ // SEL_BLOCK)

    def count_ahead(grp):
        cnt = [jnp.zeros((8, TQ), jnp.int32) for _ in score8]
        for jp in range(grp * 8, grp * 8 + 8):
            row = score[jp:jp + 1, :]
            for r, sc in enumerate(score8):
                if r * 8 > jp:
                    ahead = row >= sc
                elif r * 8 + 7 < jp:
                    ahead = row > sc
                else:
                    ahead = (row > sc) | ((row == sc) & (row8 > jp - r * 8))
                cnt[r] = cnt[r] + ahead.astype(jnp.int32)
        for r, c in enumerate(cnt):
            rank_scr[r * 8:(r + 1) * 8, :] += c

    for grp in range(nsel // 8):
        pl.when(grp * 8 < n_visible)(functools.partial(count_ahead, grp))
    rank = rank_scr[...]
    selm = jnp.where(rank < min(N_SELECT, nsel), 0.0, NEG_INF)
    selm = jnp.concatenate([selm] * NSA_HPG, axis=1)
    selm_scr[...] = selm
    selb_scr[...] = selm + far_ref[...]

    def step_rows(u0, nt):
        return pl.ds(pl.multiple_of(u0 * TK, TK), nt * TK)

    def logits_step(k_ref, u0, nt, win_row0, sel_scr):
        s = lax.dot_general(k_ref[step_rows(u0, nt), :], q_scr[...], NT_DIMS,
                            preferred_element_type=F32)
        if win_row0 is not None:
            s = s + win_ref[win_row0:win_row0 + nt * TK, :]
        if sel_scr is not None:
            parts = [s[r * SEL_BLOCK:(r + 1) * SEL_BLOCK]
                     + sel_scr[pl.ds(u0 * blk_per_tile + r, 1), :] for r in range(nt * blk_per_tile)]
            s = jnp.concatenate(parts, axis=0)
        s_scr[step_rows(u0, nt), :] = s
        m8_scr[...] = jnp.maximum(m8_scr[...], jnp.max(s.reshape(nt * TK // 8, 8, L), axis=0))

    def pv_step(vt_ref, m, u0, nt):
        p = jnp.exp2((s_scr[step_rows(u0, nt), :] - m).astype(BF16))
        vt = jnp.concatenate([vt_ref[u0 + j] for j in range(nt)], axis=1)
        acc_scr[...] += jnp.dot(vt, p, preferred_element_type=F32)

    def over_tiles(u_lo, n, step):
        big = ATT_STEP_TILES
        nbig = lax.shift_right_logical(n, int(math.log2(big)))

        def body(j, carry):
            step(u_lo + big * j, big)
            return carry
        lax.fori_loop(0, nbig, body, 0)
        u = u_lo + big * nbig
        nt = big // 2
        while nt >= 1:
            pl.when((n & nt) != 0)(functools.partial(step, u, nt))
            u = u + (n & nt)
            nt //= 2

    def near_diagonal(step):
        for nt in range(1, n_win + 1):
            cond = (i == nt - 1) if nt < n_win else (i >= nt - 1)
            pl.when(cond)(functools.partial(step, i - (nt - 1), nt, (n_win - nt) * TK))

    def finish(c):
        w = gate_row(c) * (1.0 / acc_scr[HEAD_DIM:HEAD_DIM + 1, :])
        tot_scr[...] = tot_scr[...] + acc_scr[0:HEAD_DIM, :] * w

    def reset():
        m8_scr[...] = jnp.full_like(m8_scr, NEG_INF)
        acc_scr[...] = jnp.zeros_like(acc_scr)

    reset()
    n_far = jnp.maximum(i - 1, 0)
    over_tiles(0, n_far, lambda u0, nt: logits_step(ks_ref, u0, nt, None, selb_scr))
    pl.when(i >= 1)(lambda: logits_step(ks_ref, i - 1, 2, (n_win - 2) * TK, selm_scr))
    pl.when(i == 0)(lambda: logits_step(ks_ref, 0, 1, (n_win - 1) * TK, selm_scr))
    m_sel = jnp.max(m8_scr[...], axis=0, keepdims=True)
    over_tiles(0, i + 1, functools.partial(pv_step, vts_ref, m_sel))
    finish(1)

    reset()
    near_diagonal(lambda u0, nt, row0: logits_step(kw_ref, u0, nt, row0, None))
    m_win = jnp.max(m8_scr[...], axis=0, keepdims=True)
    near_diagonal(lambda u0, nt, row0: pv_step(vtw_ref, m_win, u0, nt))
    finish(2)

    for h in range(NSA_HPG):
        cols = slice(h * HEAD_DIM, (h + 1) * HEAD_DIM)
        z = z_ref[:, cols].astype(F32)
        o_ref[:, cols] = (tot_scr[:, h * TQ:(h + 1) * TQ].T * (z * _sigmoid(z))).astype(o_ref.dtype)


def _nsa_attention(proj, kcmp, vtc, vts, vtw, gt, win, cmpb, far, ovt, cols, batch, seq):
    TQ, TK = ATT_TQ, ATT_TK
    n = proj.shape[0]
    ng = NSA_GROUPS
    nq = seq // TQ
    L = NSA_HPG * TQ
    gw = NSA_HPG * HEAD_DIM

    def seq_block(col0):
        return pl.BlockSpec((seq, HEAD_DIM), lambda b, g, i: (b, col0 + g))

    def per_bg(a):
        return pl.BlockSpec((None, None) + a.shape[2:], lambda b, g, i: (b, g) + (0,) * (a.ndim - 2))

    def per_g(a):
        return pl.BlockSpec((None,) + a.shape[1:], lambda b, g, i: (g,) + (0,) * (a.ndim - 1))

    return pl.pallas_call(
        functools.partial(_nsa_attn_kernel, seq=seq),
        grid=(batch, ng, nq),
        in_specs=[pl.BlockSpec((TQ, gw), lambda b, g, i: (b * nq + i, cols["q"] + g)),
                  per_bg(kcmp), per_bg(vtc),
                  seq_block(cols["k_s"]), per_bg(vts),
                  seq_block(cols["k_w"]), per_bg(vtw),
                  pl.BlockSpec((TQ, gw), lambda b, g, i: (b * nq + i, cols["z"] + g)),
                  pl.BlockSpec((gt.shape[0], TQ), lambda b, g, i: (0, b * nq + i)),
                  per_g(win), per_g(cmpb), per_g(far),
                  pl.BlockSpec(ovt.shape, lambda b, g, i: (0, 0))],
        out_specs=pl.BlockSpec((TQ, gw), lambda b, g, i: (b * nq + i, g)),
        out_shape=jax.ShapeDtypeStruct((n, ng * gw), BF16),
        scratch_shapes=[pltpu.VMEM((L, HEAD_DIM), BF16),
                        pltpu.VMEM((seq // CMP_STRIDE, TQ), F32),
                        pltpu.VMEM((seq // SEL_BLOCK, TQ), jnp.int32),
                        pltpu.VMEM((seq // SEL_BLOCK, L), F32),
                        pltpu.VMEM((seq // SEL_BLOCK, L), F32),
                        pltpu.VMEM((seq, L), F32),
                        pltpu.VMEM((8, L), F32),
                        pltpu.VMEM((HEAD_DIM + V_PAD, L), F32),
                        pltpu.VMEM((HEAD_DIM, L), F32)],
        compiler_params=pltpu.CompilerParams(
            dimension_semantics=("arbitrary", "arbitrary", "arbitrary"),
            vmem_limit_bytes=ATT_VMEM_LIMIT),
        name="nsa_attention",
    )(proj, kcmp, vtc, proj, vts, proj, vtw, proj, gt, win, cmpb, far, ovt)


def _bucket_of_distance():
    d = np.arange(MAX_DISTANCE)
    max_exact = NUM_BUCKETS // 2
    large = max_exact + (np.log(np.maximum(d, 1).astype(np.float32) / max_exact)
                         / math.log(MAX_DISTANCE / max_exact) * (NUM_BUCKETS - max_exact)).astype(np.int32)
    large = np.minimum(large, NUM_BUCKETS - 1)
    return np.where(d < max_exact, d, large)


CMP_NEAR = 16


def _bias_builder_kernel(vw_ref, vc_ref, far_ref, win_ref, cmpb_ref, *, nc):
    TQ, TK = ATT_TQ, ATT_TK
    n_win = (WINDOW + TQ) // TK
    for h in range(NSA_HPG):
        lanes = slice(h * TQ, (h + 1) * TQ)
        for w in range(n_win):
            v = vw_ref[h * n_win + w:h * n_win + w + 1, :]
            r = pltpu.roll(jnp.broadcast_to(v, (TK, 2 * TQ)), 0, 1, stride=1, stride_axis=0)
            win_ref[w * TK:(w + 1) * TK, lanes] = r[:, TQ:2 * TQ]
        cmpb_ref[0:nc - CMP_NEAR, lanes] = jnp.broadcast_to(far_ref[:, lanes], (nc - CMP_NEAR, TQ))
        vc = vc_ref[h:h + 1, :]
        rc = pltpu.roll(jnp.broadcast_to(vc, (2 * CMP_NEAR, 4 * TQ)), 0, 1,
                        stride=CMP_STRIDE, stride_axis=0)
        cmpb_ref[nc - CMP_NEAR:nc + CMP_NEAR, lanes] = rc[:, 2 * TQ:3 * TQ]
        cmpb_ref[nc + CMP_NEAR:2 * nc, lanes] = jnp.full((nc - CMP_NEAR, TQ), NEG_INF, F32)


def _bias_tiles(rel_bias, seq):
    TQ, TK = ATT_TQ, ATT_TK
    assert TQ == TK and TQ >= MAX_DISTANCE
    assert CMP_STRIDE * CMP_NEAR >= TQ and 2 * TQ >= CMP_STRIDE * 2 * CMP_NEAR
    assert CMP_STRIDE * (CMP_NEAR + 1) - (CMP_BLOCK - 1) >= MAX_DISTANCE - 1
    nc = seq // CMP_STRIDE
    ng, hpg = NSA_GROUPS, NSA_HPG
    n_win = (WINDOW + TQ) // TK
    tb = (rel_bias.astype(F32) * LOG2E)[_bucket_of_distance()].T.reshape(ng, hpg, MAX_DISTANCE)

    def by_distance(dist, valid):
        vals = tb[:, :, np.clip(dist, 0, MAX_DISTANCE - 1)]
        return jnp.where(valid, vals, NEG_INF)

    x = np.arange(2 * TQ)[None, :]
    dw = WINDOW - TK * np.arange(n_win)[:, None] - TQ + x
    vw = by_distance(dw, (dw >= 0) & (dw < WINDOW)).reshape(ng, hpg * n_win, 2 * TQ)
    xc = np.arange(4 * TQ)
    dc = xc - 2 * TQ + CMP_STRIDE * CMP_NEAR - (CMP_BLOCK - 1)
    vc = by_distance(dc, dc >= 0)
    L = hpg * TQ
    far = jnp.broadcast_to(tb[:, :, MAX_DISTANCE - 1:], (ng, hpg, TQ)).reshape(ng, 1, L)

    win, cmpb = pl.pallas_call(
        functools.partial(_bias_builder_kernel, nc=nc),
        grid=(ng,),
        in_specs=[pl.BlockSpec((None,) + vw.shape[1:], lambda g: (g, 0, 0)),
                  pl.BlockSpec((None,) + vc.shape[1:], lambda g: (g, 0, 0)),
                  pl.BlockSpec((None,) + far.shape[1:], lambda g: (g, 0, 0))],
        out_specs=[pl.BlockSpec((None, WINDOW + TQ, L), lambda g: (g, 0, 0)),
                   pl.BlockSpec((None, 2 * nc, L), lambda g: (g, 0, 0))],
        out_shape=[jax.ShapeDtypeStruct((ng, WINDOW + TQ, L), F32),
                   jax.ShapeDtypeStruct((ng, 2 * nc, L), F32)],
        compiler_params=pltpu.CompilerParams(dimension_semantics=("arbitrary",)),
        name="nsa_bias_tiles",
    )(vw, vc, far)
    return win, cmpb, far


def _overlap_t(seq):
    nc = seq // CMP_STRIDE
    nsel = seq // SEL_BLOCK
    ci = np.arange(nc)[None, :]
    sj = np.arange(nsel)[:, None]
    ov = (CMP_STRIDE * ci < SEL_BLOCK * (sj + 1)) & (CMP_STRIDE * ci + CMP_BLOCK > SEL_BLOCK * sj)
    ov = ov & (ci < nc - 1)
    return jnp.asarray(ov, BF16)


def kernel(x, norm_gains, final_gain, rel_bias, hgrn_lb, hgrn_w_in, hgrn_head_gain, hgrn_w_out,
           nsa_w_in, nsa_pe_k, nsa_pe_v, nsa_phi_k_w1, nsa_phi_k_b1, nsa_phi_k_w2,
           nsa_phi_v_w1, nsa_phi_v_b1, nsa_phi_v_w2, nsa_w_out):
    batch, seq, d = x.shape
    n = batch * seq
    x2d = x.reshape(n, d)
    qscale = HEAD_DIM ** -0.5

    def scale_q_columns(w, scale):
        col = np.ones((1, w.shape[1]), np.float32)
        col[:, :d] = scale
        return (w * col).astype(BF16)

    proj = _norm_proj(x2d, norm_gains[0:1], scale_q_columns(hgrn_w_in[0], qscale))
    o = _hgrn_recurrence(proj, hgrn_lb, hgrn_head_gain[0:1], batch, seq)
    x1, h1 = _out_proj(o, hgrn_w_out[0].astype(BF16), x2d, norm_gains[1:2], final=False)

    kvw = NSA_GROUPS * HEAD_DIM
    gate0 = d + 6 * kvw
    gate1 = gate0 + NSA_HEADS * N_BRANCH
    w_main, wg = _nsa_weight_prep(nsa_w_in, d, gate0, gate1, qscale * LOG2E)
    proj, gt = _proj_gate(h1, w_main, wg)

    blk = {}
    for idx, name in enumerate(("k_c", "v_c", "k_s", "v_s", "k_w", "v_w")):
        blk[name] = (d + idx * kvw) // HEAD_DIM
    gw = NSA_HPG * HEAD_DIM
    cols = dict(blk, q=0, z=gate0 // gw)

    kcmp, vtc, vts, vtw = _kv_prep(
        proj, blk, nsa_pe_k[0], nsa_pe_v[0],
        nsa_phi_k_w1[0].astype(BF16), nsa_phi_v_w1[0].astype(BF16),
        nsa_phi_k_b1[0:1], nsa_phi_v_b1[0:1],
        nsa_phi_k_w2[0].astype(BF16), nsa_phi_v_w2[0].astype(BF16), batch, seq)

    win, cmpb, far = _bias_tiles(rel_bias, seq)
    o = _nsa_attention(proj, kcmp, vtc, vts, vtw, gt, win, cmpb, far, _overlap_t(seq), cols,
                       batch, seq)
    out, = _out_proj(o, nsa_w_out[0].astype(BF16), x1, final_gain[None, :], final=True)
    return out.reshape(batch, seq, d)
```

```python
import functools
import math

import numpy as np
import jax
import jax.numpy as jnp
from jax import lax
from jax.experimental import pallas as pl
from jax.experimental.pallas import tpu as pltpu

F32 = jnp.float32
BF16 = jnp.bfloat16

D_MODEL = 2048
RMS_EPS = 1e-6
HEAD_DIM = 128

HGRN_HEADS = D_MODEL // HEAD_DIM
HGRN_CHUNK = 64
HGRN_SUB = 16
HGRN_SAFE_LOG2 = 96.0

NSA_HEADS = D_MODEL // HEAD_DIM
NSA_GROUPS = 4
NSA_HPG = NSA_HEADS // NSA_GROUPS
N_BRANCH = 3
CMP_BLOCK = 32
CMP_STRIDE = 16
SEL_BLOCK = 64
N_SELECT = 16
WINDOW = 512
NUM_BUCKETS = 32
MAX_DISTANCE = 128
NEG_INF = -1e30
FORCE_SCORE = 1e9

ATT_TQ = 256
ATT_TK = 256
V_PAD = 16
ATT_STEP_TILES = 8
CMP_ROWS = 64
WPREP_TAIL = 64
LOG2E = math.log2(math.e)
V7X_VMEM_BYTES = 64 * 1024 * 1024
VMEM_LIMIT = V7X_VMEM_BYTES * 3 // 4
ATT_VMEM_LIMIT = V7X_VMEM_BYTES * 7 // 8

NT_DIMS = (((1,), (1,)), ((), ()))
TN_DIMS = (((0,), (0,)), ((), ()))


def _sigmoid(x):
    return 1.0 / (1.0 + jnp.exp(-x))


def _split3(a):
    hi = a.astype(BF16)
    r1 = a - hi.astype(F32)
    mid = r1.astype(BF16)
    lo = (r1 - mid.astype(F32)).astype(BF16)
    return hi, mid, lo


def _norm_proj_kernel(x_ref, g_ref, w_ref, o_ref, h_scr):
    @pl.when(pl.program_id(1) == 0)
    def _():
        x = x_ref[...]
        ms = jnp.mean(x * x, axis=-1, keepdims=True)
        h_scr[...] = (x * lax.rsqrt(ms + RMS_EPS) * g_ref[...]).astype(BF16)

    o_ref[...] = jnp.dot(h_scr[...], w_ref[...], preferred_element_type=F32).astype(o_ref.dtype)


def _norm_proj(x, gain, w, *, tm=1024, tn=2048):
    n, d = x.shape
    nout = w.shape[1]
    return pl.pallas_call(
        _norm_proj_kernel,
        grid=(n // tm, nout // tn),
        in_specs=[pl.BlockSpec((tm, d), lambda i, j: (i, 0)),
                  pl.BlockSpec((1, d), lambda i, j: (0, 0)),
                  pl.BlockSpec((d, tn), lambda i, j: (0, j))],
        out_specs=pl.BlockSpec((tm, tn), lambda i, j: (i, j)),
        out_shape=jax.ShapeDtypeStruct((n, nout), BF16),
        scratch_shapes=[pltpu.VMEM((tm, d), BF16)],
        compiler_params=pltpu.CompilerParams(
            dimension_semantics=("arbitrary", "arbitrary"), vmem_limit_bytes=VMEM_LIMIT),
        name="norm_proj",
    )(x, gain, w)


def _proj_gate_kernel(h_ref, wt_ref, wgt_ref, o_ref, gt_ref):
    @pl.when(pl.program_id(1) == 0)
    def _():
        gt_ref[...] = lax.dot_general(wgt_ref[...], h_ref[...], NT_DIMS, preferred_element_type=F32)

    o_ref[...] = lax.dot_general(h_ref[...], wt_ref[...], NT_DIMS,
                                 preferred_element_type=F32).astype(o_ref.dtype)


def _proj_gate(h, wt, wgt, *, tm=1024, tn=1792):
    n, d = h.shape
    nout = wt.shape[0]
    ng = wgt.shape[0]
    return pl.pallas_call(
        _proj_gate_kernel,
        grid=(n // tm, nout // tn),
        in_specs=[pl.BlockSpec((tm, d), lambda i, j: (i, 0)),
                  pl.BlockSpec((tn, d), lambda i, j: (j, 0)),
                  pl.BlockSpec((ng, d), lambda i, j: (0, 0))],
        out_specs=[pl.BlockSpec((tm, tn), lambda i, j: (i, j)),
                   pl.BlockSpec((ng, tm), lambda i, j: (0, i))],
        out_shape=[jax.ShapeDtypeStruct((n, nout), BF16),
                   jax.ShapeDtypeStruct((ng, n), F32)],
        compiler_params=pltpu.CompilerParams(
            dimension_semantics=("arbitrary", "arbitrary"), vmem_limit_bytes=VMEM_LIMIT),
        name="proj_gate",
    )(h, wt, wgt)


def _nsa_weight_prep_kernel(wa_ref, wb_ref, o_ref, g_ref, *, q_blocks, shifted_from, shift, scale):
    i = pl.program_id(0)
    tr, d = o_ref.shape

    @pl.when(i < shifted_from)
    def _():
        o_ref[...] = (wa_ref[...] * jnp.where(i < q_blocks, scale, 1.0)).astype(BF16)

    @pl.when(i >= shifted_from)
    def _():
        both = jnp.concatenate([wa_ref[...], wb_ref[0:WPREP_TAIL, :]], axis=0)
        o_ref[...] = both[shift:shift + tr, :].astype(BF16)

    @pl.when(i == shifted_from)
    def _():
        g_ref[...] = jnp.concatenate(
            [wa_ref[0:shift, :], jnp.zeros((HEAD_DIM - shift, d), F32)], axis=0).astype(BF16)


def _nsa_weight_prep(w3, d, gate0, gate1, scale, *, tr=512):
    wt = jnp.swapaxes(w3, 1, 2)
    n_proj = wt.shape[1]
    shift = gate1 - gate0
    n_out = n_proj - shift
    assert gate0 % tr == 0 and n_out % tr == 0 and d % tr == 0
    assert shift % 16 == 0 and shift <= WPREP_TAIL <= tr
    last_src = (n_proj - 1) // tr
    first_b = gate0 // tr + 1
    return pl.pallas_call(
        functools.partial(_nsa_weight_prep_kernel, q_blocks=d // tr, shifted_from=gate0 // tr,
                          shift=shift, scale=scale),
        grid=(n_out // tr,),
        in_specs=[pl.BlockSpec((None, tr, d), lambda i: (0, i, 0)),
                  pl.BlockSpec((None, tr, d),
                               lambda i: (0, jnp.minimum(jnp.maximum(i + 1, first_b), last_src), 0))],
        out_specs=[pl.BlockSpec((tr, d), lambda i: (i, 0)),
                   pl.BlockSpec((HEAD_DIM, d), lambda i: (0, 0))],
        out_shape=[jax.ShapeDtypeStruct((n_out, d), BF16),
                   jax.ShapeDtypeStruct((HEAD_DIM, d), BF16)],
        compiler_params=pltpu.CompilerParams(
            dimension_semantics=("arbitrary",), vmem_limit_bytes=VMEM_LIMIT),
        name="nsa_weight_prep",
    )(wt, wt)


def _out_proj_kernel(o_ref, w_ref, res_ref, g_ref, *out_refs, final):
    xn = res_ref[...] + jnp.dot(o_ref[...], w_ref[...], preferred_element_type=F32)
    ms = jnp.mean(xn * xn, axis=-1, keepdims=True)
    y = xn * lax.rsqrt(ms + RMS_EPS) * g_ref[...]
    if final:
        out_refs[0][...] = y
    else:
        out_refs[0][...] = xn
        out_refs[1][...] = y.astype(BF16)


def _out_proj(o, w, res, gain, *, final, tm=512):
    n, d = res.shape
    row = pl.BlockSpec((tm, d), lambda i: (i, 0))
    if final:
        out_specs = [row]
        out_shape = [jax.ShapeDtypeStruct((n, d), F32)]
    else:
        out_specs = [row, row]
        out_shape = [jax.ShapeDtypeStruct((n, d), F32), jax.ShapeDtypeStruct((n, d), BF16)]
    return pl.pallas_call(
        functools.partial(_out_proj_kernel, final=final),
        grid=(n // tm,),
        in_specs=[pl.BlockSpec((tm, o.shape[1]), lambda i: (i, 0)),
                  pl.BlockSpec(w.shape, lambda i: (0, 0)),
                  row,
                  pl.BlockSpec((1, d), lambda i: (0, 0))],
        out_specs=out_specs,
        out_shape=out_shape,
        compiler_params=pltpu.CompilerParams(
            dimension_semantics=("arbitrary",), vmem_limit_bytes=VMEM_LIMIT),
        name="out_proj_final" if final else "out_proj",
    )(o, w, res, gain)


def _hgrn_kernel(q_ref, f_ref, v_ref, z_ref, lbp_ref, hg_ref, o_ref,
                 st_scr, k_scr, b_scr, qe_scr, oi_scr, u_scr, d_scr, stb_scr, *, n_chunks):
    C, SUB = HGRN_CHUNK, HGRN_SUB

    @pl.when(pl.program_id(2) == 0)
    def _():
        st_scr[...] = jnp.zeros_like(st_scr)

    lbp = lbp_ref[...]
    e = jnp.exp(lbp - jnp.max(lbp, axis=0, keepdims=True))
    lb = e[0:1, :] / jnp.sum(e, axis=0, keepdims=True)
    hg = hg_ref[...]

    rr = lax.broadcasted_iota(jnp.int32, (C, C), 0)
    cc = lax.broadcasted_iota(jnp.int32, (C, C), 1)
    causal = rr >= cc
    tril = causal.astype(BF16)
    sub_c = lax.broadcasted_iota(jnp.int32, (SUB, C), 1)
    row8 = lax.broadcasted_iota(jnp.int32, (8, C), 0)
    col8 = lax.broadcasted_iota(jnp.int32, (8, C), 1)

    def emit(rows, o):
        z = z_ref[rows, :].astype(F32)
        ms = jnp.mean(o * o, axis=-1, keepdims=True)
        o_ref[rows, :] = (o * lax.rsqrt(ms + RMS_EPS) * hg * (z * _sigmoid(z))).astype(o_ref.dtype)

    chunks = [slice(c * C, (c + 1) * C) for c in range(n_chunks)]
    ks, gsplit = [], []
    for rows in chunks:
        sig = _sigmoid(f_ref[rows, :].astype(F32))
        gsplit.append(_split3(jnp.log(lb + (1.0 - lb) * sig)))
        ks.append((1.0 - lb) * (1.0 - sig))
        k_scr[rows, :] = ks[-1]
    b2s = []
    for rows, (ghi, gmid, glo) in zip(chunks, gsplit):
        b = (jnp.dot(tril, ghi, preferred_element_type=F32)
             + jnp.dot(tril, gmid, preferred_element_type=F32)
             + jnp.dot(tril, glo, preferred_element_type=F32))
        b2s.append(b * LOG2E)
        b_scr[rows, :] = b2s[-1]
    span = jnp.zeros((1, HEAD_DIM), F32)
    for b2 in b2s:
        span = jnp.maximum(span, -b2[C - 1:C])
    safe = jnp.max(span) < HGRN_SAFE_LOG2

    qes, kes = [], []
    for rows, k, b2 in zip(chunks, ks, b2s):
        qes.append((q_ref[rows, :].astype(F32) * jnp.exp2(b2)).astype(BF16))
        kes.append(k * jnp.exp2(-b2))
        qe_scr[rows, :] = qes[-1]
    scores = [lax.dot_general(qe, ke.astype(BF16), NT_DIMS, preferred_element_type=F32)
              for qe, ke in zip(qes, kes)]
    vt = v_ref[...].astype(F32).T.astype(BF16)
    zero = jnp.zeros((C, HEAD_DIM), BF16)
    for c, (rows, sc, ke, b2) in enumerate(zip(chunks, scores, kes, b2s)):
        bl = b2[C - 1:C]
        oi_scr[rows, :] = jnp.dot(jnp.where(causal, sc, 0.0).astype(BF16), v_ref[rows, :],
                                  preferred_element_type=F32)
        kd = (ke * jnp.exp2(bl)).astype(BF16)
        pair = slice(c // 2 * 2 * C, (c // 2 + 1) * 2 * C)
        u_scr[c] = jnp.dot(vt[:, pair], jnp.concatenate([kd, zero] if c % 2 == 0 else [zero, kd]),
                           preferred_element_type=F32)
        d_scr[c:c + 1, :] = jnp.exp2(bl)

    @pl.when(safe)
    def _():
        st = st_scr[...]
        for c in range(n_chunks):
            stb_scr[c] = st.astype(BF16)
            st = st * d_scr[c:c + 1, :] + u_scr[c]
        st_scr[...] = st
        carried = [lax.dot_general(qe_scr[rows, :], stb_scr[c], NT_DIMS, preferred_element_type=F32)
                   for c, rows in enumerate(chunks)]
        for rows, oc in zip(chunks, carried):
            emit(rows, oi_scr[rows, :] + oc)

    def robust_chunk(c):
        rows = slice(c * C, (c + 1) * C)
        q = q_ref[rows, :].astype(F32)
        v = v_ref[rows, :]
        k, b2 = k_scr[rows, :], b_scr[rows, :]
        st = st_scr[...]
        o = lax.dot_general((q * jnp.exp2(b2)).astype(BF16), st.astype(BF16), NT_DIMS,
                            preferred_element_type=F32)
        a_rows = []
        for i in range(C // SUB):
            lo, hi = i * SUB, (i + 1) * SUB
            bq, qq, kk = b2[lo:hi], q[lo:hi], k[lo:hi]
            if i > 0:
                r = b2[lo - 1:lo]
                qi = (qq * jnp.exp2(bq - r)).astype(BF16)
                ki = (k * jnp.exp2(jnp.minimum(r - b2, 0.0))).astype(BF16)
                a = lax.dot_general(qi, ki, NT_DIMS, preferred_element_type=F32)
                a = jnp.where(sub_c < lo, a, 0.0)
            else:
                a = jnp.zeros((SUB, C), F32)
            a8 = [a[j * 8:(j + 1) * 8] for j in range(SUB // 8)]
            for s in range(SUB):
                for j in range(s // 8, SUB // 8):
                    grp = slice(j * 8, (j + 1) * 8)
                    ed = qq[grp] * jnp.exp2(bq[grp] - bq[s:s + 1]) * kk[s:s + 1]
                    col = jnp.sum(ed, axis=-1, keepdims=True)
                    hit = col8 == lo + s
                    if j == s // 8:
                        hit = hit & (row8 >= s - j * 8)
                    a8[j] = jnp.where(hit, col, a8[j])
            a_rows.extend(a8)
        scores = jnp.concatenate(a_rows, axis=0)
        o = o + jnp.dot(scores.astype(BF16), v, preferred_element_type=F32)
        bl = b2[C - 1:C]
        st_scr[...] = st * jnp.exp2(bl) + lax.dot_general(
            v, (k * jnp.exp2(bl - b2)).astype(BF16), TN_DIMS, preferred_element_type=F32)
        emit(rows, o)

    @pl.when(jnp.logical_not(safe))
    def _():
        for c in range(n_chunks):
            robust_chunk(c)


def _hgrn_recurrence(proj, lb_param, head_gain, batch, seq, *, t_blk=2048):
    n = proj.shape[0]
    nh = HGRN_HEADS
    t_blk = min(t_blk, seq)
    ns = seq // t_blk

    def sec(k):
        return pl.BlockSpec((t_blk, HEAD_DIM), lambda b, h, s, k=k: (b * ns + s, k * nh + h))

    n_chunks = t_blk // HGRN_CHUNK
    return pl.pallas_call(
        functools.partial(_hgrn_kernel, n_chunks=n_chunks),
        grid=(batch, nh, ns),
        in_specs=[sec(0), sec(1), sec(2), sec(3),
                  pl.BlockSpec((lb_param.shape[0], HEAD_DIM), lambda b, h, s: (0, h)),
                  pl.BlockSpec((1, HEAD_DIM), lambda b, h, s: (0, 0))],
        out_specs=pl.BlockSpec((t_blk, HEAD_DIM), lambda b, h, s: (b * ns + s, h)),
        out_shape=jax.ShapeDtypeStruct((n, nh * HEAD_DIM), BF16),
        scratch_shapes=[pltpu.VMEM((HEAD_DIM, HEAD_DIM), F32),
                        pltpu.VMEM((t_blk, HEAD_DIM), F32),
                        pltpu.VMEM((t_blk, HEAD_DIM), F32),
                        pltpu.VMEM((t_blk, HEAD_DIM), BF16),
                        pltpu.VMEM((t_blk, HEAD_DIM), F32),
                        pltpu.VMEM((n_chunks, HEAD_DIM, HEAD_DIM), F32),
                        pltpu.VMEM((n_chunks, HEAD_DIM), F32),
                        pltpu.VMEM((n_chunks, HEAD_DIM, HEAD_DIM), BF16)],
        compiler_params=pltpu.CompilerParams(
            dimension_semantics=("arbitrary", "arbitrary", "arbitrary")),
        name="hgrn_recurrence",
    )(proj, proj, proj, proj, lb_param, head_gain)


def _kv_prep_kernel(kc_ref, vc_ref, vs_ref, vw_ref, pek_ref, pev_ref, w1k_ref, w1v_ref,
                    b1k_ref, b1v_ref, w2k_ref, w2v_ref, ko_ref, vto_ref, vts_ref, vtw_ref, x_scr,
                    *, nc, ntile):
    half = CMP_BLOCK // 2
    TK = ATT_TK

    def phi(src_ref, pe_ref, w1_ref, b1_ref, w2_ref):
        x_scr[...] = src_ref[...].astype(F32)
        top = jnp.zeros((nc, HEAD_DIM), F32)
        bot = jnp.zeros((nc, HEAD_DIM), F32)
        for p in range(CMP_STRIDE):
            xp = x_scr[pl.ds(p, nc, stride=CMP_STRIDE), :]
            xa = (xp + pe_ref[p:p + 1, :]).astype(BF16)
            xb = (xp + pe_ref[half + p:half + p + 1, :]).astype(BF16)
            top = top + jnp.dot(xa, w1_ref[p * HEAD_DIM:(p + 1) * HEAD_DIM, :],
                                preferred_element_type=F32)
            bot = bot + jnp.dot(xb, w1_ref[(half + p) * HEAD_DIM:(half + p + 1) * HEAD_DIM, :],
                                preferred_element_type=F32)
        hid = top + pltpu.roll(bot, nc - 1, 0) + b1_ref[...]
        act = jax.nn.gelu(hid, approximate=True)
        return jnp.dot(act.astype(BF16), w2_ref[...], preferred_element_type=F32)

    ko_ref[...] = phi(kc_ref, pek_ref, w1k_ref, b1k_ref, w2k_ref).astype(BF16)
    vto_ref[...] = phi(vc_ref, pev_ref, w1v_ref, b1v_ref, w2v_ref).T.astype(BF16)

    ones_row = (lax.broadcasted_iota(jnp.int32, (V_PAD, TK), 0) == 0).astype(BF16)

    def tr(u, carry):
        rows = pl.ds(pl.multiple_of(u * TK, TK), TK)
        vts_ref[u] = jnp.concatenate([vs_ref[rows, :].astype(F32).T.astype(BF16), ones_row], axis=0)
        vtw_ref[u] = jnp.concatenate([vw_ref[rows, :].astype(F32).T.astype(BF16), ones_row], axis=0)
        return carry
    lax.fori_loop(0, ntile, tr, 0)


def _kv_prep(proj, blk, pe_k, pe_v, w1k, w1v, b1k, b1v, w2k, w2v, batch, seq):
    g = NSA_GROUPS
    nc = seq // CMP_STRIDE
    ntile = seq // ATT_TK

    def full(a):
        return pl.BlockSpec(a.shape, lambda b, gg: (0,) * a.ndim)

    def seq_block(col0):
        return pl.BlockSpec((seq, HEAD_DIM), lambda b, gg: (b, col0 + gg))

    def out(*shape):
        spec = pl.BlockSpec((None, None) + shape, lambda b, gg: (b, gg) + (0,) * len(shape))
        return spec, jax.ShapeDtypeStruct((batch, g) + shape, BF16)

    specs, shapes = zip(out(nc, HEAD_DIM), out(HEAD_DIM, nc),
                        out(ntile, HEAD_DIM + V_PAD, ATT_TK), out(ntile, HEAD_DIM + V_PAD, ATT_TK))
    return pl.pallas_call(
        functools.partial(_kv_prep_kernel, nc=nc, ntile=ntile),
        grid=(batch, g),
        in_specs=[seq_block(blk["k_c"]), seq_block(blk["v_c"]), seq_block(blk["v_s"]),
                  seq_block(blk["v_w"]),
                  full(pe_k), full(pe_v), full(w1k), full(w1v), full(b1k), full(b1v),
                  full(w2k), full(w2v)],
        out_specs=list(specs),
        out_shape=list(shapes),
        scratch_shapes=[pltpu.VMEM((seq, HEAD_DIM), F32)],
        compiler_params=pltpu.CompilerParams(dimension_semantics=("arbitrary", "arbitrary")),
        name="nsa_kv_prep",
    )(proj, proj, proj, proj, pe_k, pe_v, w1k, w1v, b1k, b1v, w2k, w2v)


def _nsa_attn_kernel(q_ref, kc_ref, vtc_ref, ks_ref, vts_ref, kw_ref, vtw_ref, z_ref, gt_ref,
                     win_ref, cmpb_ref, far_ref, ovt_ref, o_ref,
                     q_scr, psum_scr, rank_scr, selm_scr, selb_scr, s_scr, m8_scr, acc_scr, tot_scr,
                     *, seq):
    TQ, TK = ATT_TQ, ATT_TK
    L = NSA_HPG * TQ
    nc = seq // CMP_STRIDE
    nsel = seq // SEL_BLOCK
    blk_per_tile = TK // SEL_BLOCK
    n_win = (WINDOW + TQ) // TK
    g = pl.program_id(1)
    i = pl.program_id(2)
    t0 = i * TQ

    for h in range(NSA_HPG):
        q_scr[h * TQ:(h + 1) * TQ, :] = q_ref[:, h * HEAD_DIM:(h + 1) * HEAD_DIM]

    def gate_row(c):
        rows = [(g * NSA_HPG + h) * N_BRANCH + c for h in range(NSA_HPG)]
        return jnp.concatenate([_sigmoid(gt_ref[pl.ds(r, 1), :]) for r in rows], axis=1)

    cmp_row0 = pl.multiple_of(nc - i * (TQ // CMP_STRIDE), 8)

    def cmp_branch(rows):
        s = lax.dot_general(kc_ref[0:rows, :], q_scr[...], NT_DIMS, preferred_element_type=F32)
        s = s + cmpb_ref[pl.ds(cmp_row0, rows), :]
        mc = jnp.max(s, axis=0, keepdims=True)
        p = jnp.exp2(s - mc)
        inv = jnp.where(mc > 0.1 * NEG_INF, 1.0 / jnp.sum(p, axis=0, keepdims=True), 0.0)
        p = p * inv
        psum = p[:, 0:TQ]
        for h in range(1, NSA_HPG):
            psum = psum + p[:, h * TQ:(h + 1) * TQ]
        pb = p.astype(BF16)
        if rows < nc:
            psum = jnp.concatenate([psum, jnp.zeros((nc - rows, TQ), F32)], axis=0)
            pb = jnp.concatenate([pb, jnp.zeros((nc - rows, L), BF16)], axis=0)
        psum_scr[...] = psum
        oc = jnp.dot(vtc_ref[...], pb, preferred_element_type=F32)
        tot_scr[...] = oc * gate_row(0)

    qblk_per_rows = CMP_ROWS // (TQ // CMP_STRIDE)
    for k in range(1, nc // CMP_ROWS + 1):
        pl.when(i // qblk_per_rows + 1 == k)(functools.partial(cmp_branch, k * CMP_ROWS))

    ovt = ovt_ref[...]
    phi, pmid, plo = _split3(psum_scr[...])
    imp = (jnp.dot(ovt, phi, preferred_element_type=F32)
           + jnp.dot(ovt, pmid, preferred_element_type=F32)
           + jnp.dot(ovt, plo, preferred_element_type=F32))

    jrow = lax.broadcasted_iota(jnp.int32, (nsel, TQ), 0)
    tcol = t0 + lax.broadcasted_iota(jnp.int32, (nsel, TQ), 1)
    cur = lax.shift_right_arithmetic(tcol, int(math.log2(SEL_BLOCK)))
    forced = (jrow == 0) | (jrow == cur) | (jrow == cur - 1)
    visible = jrow * SEL_BLOCK <= tcol
    score = jnp.where(forced, FORCE_SCORE, jnp.where(visible, imp, NEG_INF))
    row8 = lax.broadcasted_iota(jnp.int32, (8, TQ), 0)
    score8 = [score[r * 8:(r + 1) * 8] for r in range(nsel // 8)]
    rank_scr[...] = jnp.zeros_like(rank_scr)
    n_visible = (i + 1) * (TQ // SEL_BLOCK)

    def count_ahead(grp):
        cnt = [jnp.zeros((8, TQ), jnp.int32) for _ in score8]
        for jp in range(grp * 8, grp * 8 + 8):
            row = score[jp:jp + 1, :]
            for r, sc in enumerate(score8):
                if r * 8 > jp:
                    ahead = row >= sc
                elif r * 8 + 7 < jp:
                    ahead = row > sc
                else:
                    ahead = (row > sc) | ((row == sc) & (row8 > jp - r * 8))
                cnt[r] = cnt[r] + ahead.astype(jnp.int32)
        for r, c in enumerate(cnt):
            rank_scr[r * 8:(r + 1) * 8, :] += c

    for grp in range(nsel // 8):
        pl.when(grp * 8 < n_visible)(functools.partial(count_ahead, grp))
    rank = rank_scr[...]
    selm = jnp.where(rank < min(N_SELECT, nsel), 0.0, NEG_INF)
    selm = jnp.concatenate([selm] * NSA_HPG, axis=1)
    selm_scr[...] = selm
    selb_scr[...] = selm + far_ref[...]

    def step_rows(u0, nt):
        return pl.ds(pl.multiple_of(u0 * TK, TK), nt * TK)

    def logits_step(k_ref, u0, nt, win_row0, sel_scr):
        s = lax.dot_general(k_ref[step_rows(u0, nt), :], q_scr[...], NT_DIMS,
                            preferred_element_type=F32)
        if win_row0 is not None:
            s = s + win_ref[win_row0:win_row0 + nt * TK, :]
        if sel_scr is not None:
            parts = [s[r * SEL_BLOCK:(r + 1) * SEL_BLOCK]
                     + sel_scr[pl.ds(u0 * blk_per_tile + r, 1), :] for r in range(nt * blk_per_tile)]
            s = jnp.concatenate(parts, axis=0)
        s_scr[step_rows(u0, nt), :] = s
        m8_scr[...] = jnp.maximum(m8_scr[...], jnp.max(s.reshape(nt * TK // 8, 8, L), axis=0))

    def pv_step(vt_ref, m, u0, nt):
        p = jnp.exp2((s_scr[step_rows(u0, nt), :] - m).astype(BF16))
        vt = jnp.concatenate([vt_ref[u0 + j] for j in range(nt)], axis=1)
        acc_scr[...] += jnp.dot(vt, p, preferred_element_type=F32)

    def over_tiles(u_lo, n, step):
        big = ATT_STEP_TILES
        nbig = lax.shift_right_logical(n, int(math.log2(big)))

        def body(j, carry):
            step(u_lo + big * j, big)
            return carry
        lax.fori_loop(0, nbig, body, 0)
        u = u_lo + big * nbig
        nt = big // 2
        while nt >= 1:
            pl.when((n & nt) != 0)(functools.partial(step, u, nt))
            u = u + (n & nt)
            nt //= 2

    def near_diagonal(step):
        for nt in range(1, n_win + 1):
            cond = (i == nt - 1) if nt < n_win else (i >= nt - 1)
            pl.when(cond)(functools.partial(step, i - (nt - 1), nt, (n_win - nt) * TK))

    def finish(c):
        w = gate_row(c) * (1.0 / acc_scr[HEAD_DIM:HEAD_DIM + 1, :])
        tot_scr[...] = tot_scr[...] + acc_scr[0:HEAD_DIM, :] * w

    def reset():
        m8_scr[...] = jnp.full_like(m8_scr, NEG_INF)
        acc_scr[...] = jnp.zeros_like(acc_scr)

    reset()
    n_far = jnp.maximum(i - 1, 0)
    over_tiles(0, n_far, lambda u0, nt: logits_step(ks_ref, u0, nt, None, selb_scr))
    pl.when(i >= 1)(lambda: logits_step(ks_ref, i - 1, 2, (n_win - 2) * TK, selm_scr))
    pl.when(i == 0)(lambda: logits_step(ks_ref, 0, 1, (n_win - 1) * TK, selm_scr))
    m_sel = jnp.max(m8_scr[...], axis=0, keepdims=True)
    over_tiles(0, i + 1, functools.partial(pv_step, vts_ref, m_sel))
    finish(1)

    reset()
    near_diagonal(lambda u0, nt, row0: logits_step(kw_ref, u0, nt, row0, None))
    m_win = jnp.max(m8_scr[...], axis=0, keepdims=True)
    near_diagonal(lambda u0, nt, row0: pv_step(vtw_ref, m_win, u0, nt))
    finish(2)

    for h in range(NSA_HPG):
        cols = slice(h * HEAD_DIM, (h + 1) * HEAD_DIM)
        z = z_ref[:, cols].astype(F32)
        o_ref[:, cols] = (tot_scr[:, h * TQ:(h + 1) * TQ].T * (z * _sigmoid(z))).astype(o_ref.dtype)


def _nsa_attention(proj, kcmp, vtc, vts, vtw, gt, win, cmpb, far, ovt, cols, batch, seq):
    TQ, TK = ATT_TQ, ATT_TK
    n = proj.shape[0]
    ng = NSA_GROUPS
    nq = seq // TQ
    L = NSA_HPG * TQ
    gw = NSA_HPG * HEAD_DIM

    def seq_block(col0):
        return pl.BlockSpec((seq, HEAD_DIM), lambda b, g, i: (b, col0 + g))

    def per_bg(a):
        return pl.BlockSpec((None, None) + a.shape[2:], lambda b, g, i: (b, g) + (0,) * (a.ndim - 2))

    def per_g(a):
        return pl.BlockSpec((None,) + a.shape[1:], lambda b, g, i: (g,) + (0,) * (a.ndim - 1))

    return pl.pallas_call(
        functools.partial(_nsa_attn_kernel, seq=seq),
        grid=(batch, ng, nq),
        in_specs=[pl.BlockSpec((TQ, gw), lambda b, g, i: (b * nq + i, cols["q"] + g)),
                  per_bg(kcmp), per_bg(vtc),
                  seq_block(cols["k_s"]), per_bg(vts),
                  seq_block(cols["k_w"]), per_bg(vtw),
                  pl.BlockSpec((TQ, gw), lambda b, g, i: (b * nq + i, cols["z"] + g)),
                  pl.BlockSpec((gt.shape[0], TQ), lambda b, g, i: (0, b * nq + i)),
                  per_g(win), per_g(cmpb), per_g(far),
                  pl.BlockSpec(ovt.shape, lambda b, g, i: (0, 0))],
        out_specs=pl.BlockSpec((TQ, gw), lambda b, g, i: (b * nq + i, g)),
        out_shape=jax.ShapeDtypeStruct((n, ng * gw), BF16),
        scratch_shapes=[pltpu.VMEM((L, HEAD_DIM), BF16),
                        pltpu.VMEM((seq // CMP_STRIDE, TQ), F32),
                        pltpu.VMEM((seq // SEL_BLOCK, TQ), jnp.int32),
                        pltpu.VMEM((seq // SEL_BLOCK, L), F32),
                        pltpu.VMEM((seq // SEL_BLOCK, L), F32),
                        pltpu.VMEM((seq, L), F32),
                        pltpu.VMEM((8, L), F32),
                        pltpu.VMEM((HEAD_DIM + V_PAD, L), F32),
                        pltpu.VMEM((HEAD_DIM, L), F32)],
        compiler_params=pltpu.CompilerParams(
            dimension_semantics=("arbitrary", "arbitrary", "arbitrary"),
            vmem_limit_bytes=ATT_VMEM_LIMIT),
        name="nsa_attention",
    )(proj, kcmp, vtc, proj, vts, proj, vtw, proj, gt, win, cmpb, far, ovt)


def _bucket_of_distance():
    d = np.arange(MAX_DISTANCE)
    max_exact = NUM_BUCKETS // 2
    large = max_exact + (np.log(np.maximum(d, 1).astype(np.float32) / max_exact)
                         / math.log(MAX_DISTANCE / max_exact) * (NUM_BUCKETS - max_exact)).astype(np.int32)
    large = np.minimum(large, NUM_BUCKETS - 1)
    return np.where(d < max_exact, d, large)


CMP_NEAR = 16


def _bias_builder_kernel(vw_ref, vc_ref, far_ref, win_ref, cmpb_ref, *, nc):
    TQ, TK = ATT_TQ, ATT_TK
    n_win = (WINDOW + TQ) // TK
    for h in range(NSA_HPG):
        lanes = slice(h * TQ, (h + 1) * TQ)
        for w in range(n_win):
            v = vw_ref[h * n_win + w:h * n_win + w + 1, :]
            r = pltpu.roll(jnp.broadcast_to(v, (TK, 2 * TQ)), 0, 1, stride=1, stride_axis=0)
            win_ref[w * TK:(w + 1) * TK, lanes] = r[:, TQ:2 * TQ]
        cmpb_ref[0:nc - CMP_NEAR, lanes] = jnp.broadcast_to(far_ref[:, lanes], (nc - CMP_NEAR, TQ))
        vc = vc_ref[h:h + 1, :]
        rc = pltpu.roll(jnp.broadcast_to(vc, (2 * CMP_NEAR, 4 * TQ)), 0, 1,
                        stride=CMP_STRIDE, stride_axis=0)
        cmpb_ref[nc - CMP_NEAR:nc + CMP_NEAR, lanes] = rc[:, 2 * TQ:3 * TQ]
        cmpb_ref[nc + CMP_NEAR:2 * nc, lanes] = jnp.full((nc - CMP_NEAR, TQ), NEG_INF, F32)


def _bias_tiles(rel_bias, seq):
    TQ, TK = ATT_TQ, ATT_TK
    assert TQ == TK and TQ >= MAX_DISTANCE
    assert CMP_STRIDE * CMP_NEAR >= TQ and 2 * TQ >= CMP_STRIDE * 2 * CMP_NEAR
    assert CMP_STRIDE * (CMP_NEAR + 1) - (CMP_BLOCK - 1) >= MAX_DISTANCE - 1
    nc = seq // CMP_STRIDE
    ng, hpg = NSA_GROUPS, NSA_HPG
    n_win = (WINDOW + TQ) // TK
    tb = (rel_bias.astype(F32) * LOG2E)[_bucket_of_distance()].T.reshape(ng, hpg, MAX_DISTANCE)

    def by_distance(dist, valid):
        vals = tb[:, :, np.clip(dist, 0, MAX_DISTANCE - 1)]
        return jnp.where(valid, vals, NEG_INF)

    x = np.arange(2 * TQ)[None, :]
    dw = WINDOW - TK * np.arange(n_win)[:, None] - TQ + x
    vw = by_distance(dw, (dw >= 0) & (dw < WINDOW)).reshape(ng, hpg * n_win, 2 * TQ)
    xc = np.arange(4 * TQ)
    dc = xc - 2 * TQ + CMP_STRIDE * CMP_NEAR - (CMP_BLOCK - 1)
    vc = by_distance(dc, dc >= 0)
    L = hpg * TQ
    far = jnp.broadcast_to(tb[:, :, MAX_DISTANCE - 1:], (ng, hpg, TQ)).reshape(ng, 1, L)

    win, cmpb = pl.pallas_call(
        functools.partial(_bias_builder_kernel, nc=nc),
        grid=(ng,),
        in_specs=[pl.BlockSpec((None,) + vw.shape[1:], lambda g: (g, 0, 0)),
                  pl.BlockSpec((None,) + vc.shape[1:], lambda g: (g, 0, 0)),
                  pl.BlockSpec((None,) + far.shape[1:], lambda g: (g, 0, 0))],
        out_specs=[pl.BlockSpec((None, WINDOW + TQ, L), lambda g: (g, 0, 0)),
                   pl.BlockSpec((None, 2 * nc, L), lambda g: (g, 0, 0))],
        out_shape=[jax.ShapeDtypeStruct((ng, WINDOW + TQ, L), F32),
                   jax.ShapeDtypeStruct((ng, 2 * nc, L), F32)],
        compiler_params=pltpu.CompilerParams(dimension_semantics=("arbitrary",)),
        name="nsa_bias_tiles",
    )(vw, vc, far)
    return win, cmpb, far


def _overlap_t(seq):
    nc = seq // CMP_STRIDE
    nsel = seq // SEL_BLOCK
    ci = np.arange(nc)[None, :]
    sj = np.arange(nsel)[:, None]
    ov = (CMP_STRIDE * ci < SEL_BLOCK * (sj + 1)) & (CMP_STRIDE * ci + CMP_BLOCK > SEL_BLOCK * sj)
    ov = ov & (ci < nc - 1)
    return jnp.asarray(ov, BF16)


def kernel(x, norm_gains, final_gain, rel_bias, hgrn_lb, hgrn_w_in, hgrn_head_gain, hgrn_w_out,
           nsa_w_in, nsa_pe_k, nsa_pe_v, nsa_phi_k_w1, nsa_phi_k_b1, nsa_phi_k_w2,
           nsa_phi_v_w1, nsa_phi_v_b1, nsa_phi_v_w2, nsa_w_out):
    batch, seq, d = x.shape
    n = batch * seq
    x2d = x.reshape(n, d)
    qscale = HEAD_DIM ** -0.5

    def scale_q_columns(w, scale):
        col = np.ones((1, w.shape[1]), np.float32)
        col[:, :d] = scale
        return (w * col).astype(BF16)

    proj = _norm_proj(x2d, norm_gains[0:1], scale_q_columns(hgrn_w_in[0], qscale))
    o = _hgrn_recurrence(proj, hgrn_lb, hgrn_head_gain[0:1], batch, seq)
    x1, h1 = _out_proj(o, hgrn_w_out[0].astype(BF16), x2d, norm_gains[1:2], final=False)

    kvw = NSA_GROUPS * HEAD_DIM
    gate0 = d + 6 * kvw
    gate1 = gate0 + NSA_HEADS * N_BRANCH
    w_main, wg = _nsa_weight_prep(nsa_w_in, d, gate0, gate1, qscale * LOG2E)
    proj, gt = _proj_gate(h1, w_main, wg)

    blk = {}
    for idx, name in enumerate(("k_c", "v_c", "k_s", "v_s", "k_w", "v_w")):
        blk[name] = (d + idx * kvw) // HEAD_DIM
    gw = NSA_HPG * HEAD_DIM
    cols = dict(blk, q=0, z=gate0 // gw)

    kcmp, vtc, vts, vtw = _kv_prep(
        proj, blk, nsa_pe_k[0], nsa_pe_v[0],
        nsa_phi_k_w1[0].astype(BF16), nsa_phi_v_w1[0].astype(BF16),
        nsa_phi_k_b1[0:1], nsa_phi_v_b1[0:1],
        nsa_phi_k_w2[0].astype(BF16), nsa_phi_v_w2[0].astype(BF16), batch, seq)

    win, cmpb, far = _bias_tiles(rel_bias, seq)
    o = _nsa_attention(proj, kcmp, vtc, vts, vtw, gt, win, cmpb, far, _overlap_t(seq), cols,
                       batch, seq)
    out, = _out_proj(o, nsa_w_out[0].astype(BF16), x1, final_gain[None, :], final=True)
    return out.reshape(batch, seq, d)
```

```python
import functools
import math

import numpy as np
import jax
import jax.numpy as jnp
from jax import lax
from jax.experimental import pallas as pl
from jax.experimental.pallas import tpu as pltpu

F32 = jnp.float32
BF16 = jnp.bfloat16

D_MODEL = 2048
RMS_EPS = 1e-6
HEAD_DIM = 128

HGRN_HEADS = D_MODEL // HEAD_DIM
HGRN_CHUNK = 64
HGRN_SUB = 16
HGRN_SAFE_LOG2 = 96.0

NSA_HEADS = D_MODEL // HEAD_DIM
NSA_GROUPS = 4
NSA_HPG = NSA_HEADS // NSA_GROUPS
N_BRANCH = 3
CMP_BLOCK = 32
CMP_STRIDE = 16
SEL_BLOCK = 64
N_SELECT = 16
WINDOW = 512
NUM_BUCKETS = 32
MAX_DISTANCE = 128
NEG_INF = -1e30
FORCE_SCORE = 1e9

ATT_TQ = 256
ATT_TK = 256
V_PAD = 16
ATT_STEP_TILES = 8
CMP_ROWS = 64
WPREP_TAIL = 64
LOG2E = math.log2(math.e)
V7X_VMEM_BYTES = 64 * 1024 * 1024
VMEM_LIMIT = V7X_VMEM_BYTES * 3 // 4
ATT_VMEM_LIMIT = V7X_VMEM_BYTES * 7 // 8

NT_DIMS = (((1,), (1,)), ((), ()))
TN_DIMS = (((0,), (0,)), ((), ()))


def _sigmoid(x):
    return 1.0 / (1.0 + jnp.exp(-x))


def _split3(a):
    hi = a.astype(BF16)
    r1 = a - hi.astype(F32)
    mid = r1.astype(BF16)
    lo = (r1 - mid.astype(F32)).astype(BF16)
    return hi, mid, lo


def _norm_proj_kernel(x_ref, g_ref, w_ref, o_ref, h_scr):
    @pl.when(pl.program_id(1) == 0)
    def _():
        x = x_ref[...]
        ms = jnp.mean(x * x, axis=-1, keepdims=True)
        h_scr[...] = (x * lax.rsqrt(ms + RMS_EPS) * g_ref[...]).astype(BF16)

    o_ref[...] = jnp.dot(h_scr[...], w_ref[...], preferred_element_type=F32).astype(o_ref.dtype)


def _norm_proj(x, gain, w, *, tm=1024, tn=2048):
    n, d = x.shape
    nout = w.shape[1]
    return pl.pallas_call(
        _norm_proj_kernel,
        grid=(n // tm, nout // tn),
        in_specs=[pl.BlockSpec((tm, d), lambda i, j: (i, 0)),
                  pl.BlockSpec((1, d), lambda i, j: (0, 0)),
                  pl.BlockSpec((d, tn), lambda i, j: (0, j))],
        out_specs=pl.BlockSpec((tm, tn), lambda i, j: (i, j)),
        out_shape=jax.ShapeDtypeStruct((n, nout), BF16),
        scratch_shapes=[pltpu.VMEM((tm, d), BF16)],
        compiler_params=pltpu.CompilerParams(
            dimension_semantics=("arbitrary", "arbitrary"), vmem_limit_bytes=VMEM_LIMIT),
        name="norm_proj",
    )(x, gain, w)


def _proj_gate_kernel(h_ref, wt_ref, wgt_ref, o_ref, gt_ref):
    @pl.when(pl.program_id(1) == 0)
    def _():
        gt_ref[...] = lax.dot_general(wgt_ref[...], h_ref[...], NT_DIMS, preferred_element_type=F32)

    o_ref[...] = lax.dot_general(h_ref[...], wt_ref[...], NT_DIMS,
                                 preferred_element_type=F32).astype(o_ref.dtype)


def _proj_gate(h, wt, wgt, *, tm=1024, tn=1792):
    n, d = h.shape
    nout = wt.shape[0]
    ng = wgt.shape[0]
    return pl.pallas_call(
        _proj_gate_kernel,
        grid=(n // tm, nout // tn),
        in_specs=[pl.BlockSpec((tm, d), lambda i, j: (i, 0)),
                  pl.BlockSpec((tn, d), lambda i, j: (j, 0)),
                  pl.BlockSpec((ng, d), lambda i, j: (0, 0))],
        out_specs=[pl.BlockSpec((tm, tn), lambda i, j: (i, j)),
                   pl.BlockSpec((ng, tm), lambda i, j: (0, i))],
        out_shape=[jax.ShapeDtypeStruct((n, nout), BF16),
                   jax.ShapeDtypeStruct((ng, n), F32)],
        compiler_params=pltpu.CompilerParams(
            dimension_semantics=("arbitrary", "arbitrary"), vmem_limit_bytes=VMEM_LIMIT),
        name="proj_gate",
    )(h, wt, wgt)


def _nsa_weight_prep_kernel(wa_ref, wb_ref, o_ref, g_ref, *, q_blocks, shifted_from, shift, scale):
    i = pl.program_id(0)
    tr, d = o_ref.shape

    @pl.when(i < shifted_from)
    def _():
        o_ref[...] = (wa_ref[...] * jnp.where(i < q_blocks, scale, 1.0)).astype(BF16)

    @pl.when(i >= shifted_from)
    def _():
        both = jnp.concatenate([wa_ref[...], wb_ref[0:WPREP_TAIL, :]], axis=0)
        o_ref[...] = both[shift:shift + tr, :].astype(BF16)

    @pl.when(i == shifted_from)
    def _():
        g_ref[...] = jnp.concatenate(
            [wa_ref[0:shift, :], jnp.zeros((HEAD_DIM - shift, d), F32)], axis=0).astype(BF16)


def _nsa_weight_prep(w3, d, gate0, gate1, scale, *, tr=512):
    wt = jnp.swapaxes(w3, 1, 2)
    n_proj = wt.shape[1]
    shift = gate1 - gate0
    n_out = n_proj - shift
    assert gate0 % tr == 0 and n_out % tr == 0 and d % tr == 0
    assert shift % 16 == 0 and shift <= WPREP_TAIL <= tr
    last_src = (n_proj - 1) // tr
    first_b = gate0 // tr + 1
    return pl.pallas_call(
        functools.partial(_nsa_weight_prep_kernel, q_blocks=d // tr, shifted_from=gate0 // tr,
                          shift=shift, scale=scale),
        grid=(n_out // tr,),
        in_specs=[pl.BlockSpec((None, tr, d), lambda i: (0, i, 0)),
                  pl.BlockSpec((None, tr, d),
                               lambda i: (0, jnp.minimum(jnp.maximum(i + 1, first_b), last_src), 0))],
        out_specs=[pl.BlockSpec((tr, d), lambda i: (i, 0)),
                   pl.BlockSpec((HEAD_DIM, d), lambda i: (0, 0))],
        out_shape=[jax.ShapeDtypeStruct((n_out, d), BF16),
                   jax.ShapeDtypeStruct((HEAD_DIM, d), BF16)],
        compiler_params=pltpu.CompilerParams(
            dimension_semantics=("arbitrary",), vmem_limit_bytes=VMEM_LIMIT),
        name="nsa_weight_prep",
    )(wt, wt)


def _out_proj_kernel(o_ref, w_ref, res_ref, g_ref, *out_refs, final):
    xn = res_ref[...] + jnp.dot(o_ref[...], w_ref[...], preferred_element_type=F32)
    ms = jnp.mean(xn * xn, axis=-1, keepdims=True)
    y = xn * lax.rsqrt(ms + RMS_EPS) * g_ref[...]
    if final:
        out_refs[0][...] = y
    else:
        out_refs[0][...] = xn
        out_refs[1][...] = y.astype(BF16)


def _out_proj(o, w, res, gain, *, final, tm=512):
    n, d = res.shape
    row = pl.BlockSpec((tm, d), lambda i: (i, 0))
    if final:
        out_specs = [row]
        out_shape = [jax.ShapeDtypeStruct((n, d), F32)]
    else:
        out_specs = [row, row]
        out_shape = [jax.ShapeDtypeStruct((n, d), F32), jax.ShapeDtypeStruct((n, d), BF16)]
    return pl.pallas_call(
        functools.partial(_out_proj_kernel, final=final),
        grid=(n // tm,),
        in_specs=[pl.BlockSpec((tm, o.shape[1]), lambda i: (i, 0)),
                  pl.BlockSpec(w.shape, lambda i: (0, 0)),
                  row,
                  pl.BlockSpec((1, d), lambda i: (0, 0))],
        out_specs=out_specs,
        out_shape=out_shape,
        compiler_params=pltpu.CompilerParams(
            dimension_semantics=("arbitrary",), vmem_limit_bytes=VMEM_LIMIT),
        name="out_proj_final" if final else "out_proj",
    )(o, w, res, gain)


def _hgrn_kernel(q_ref, f_ref, v_ref, z_ref, lbp_ref, hg_ref, o_ref,
                 st_scr, k_scr, b_scr, qe_scr, oi_scr, u_scr, d_scr, stb_scr, *, n_chunks):
    C, SUB = HGRN_CHUNK, HGRN_SUB

    @pl.when(pl.program_id(2) == 0)
    def _():
        st_scr[...] = jnp.zeros_like(st_scr)

    lbp = lbp_ref[...]
    e = jnp.exp(lbp - jnp.max(lbp, axis=0, keepdims=True))
    lb = e[0:1, :] / jnp.sum(e, axis=0, keepdims=True)
    hg = hg_ref[...]

    rr = lax.broadcasted_iota(jnp.int32, (C, C), 0)
    cc = lax.broadcasted_iota(jnp.int32, (C, C), 1)
    causal = rr >= cc
    tril = causal.astype(BF16)
    sub_c = lax.broadcasted_iota(jnp.int32, (SUB, C), 1)
    row8 = lax.broadcasted_iota(jnp.int32, (8, C), 0)
    col8 = lax.broadcasted_iota(jnp.int32, (8, C), 1)

    def emit(rows, o):
        z = z_ref[rows, :].astype(F32)
        ms = jnp.mean(o * o, axis=-1, keepdims=True)
        o_ref[rows, :] = (o * lax.rsqrt(ms + RMS_EPS) * hg * (z * _sigmoid(z))).astype(o_ref.dtype)

    chunks = [slice(c * C, (c + 1) * C) for c in range(n_chunks)]
    ks, gsplit = [], []
    for rows in chunks:
        sig = _sigmoid(f_ref[rows, :].astype(F32))
        gsplit.append(_split3(jnp.log(lb + (1.0 - lb) * sig)))
        ks.append((1.0 - lb) * (1.0 - sig))
        k_scr[rows, :] = ks[-1]
    b2s = []
    for rows, (ghi, gmid, glo) in zip(chunks, gsplit):
        b = (jnp.dot(tril, ghi, preferred_element_type=F32)
             + jnp.dot(tril, gmid, preferred_element_type=F32)
             + jnp.dot(tril, glo, preferred_element_type=F32))
        b2s.append(b * LOG2E)
        b_scr[rows, :] = b2s[-1]
    span = jnp.zeros((1, HEAD_DIM), F32)
    for b2 in b2s:
        span = jnp.maximum(span, -b2[C - 1:C])
    safe = jnp.max(span) < HGRN_SAFE_LOG2

    qes, kes = [], []
    for rows, k, b2 in zip(chunks, ks, b2s):
        qes.append((q_ref[rows, :].astype(F32) * jnp.exp2(b2)).astype(BF16))
        kes.append(k * jnp.exp2(-b2))
        qe_scr[rows, :] = qes[-1]
    scores = [lax.dot_general(qe, ke.astype(BF16), NT_DIMS, preferred_element_type=F32)
              for qe, ke in zip(qes, kes)]
    vt = v_ref[...].astype(F32).T.astype(BF16)
    zero = jnp.zeros((C, HEAD_DIM), BF16)
    for c, (rows, sc, ke, b2) in enumerate(zip(chunks, scores, kes, b2s)):
        bl = b2[C - 1:C]
        oi_scr[rows, :] = jnp.dot(jnp.where(causal, sc, 0.0).astype(BF16), v_ref[rows, :],
                                  preferred_element_type=F32)
        kd = (ke * jnp.exp2(bl)).astype(BF16)
        pair = slice(c // 2 * 2 * C, (c // 2 + 1) * 2 * C)
        u_scr[c] = jnp.dot(vt[:, pair], jnp.concatenate([kd, zero] if c % 2 == 0 else [zero, kd]),
                           preferred_element_type=F32)
        d_scr[c:c + 1, :] = jnp.exp2(bl)

    @pl.when(safe)
    def _():
        st = st_scr[...]
        for c in range(n_chunks):
            stb_scr[c] = st.astype(BF16)
            st = st * d_scr[c:c + 1, :] + u_scr[c]
        st_scr[...] = st
        carried = [lax.dot_general(qe_scr[rows, :], stb_scr[c], NT_DIMS, preferred_element_type=F32)
                   for c, rows in enumerate(chunks)]
        for rows, oc in zip(chunks, carried):
            emit(rows, oi_scr[rows, :] + oc)

    def robust_chunk(c):
        rows = slice(c * C, (c + 1) * C)
        q = q_ref[rows, :].astype(F32)
        v = v_ref[rows, :]
        k, b2 = k_scr[rows, :], b_scr[rows, :]
        st = st_scr[...]
        o = lax.dot_general((q * jnp.exp2(b2)).astype(BF16), st.astype(BF16), NT_DIMS,
                            preferred_element_type=F32)
        a_rows = []
        for i in range(C // SUB):
            lo, hi = i * SUB, (i + 1) * SUB
            bq, qq, kk = b2[lo:hi], q[lo:hi], k[lo:hi]
            if i > 0:
                r = b2[lo - 1:lo]
                qi = (qq * jnp.exp2(bq - r)).astype(BF16)
                ki = (k * jnp.exp2(jnp.minimum(r - b2, 0.0))).astype(BF16)
                a = lax.dot_general(qi, ki, NT_DIMS, preferred_element_type=F32)
                a = jnp.where(sub_c < lo, a, 0.0)
            else:
                a = jnp.zeros((SUB, C), F32)
            a8 = [a[j * 8:(j + 1) * 8] for j in range(SUB // 8)]
            for s in range(SUB):
                for j in range(s // 8, SUB // 8):
                    grp = slice(j * 8, (j + 1) * 8)
                    ed = qq[grp] * jnp.exp2(bq[grp] - bq[s:s + 1]) * kk[s:s + 1]
                    col = jnp.sum(ed, axis=-1, keepdims=True)
                    hit = col8 == lo + s
                    if j == s // 8:
                        hit = hit & (row8 >= s - j * 8)
                    a8[j] = jnp.where(hit, col, a8[j])
            a_rows.extend(a8)
        scores = jnp.concatenate(a_rows, axis=0)
        o = o + jnp.dot(scores.astype(BF16), v, preferred_element_type=F32)
        bl = b2[C - 1:C]
        st_scr[...] = st * jnp.exp2(bl) + lax.dot_general(
            v, (k * jnp.exp2(bl - b2)).astype(BF16), TN_DIMS, preferred_element_type=F32)
        emit(rows, o)

    @pl.when(jnp.logical_not(safe))
    def _():
        for c in range(n_chunks):
            robust_chunk(c)


def _hgrn_recurrence(proj, lb_param, head_gain, batch, seq, *, t_blk=2048):
    n = proj.shape[0]
    nh = HGRN_HEADS
    t_blk = min(t_blk, seq)
    ns = seq // t_blk

    def sec(k):
        return pl.BlockSpec((t_blk, HEAD_DIM), lambda b, h, s, k=k: (b * ns + s, k * nh + h))

    n_chunks = t_blk // HGRN_CHUNK
    return pl.pallas_call(
        functools.partial(_hgrn_kernel, n_chunks=n_chunks),
        grid=(batch, nh, ns),
        in_specs=[sec(0), sec(1), sec(2), sec(3),
                  pl.BlockSpec((lb_param.shape[0], HEAD_DIM), lambda b, h, s: (0, h)),
                  pl.BlockSpec((1, HEAD_DIM), lambda b, h, s: (0, 0))],
        out_specs=pl.BlockSpec((t_blk, HEAD_DIM), lambda b, h, s: (b * ns + s, h)),
        out_shape=jax.ShapeDtypeStruct((n, nh * HEAD_DIM), BF16),
        scratch_shapes=[pltpu.VMEM((HEAD_DIM, HEAD_DIM), F32),
                        pltpu.VMEM((t_blk, HEAD_DIM), F32),
                        pltpu.VMEM((t_blk, HEAD_DIM), F32),
                        pltpu.VMEM((t_blk, HEAD_DIM), BF16),
                        pltpu.VMEM((t_blk, HEAD_DIM), F32),
                        pltpu.VMEM((n_chunks, HEAD_DIM, HEAD_DIM), F32),
                        pltpu.VMEM((n_chunks, HEAD_DIM), F32),
                        pltpu.VMEM((n_chunks, HEAD_DIM, HEAD_DIM), BF16)],
        compiler_params=pltpu.CompilerParams(
            dimension_semantics=("arbitrary", "arbitrary", "arbitrary")),
        name="hgrn_recurrence",
    )(proj, proj, proj, proj, lb_param, head_gain)


def _kv_prep_kernel(kc_ref, vc_ref, vs_ref, vw_ref, pek_ref, pev_ref, w1k_ref, w1v_ref,
                    b1k_ref, b1v_ref, w2k_ref, w2v_ref, ko_ref, vto_ref, vts_ref, vtw_ref, x_scr,
                    *, nc, ntile):
    half = CMP_BLOCK // 2
    TK = ATT_TK

    def phi(src_ref, pe_ref, w1_ref, b1_ref, w2_ref):
        x_scr[...] = src_ref[...].astype(F32)
        top = jnp.zeros((nc, HEAD_DIM), F32)
        bot = jnp.zeros((nc, HEAD_DIM), F32)
        for p in range(CMP_STRIDE):
            xp = x_scr[pl.ds(p, nc, stride=CMP_STRIDE), :]
            xa = (xp + pe_ref[p:p + 1, :]).astype(BF16)
            xb = (xp + pe_ref[half + p:half + p + 1, :]).astype(BF16)
            top = top + jnp.dot(xa, w1_ref[p * HEAD_DIM:(p + 1) * HEAD_DIM, :],
                                preferred_element_type=F32)
            bot = bot + jnp.dot(xb, w1_ref[(half + p) * HEAD_DIM:(half + p + 1) * HEAD_DIM, :],
                                preferred_element_type=F32)
        hid = top + pltpu.roll(bot, nc - 1, 0) + b1_ref[...]
        act = jax.nn.gelu(hid, approximate=True)
        return jnp.dot(act.astype(BF16), w2_ref[...], preferred_element_type=F32)

    ko_ref[...] = phi(kc_ref, pek_ref, w1k_ref, b1k_ref, w2k_ref).astype(BF16)
    vto_ref[...] = phi(vc_ref, pev_ref, w1v_ref, b1v_ref, w2v_ref).T.astype(BF16)

    ones_row = (lax.broadcasted_iota(jnp.int32, (V_PAD, TK), 0) == 0).astype(BF16)

    def tr(u, carry):
        rows = pl.ds(pl.multiple_of(u * TK, TK), TK)
        vts_ref[u] = jnp.concatenate([vs_ref[rows, :].astype(F32).T.astype(BF16), ones_row], axis=0)
        vtw_ref[u] = jnp.concatenate([vw_ref[rows, :].astype(F32).T.astype(BF16), ones_row], axis=0)
        return carry
    lax.fori_loop(0, ntile, tr, 0)


def _kv_prep(proj, blk, pe_k, pe_v, w1k, w1v, b1k, b1v, w2k, w2v, batch, seq):
    g = NSA_GROUPS
    nc = seq // CMP_STRIDE
    ntile = seq // ATT_TK

    def full(a):
        return pl.BlockSpec(a.shape, lambda b, gg: (0,) * a.ndim)

    def seq_block(col0):
        return pl.BlockSpec((seq, HEAD_DIM), lambda b, gg: (b, col0 + gg))

    def out(*shape):
        spec = pl.BlockSpec((None, None) + shape, lambda b, gg: (b, gg) + (0,) * len(shape))
        return spec, jax.ShapeDtypeStruct((batch, g) + shape, BF16)

    specs, shapes = zip(out(nc, HEAD_DIM), out(HEAD_DIM, nc),
                        out(ntile, HEAD_DIM + V_PAD, ATT_TK), out(ntile, HEAD_DIM + V_PAD, ATT_TK))
    return pl.pallas_call(
        functools.partial(_kv_prep_kernel, nc=nc, ntile=ntile),
        grid=(batch, g),
        in_specs=[seq_block(blk["k_c"]), seq_block(blk["v_c"]), seq_block(blk["v_s"]),
                  seq_block(blk["v_w"]),
                  full(pe_k), full(pe_v), full(w1k), full(w1v), full(b1k), full(b1v),
                  full(w2k), full(w2v)],
        out_specs=list(specs),
        out_shape=list(shapes),
        scratch_shapes=[pltpu.VMEM((seq, HEAD_DIM), F32)],
        compiler_params=pltpu.CompilerParams(dimension_semantics=("arbitrary", "arbitrary")),
        name="nsa_kv_prep",
    )(proj, proj, proj, proj, pe_k, pe_v, w1k, w1v, b1k, b1v, w2k, w2v)


def _nsa_attn_kernel(q_ref, kc_ref, vtc_ref, ks_ref, vts_ref, kw_ref, vtw_ref, z_ref, gt_ref,
                     win_ref, cmpb_ref, ovt_ref, o_ref,
                     q_scr, psum_scr, rank_scr, selm_scr, s_scr, m8_scr, acc_scr, tot_scr,
                     *, seq):
    TQ, TK = ATT_TQ, ATT_TK
    L = NSA_HPG * TQ
    nc = seq // CMP_STRIDE
    nsel = seq // SEL_BLOCK
    blk_per_tile = TK // SEL_BLOCK
    n_win = (WINDOW + TQ) // TK
    g = pl.program_id(1)
    i = pl.program_id(2)
    t0 = i * TQ

    for h in range(NSA_HPG):
        q_scr[h * TQ:(h + 1) * TQ, :] = q_ref[:, h * HEAD_DIM:(h + 1) * HEAD_DIM]

    def gate_row(c):
        rows = [(g * NSA_HPG + h) * N_BRANCH + c for h in range(NSA_HPG)]
        return jnp.concatenate([_sigmoid(gt_ref[pl.ds(r, 1), :]) for r in rows], axis=1)

    cmp_row0 = pl.multiple_of(nc - i * (TQ // CMP_STRIDE), 8)

    def cmp_branch(rows):
        s = lax.dot_general(kc_ref[0:rows, :], q_scr[...], NT_DIMS, preferred_element_type=F32)
        s = s + cmpb_ref[pl.ds(cmp_row0, rows), :]
        mc = jnp.max(s, axis=0, keepdims=True)
        p = jnp.exp2(s - mc)
        inv = jnp.where(mc > 0.1 * NEG_INF, 1.0 / jnp.sum(p, axis=0, keepdims=True), 0.0)
        p = p * inv
        psum = p[:, 0:TQ]
        for h in range(1, NSA_HPG):
            psum = psum + p[:, h * TQ:(h + 1) * TQ]
        if rows < nc:
            psum = jnp.concatenate([psum, jnp.zeros((nc - rows, TQ), F32)], axis=0)
        psum_scr[...] = psum
        oc = jnp.dot(vtc_ref[:, 0:rows], p.astype(BF16), preferred_element_type=F32)
        tot_scr[...] = oc * gate_row(0)

    qblk_per_rows = CMP_ROWS // (TQ // CMP_STRIDE)
    for k in range(1, nc // CMP_ROWS + 1):
        pl.when(i // qblk_per_rows + 1 == k)(functools.partial(cmp_branch, k * CMP_ROWS))

    ovt = ovt_ref[...]
    phi, pmid, plo = _split3(psum_scr[...])
    imp = (jnp.dot(ovt, phi, preferred_element_type=F32)
           + jnp.dot(ovt, pmid, preferred_element_type=F32)
           + jnp.dot(ovt, plo, preferred_element_type=F32))

    jrow = lax.broadcasted_iota(jnp.int32, (nsel, TQ), 0)
    tcol = t0 + lax.broadcasted_iota(jnp.int32, (nsel, TQ), 1)
    cur = lax.shift_right_arithmetic(tcol, int(math.log2(SEL_BLOCK)))
    forced = (jrow == 0) | (jrow == cur) | (jrow == cur - 1)
    visible = jrow * SEL_BLOCK <= tcol
    score = jnp.where(forced, FORCE_SCORE, jnp.where(visible, imp, NEG_INF))
    row8 = lax.broadcasted_iota(jnp.int32, (8, TQ), 0)
    score8 = [score[r * 8:(r + 1) * 8] for r in range(nsel // 8)]
    rank_scr[...] = jnp.zeros_like(rank_scr)
    n_visible = (i + 1) * (TQ // SEL_BLOCK)

    def count_ahead(grp):
        cnt = [jnp.zeros((8, TQ), jnp.int32) for _ in score8]
        for jp in range(grp * 8, grp * 8 + 8):
            row = score[jp:jp + 1, :]
            for r, sc in enumerate(score8):
                if r * 8 > jp:
                    ahead = row >= sc
                elif r * 8 + 7 < jp:
                    ahead = row > sc
                else:
                    ahead = (row > sc) | ((row == sc) & (row8 > jp - r * 8))
                cnt[r] = cnt[r] + ahead.astype(jnp.int32)
        for r, c in enumerate(cnt):
            rank_scr[r * 8:(r + 1) * 8, :] += c

    for grp in range(nsel // 8):
        pl.when(grp * 8 < n_visible)(functools.partial(count_ahead, grp))
    rank = rank_scr[...]
    selm = jnp.where(rank < min(N_SELECT, nsel), 0.0, NEG_INF)
    selm = jnp.concatenate([selm] * NSA_HPG, axis=1)
    selm_scr[...] = selm

    def step_rows(u0, nt):
        return pl.ds(pl.multiple_of(u0 * TK, TK), nt * TK)

    def logits_step(k_ref, u0, nt, win_row0):
        s = lax.dot_general(k_ref[step_rows(u0, nt), :], q_scr[...], NT_DIMS,
                            preferred_element_type=F32)
        if win_row0 is not None:
            s = s + win_ref[win_row0:win_row0 + nt * TK, :]
        else:
            parts = []
            for j in range(nt):
                near = u0 + j - (i - (n_win - 2))
                row0 = pl.multiple_of(jnp.where(near < 0, n_win, near + n_win - 2) * TK, TK)
                for r in range(blk_per_tile):
                    lo = j * TK + r * SEL_BLOCK
                    parts.append(s[lo:lo + SEL_BLOCK]
                                 + win_ref[pl.ds(row0 + r * SEL_BLOCK, SEL_BLOCK), :]
                                 + selm_scr[pl.ds((u0 + j) * blk_per_tile + r, 1), :])
            s = jnp.concatenate(parts, axis=0)
        s_scr[step_rows(u0, nt), :] = s
        m8_scr[...] = jnp.maximum(m8_scr[...], jnp.max(s.reshape(nt * TK // 8, 8, L), axis=0))

    def pv_step(vt_ref, m, u0, nt):
        p = jnp.exp2((s_scr[step_rows(u0, nt), :] - m).astype(BF16))
        vt = jnp.concatenate([vt_ref[u0 + j] for j in range(nt)], axis=1)
        acc_scr[...] += jnp.dot(vt, p, preferred_element_type=F32)

    def over_tiles(u_lo, n, step):
        big = ATT_STEP_TILES
        nbig = lax.shift_right_logical(n, int(math.log2(big)))

        def body(j, carry):
            step(u_lo + big * j, big)
            return carry
        lax.fori_loop(0, nbig, body, 0)
        u = u_lo + big * nbig
        nt = big // 2
        while nt >= 1:
            pl.when((n & nt) != 0)(functools.partial(step, u, nt))
            u = u + (n & nt)
            nt //= 2

    def near_diagonal(step):
        for nt in range(1, n_win + 1):
            cond = (i == nt - 1) if nt < n_win else (i >= nt - 1)
            pl.when(cond)(functools.partial(step, i - (nt - 1), nt, (n_win - nt) * TK))

    def finish(c):
        w = gate_row(c) * (1.0 / acc_scr[HEAD_DIM:HEAD_DIM + 1, :])
        tot_scr[...] = tot_scr[...] + acc_scr[0:HEAD_DIM, :] * w

    def reset():
        m8_scr[...] = jnp.full_like(m8_scr, NEG_INF)
        acc_scr[...] = jnp.zeros_like(acc_scr)

    reset()
    over_tiles(0, i + 1, lambda u0, nt: logits_step(ks_ref, u0, nt, None))
    m_sel = jnp.max(m8_scr[...], axis=0, keepdims=True)
    over_tiles(0, i + 1, functools.partial(pv_step, vts_ref, m_sel))
    finish(1)

    reset()
    near_diagonal(lambda u0, nt, row0: logits_step(kw_ref, u0, nt, row0))
    m_win = jnp.max(m8_scr[...], axis=0, keepdims=True)
    near_diagonal(lambda u0, nt, row0: pv_step(vtw_ref, m_win, u0, nt))
    finish(2)

    for h in range(NSA_HPG):
        cols = slice(h * HEAD_DIM, (h + 1) * HEAD_DIM)
        z = z_ref[:, cols].astype(F32)
        o_ref[:, cols] = (tot_scr[:, h * TQ:(h + 1) * TQ].T * (z * _sigmoid(z))).astype(o_ref.dtype)


def _nsa_attention(proj, kcmp, vtc, vts, vtw, gt, win, cmpb, ovt, cols, batch, seq):
    TQ, TK = ATT_TQ, ATT_TK
    n = proj.shape[0]
    ng = NSA_GROUPS
    nq = seq // TQ
    L = NSA_HPG * TQ
    gw = NSA_HPG * HEAD_DIM

    def seq_block(col0):
        return pl.BlockSpec((seq, HEAD_DIM), lambda b, g, i: (b, col0 + g))

    def per_bg(a):
        return pl.BlockSpec((None, None) + a.shape[2:], lambda b, g, i: (b, g) + (0,) * (a.ndim - 2))

    def per_g(a):
        return pl.BlockSpec((None,) + a.shape[1:], lambda b, g, i: (g,) + (0,) * (a.ndim - 1))

    return pl.pallas_call(
        functools.partial(_nsa_attn_kernel, seq=seq),
        grid=(batch, ng, nq),
        in_specs=[pl.BlockSpec((TQ, gw), lambda b, g, i: (b * nq + i, cols["q"] + g)),
                  per_bg(kcmp), per_bg(vtc),
                  seq_block(cols["k_s"]), per_bg(vts),
                  seq_block(cols["k_w"]), per_bg(vtw),
                  pl.BlockSpec((TQ, gw), lambda b, g, i: (b * nq + i, cols["z"] + g)),
                  pl.BlockSpec((gt.shape[0], TQ), lambda b, g, i: (0, b * nq + i)),
                  per_g(win), per_g(cmpb),
                  pl.BlockSpec(ovt.shape, lambda b, g, i: (0, 0))],
        out_specs=pl.BlockSpec((TQ, gw), lambda b, g, i: (b * nq + i, g)),
        out_shape=jax.ShapeDtypeStruct((n, ng * gw), BF16),
        scratch_shapes=[pltpu.VMEM((L, HEAD_DIM), BF16),
                        pltpu.VMEM((seq // CMP_STRIDE, TQ), F32),
                        pltpu.VMEM((seq // SEL_BLOCK, TQ), jnp.int32),
                        pltpu.VMEM((seq // SEL_BLOCK, L), F32),
                        pltpu.VMEM((seq, L), F32),
                        pltpu.VMEM((8, L), F32),
                        pltpu.VMEM((HEAD_DIM + V_PAD, L), F32),
                        pltpu.VMEM((HEAD_DIM, L), F32)],
        compiler_params=pltpu.CompilerParams(
            dimension_semantics=("arbitrary", "arbitrary", "arbitrary"),
            vmem_limit_bytes=ATT_VMEM_LIMIT),
        name="nsa_attention",
    )(proj, kcmp, vtc, proj, vts, proj, vtw, proj, gt, win, cmpb, ovt)


def _bucket_of_distance():
    d = np.arange(MAX_DISTANCE)
    max_exact = NUM_BUCKETS // 2
    large = max_exact + (np.log(np.maximum(d, 1).astype(np.float32) / max_exact)
                         / math.log(MAX_DISTANCE / max_exact) * (NUM_BUCKETS - max_exact)).astype(np.int32)
    large = np.minimum(large, NUM_BUCKETS - 1)
    return np.where(d < max_exact, d, large)


CMP_NEAR = 16


def _bias_builder_kernel(vw_ref, vc_ref, far_ref, win_ref, cmpb_ref, *, nc):
    TQ, TK = ATT_TQ, ATT_TK
    n_win = (WINDOW + TQ) // TK
    for h in range(NSA_HPG):
        lanes = slice(h * TQ, (h + 1) * TQ)
        for w in range(n_win):
            v = vw_ref[h * n_win + w:h * n_win + w + 1, :]
            r = pltpu.roll(jnp.broadcast_to(v, (TK, 2 * TQ)), 0, 1, stride=1, stride_axis=0)
            win_ref[w * TK:(w + 1) * TK, lanes] = r[:, TQ:2 * TQ]
        win_ref[n_win * TK:(n_win + 1) * TK, lanes] = jnp.broadcast_to(far_ref[:, lanes], (TK, TQ))
        cmpb_ref[0:nc - CMP_NEAR, lanes] = jnp.broadcast_to(far_ref[:, lanes], (nc - CMP_NEAR, TQ))
        vc = vc_ref[h:h + 1, :]
        rc = pltpu.roll(jnp.broadcast_to(vc, (2 * CMP_NEAR, 4 * TQ)), 0, 1,
                        stride=CMP_STRIDE, stride_axis=0)
        cmpb_ref[nc - CMP_NEAR:nc + CMP_NEAR, lanes] = rc[:, 2 * TQ:3 * TQ]
        cmpb_ref[nc + CMP_NEAR:2 * nc, lanes] = jnp.full((nc - CMP_NEAR, TQ), NEG_INF, F32)


def _bias_tiles(rel_bias, seq):
    TQ, TK = ATT_TQ, ATT_TK
    assert TQ == TK and TQ >= MAX_DISTANCE
    assert CMP_STRIDE * CMP_NEAR >= TQ and 2 * TQ >= CMP_STRIDE * 2 * CMP_NEAR
    assert CMP_STRIDE * (CMP_NEAR + 1) - (CMP_BLOCK - 1) >= MAX_DISTANCE - 1
    nc = seq // CMP_STRIDE
    ng, hpg = NSA_GROUPS, NSA_HPG
    n_win = (WINDOW + TQ) // TK
    tb = (rel_bias.astype(F32) * LOG2E)[_bucket_of_distance()].T.reshape(ng, hpg, MAX_DISTANCE)

    def by_distance(dist, valid):
        vals = tb[:, :, np.clip(dist, 0, MAX_DISTANCE - 1)]
        return jnp.where(valid, vals, NEG_INF)

    x = np.arange(2 * TQ)[None, :]
    dw = WINDOW - TK * np.arange(n_win)[:, None] - TQ + x
    vw = by_distance(dw, (dw >= 0) & (dw < WINDOW)).reshape(ng, hpg * n_win, 2 * TQ)
    xc = np.arange(4 * TQ)
    dc = xc - 2 * TQ + CMP_STRIDE * CMP_NEAR - (CMP_BLOCK - 1)
    vc = by_distance(dc, dc >= 0)
    L = hpg * TQ
    far = jnp.broadcast_to(tb[:, :, MAX_DISTANCE - 1:], (ng, hpg, TQ)).reshape(ng, 1, L)

    win, cmpb = pl.pallas_call(
        functools.partial(_bias_builder_kernel, nc=nc),
        grid=(ng,),
        in_specs=[pl.BlockSpec((None,) + vw.shape[1:], lambda g: (g, 0, 0)),
                  pl.BlockSpec((None,) + vc.shape[1:], lambda g: (g, 0, 0)),
                  pl.BlockSpec((None,) + far.shape[1:], lambda g: (g, 0, 0))],
        out_specs=[pl.BlockSpec((None, WINDOW + TQ + TK, L), lambda g: (g, 0, 0)),
                   pl.BlockSpec((None, 2 * nc, L), lambda g: (g, 0, 0))],
        out_shape=[jax.ShapeDtypeStruct((ng, WINDOW + TQ + TK, L), F32),
                   jax.ShapeDtypeStruct((ng, 2 * nc, L), F32)],
        compiler_params=pltpu.CompilerParams(dimension_semantics=("arbitrary",)),
        name="nsa_bias_tiles",
    )(vw, vc, far)
    return win, cmpb


def _overlap_t(seq):
    nc = seq // CMP_STRIDE
    nsel = seq // SEL_BLOCK
    ci = np.arange(nc)[None, :]
    sj = np.arange(nsel)[:, None]
    ov = (CMP_STRIDE * ci < SEL_BLOCK * (sj + 1)) & (CMP_STRIDE * ci + CMP_BLOCK > SEL_BLOCK * sj)
    ov = ov & (ci < nc - 1)
    return jnp.asarray(ov, BF16)


def kernel(x, norm_gains, final_gain, rel_bias, hgrn_lb, hgrn_w_in, hgrn_head_gain, hgrn_w_out,
           nsa_w_in, nsa_pe_k, nsa_pe_v, nsa_phi_k_w1, nsa_phi_k_b1, nsa_phi_k_w2,
           nsa_phi_v_w1, nsa_phi_v_b1, nsa_phi_v_w2, nsa_w_out):
    batch, seq, d = x.shape
    n = batch * seq
    x2d = x.reshape(n, d)
    qscale = HEAD_DIM ** -0.5

    def scale_q_columns(w, scale):
        col = np.ones((1, w.shape[1]), np.float32)
        col[:, :d] = scale
        return (w * col).astype(BF16)

    proj = _norm_proj(x2d, norm_gains[0:1], scale_q_columns(hgrn_w_in[0], qscale))
    o = _hgrn_recurrence(proj, hgrn_lb, hgrn_head_gain[0:1], batch, seq)
    x1, h1 = _out_proj(o, hgrn_w_out[0].astype(BF16), x2d, norm_gains[1:2], final=False)

    kvw = NSA_GROUPS * HEAD_DIM
    gate0 = d + 6 * kvw
    gate1 = gate0 + NSA_HEADS * N_BRANCH
    w_main, wg = _nsa_weight_prep(nsa_w_in, d, gate0, gate1, qscale * LOG2E)
    proj, gt = _proj_gate(h1, w_main, wg)

    blk = {}
    for idx, name in enumerate(("k_c", "v_c", "k_s", "v_s", "k_w", "v_w")):
        blk[name] = (d + idx * kvw) // HEAD_DIM
    gw = NSA_HPG * HEAD_DIM
    cols = dict(blk, q=0, z=gate0 // gw)

    kcmp, vtc, vts, vtw = _kv_prep(
        proj, blk, nsa_pe_k[0], nsa_pe_v[0],
        nsa_phi_k_w1[0].astype(BF16), nsa_phi_v_w1[0].astype(BF16),
        nsa_phi_k_b1[0:1], nsa_phi_v_b1[0:1],
        nsa_phi_k_w2[0].astype(BF16), nsa_phi_v_w2[0].astype(BF16), batch, seq)

    win, cmpb = _bias_tiles(rel_bias, seq)
    o = _nsa_attention(proj, kcmp, vtc, vts, vtw, gt, win, cmpb, _overlap_t(seq), cols, batch, seq)
    out, = _out_proj(o, nsa_w_out[0].astype(BF16), x1, final_gain[None, :], final=True)
    return out.reshape(batch, seq, d)
```

```python
import functools
import math

import numpy as np
import jax
import jax.numpy as jnp
from jax import lax
from jax.experimental import pallas as pl
from jax.experimental.pallas import tpu as pltpu

F32 = jnp.float32
BF16 = jnp.bfloat16

D_MODEL = 2048
RMS_EPS = 1e-6
HEAD_DIM = 128

HGRN_HEADS = D_MODEL // HEAD_DIM
HGRN_CHUNK = 64
HGRN_SUB = 16
HGRN_SAFE_LOG2 = 96.0

NSA_HEADS = D_MODEL // HEAD_DIM
NSA_GROUPS = 4
NSA_HPG = NSA_HEADS // NSA_GROUPS
N_BRANCH = 3
CMP_BLOCK = 32
CMP_STRIDE = 16
SEL_BLOCK = 64
N_SELECT = 16
WINDOW = 512
NUM_BUCKETS = 32
MAX_DISTANCE = 128
NEG_INF = -1e30
FORCE_SCORE = 1e9

ATT_TQ = 256
ATT_TK = 256
V_PAD = 16
ATT_STEP_TILES = 16
CMP_ROWS = 64
WPREP_TAIL = 64
LOG2E = math.log2(math.e)
V7X_VMEM_BYTES = 64 * 1024 * 1024
VMEM_LIMIT = V7X_VMEM_BYTES * 3 // 4
ATT_VMEM_LIMIT = V7X_VMEM_BYTES * 7 // 8

NT_DIMS = (((1,), (1,)), ((), ()))
TN_DIMS = (((0,), (0,)), ((), ()))


def _sigmoid(x):
    return 1.0 / (1.0 + jnp.exp(-x))


def _split3(a):
    hi = a.astype(BF16)
    r1 = a - hi.astype(F32)
    mid = r1.astype(BF16)
    lo = (r1 - mid.astype(F32)).astype(BF16)
    return hi, mid, lo


def _norm_proj_kernel(x_ref, g_ref, w_ref, o_ref, h_scr):
    @pl.when(pl.program_id(1) == 0)
    def _():
        x = x_ref[...]
        ms = jnp.mean(x * x, axis=-1, keepdims=True)
        h_scr[...] = (x * lax.rsqrt(ms + RMS_EPS) * g_ref[...]).astype(BF16)

    o_ref[...] = jnp.dot(h_scr[...], w_ref[...], preferred_element_type=F32).astype(o_ref.dtype)


def _norm_proj(x, gain, w, *, tm=1024, tn=2048):
    n, d = x.shape
    nout = w.shape[1]
    return pl.pallas_call(
        _norm_proj_kernel,
        grid=(n // tm, nout // tn),
        in_specs=[pl.BlockSpec((tm, d), lambda i, j: (i, 0)),
                  pl.BlockSpec((1, d), lambda i, j: (0, 0)),
                  pl.BlockSpec((d, tn), lambda i, j: (0, j))],
        out_specs=pl.BlockSpec((tm, tn), lambda i, j: (i, j)),
        out_shape=jax.ShapeDtypeStruct((n, nout), BF16),
        scratch_shapes=[pltpu.VMEM((tm, d), BF16)],
        compiler_params=pltpu.CompilerParams(
            dimension_semantics=("arbitrary", "arbitrary"), vmem_limit_bytes=VMEM_LIMIT),
        name="norm_proj",
    )(x, gain, w)


def _proj_gate_kernel(h_ref, wt_ref, wgt_ref, o_ref, gt_ref):
    @pl.when(pl.program_id(1) == 0)
    def _():
        gt_ref[...] = lax.dot_general(wgt_ref[...], h_ref[...], NT_DIMS, preferred_element_type=F32)

    o_ref[...] = lax.dot_general(h_ref[...], wt_ref[...], NT_DIMS,
                                 preferred_element_type=F32).astype(o_ref.dtype)


def _proj_gate(h, wt, wgt, *, tm=1024, tn=1792):
    n, d = h.shape
    nout = wt.shape[0]
    ng = wgt.shape[0]
    return pl.pallas_call(
        _proj_gate_kernel,
        grid=(n // tm, nout // tn),
        in_specs=[pl.BlockSpec((tm, d), lambda i, j: (i, 0)),
                  pl.BlockSpec((tn, d), lambda i, j: (j, 0)),
                  pl.BlockSpec((ng, d), lambda i, j: (0, 0))],
        out_specs=[pl.BlockSpec((tm, tn), lambda i, j: (i, j)),
                   pl.BlockSpec((ng, tm), lambda i, j: (0, i))],
        out_shape=[jax.ShapeDtypeStruct((n, nout), BF16),
                   jax.ShapeDtypeStruct((ng, n), F32)],
        compiler_params=pltpu.CompilerParams(
            dimension_semantics=("arbitrary", "arbitrary"), vmem_limit_bytes=VMEM_LIMIT),
        name="proj_gate",
    )(h, wt, wgt)


def _nsa_weight_prep_kernel(wa_ref, wb_ref, o_ref, g_ref, *, q_blocks, shifted_from, shift, scale):
    i = pl.program_id(0)
    tr, d = o_ref.shape

    @pl.when(i < shifted_from)
    def _():
        o_ref[...] = (wa_ref[...] * jnp.where(i < q_blocks, scale, 1.0)).astype(BF16)

    @pl.when(i >= shifted_from)
    def _():
        both = jnp.concatenate([wa_ref[...], wb_ref[0:WPREP_TAIL, :]], axis=0)
        o_ref[...] = both[shift:shift + tr, :].astype(BF16)

    @pl.when(i == shifted_from)
    def _():
        g_ref[...] = jnp.concatenate(
            [wa_ref[0:shift, :], jnp.zeros((HEAD_DIM - shift, d), F32)], axis=0).astype(BF16)


def _nsa_weight_prep(w3, d, gate0, gate1, scale, *, tr=512):
    wt = jnp.swapaxes(w3, 1, 2)
    n_proj = wt.shape[1]
    shift = gate1 - gate0
    n_out = n_proj - shift
    assert gate0 % tr == 0 and n_out % tr == 0 and d % tr == 0
    assert shift % 16 == 0 and shift <= WPREP_TAIL <= tr
    last_src = (n_proj - 1) // tr
    first_b = gate0 // tr + 1
    return pl.pallas_call(
        functools.partial(_nsa_weight_prep_kernel, q_blocks=d // tr, shifted_from=gate0 // tr,
                          shift=shift, scale=scale),
        grid=(n_out // tr,),
        in_specs=[pl.BlockSpec((None, tr, d), lambda i: (0, i, 0)),
                  pl.BlockSpec((None, tr, d),
                               lambda i: (0, jnp.minimum(jnp.maximum(i + 1, first_b), last_src), 0))],
        out_specs=[pl.BlockSpec((tr, d), lambda i: (i, 0)),
                   pl.BlockSpec((HEAD_DIM, d), lambda i: (0, 0))],
        out_shape=[jax.ShapeDtypeStruct((n_out, d), BF16),
                   jax.ShapeDtypeStruct((HEAD_DIM, d), BF16)],
        compiler_params=pltpu.CompilerParams(
            dimension_semantics=("arbitrary",), vmem_limit_bytes=VMEM_LIMIT),
        name="nsa_weight_prep",
    )(wt, wt)


def _out_proj_kernel(o_ref, w_ref, res_ref, g_ref, *out_refs, final):
    xn = res_ref[...] + jnp.dot(o_ref[...], w_ref[...], preferred_element_type=F32)
    ms = jnp.mean(xn * xn, axis=-1, keepdims=True)
    y = xn * lax.rsqrt(ms + RMS_EPS) * g_ref[...]
    if final:
        out_refs[0][...] = y
    else:
        out_refs[0][...] = xn
        out_refs[1][...] = y.astype(BF16)


def _out_proj(o, w, res, gain, *, final, tm=512):
    n, d = res.shape
    row = pl.BlockSpec((tm, d), lambda i: (i, 0))
    if final:
        out_specs = [row]
        out_shape = [jax.ShapeDtypeStruct((n, d), F32)]
    else:
        out_specs = [row, row]
        out_shape = [jax.ShapeDtypeStruct((n, d), F32), jax.ShapeDtypeStruct((n, d), BF16)]
    return pl.pallas_call(
        functools.partial(_out_proj_kernel, final=final),
        grid=(n // tm,),
        in_specs=[pl.BlockSpec((tm, o.shape[1]), lambda i: (i, 0)),
                  pl.BlockSpec(w.shape, lambda i: (0, 0)),
                  row,
                  pl.BlockSpec((1, d), lambda i: (0, 0))],
        out_specs=out_specs,
        out_shape=out_shape,
        compiler_params=pltpu.CompilerParams(
            dimension_semantics=("arbitrary",), vmem_limit_bytes=VMEM_LIMIT),
        name="out_proj_final" if final else "out_proj",
    )(o, w, res, gain)


def _hgrn_kernel(q_ref, f_ref, v_ref, z_ref, lbp_ref, hg_ref, o_ref,
                 st_scr, k_scr, b_scr, qe_scr, oi_scr, u_scr, d_scr, stb_scr, *, n_chunks):
    C, SUB = HGRN_CHUNK, HGRN_SUB

    @pl.when(pl.program_id(2) == 0)
    def _():
        st_scr[...] = jnp.zeros_like(st_scr)

    lbp = lbp_ref[...]
    e = jnp.exp(lbp - jnp.max(lbp, axis=0, keepdims=True))
    lb = e[0:1, :] / jnp.sum(e, axis=0, keepdims=True)
    hg = hg_ref[...]

    rr = lax.broadcasted_iota(jnp.int32, (C, C), 0)
    cc = lax.broadcasted_iota(jnp.int32, (C, C), 1)
    causal = rr >= cc
    tril = causal.astype(BF16)
    sub_c = lax.broadcasted_iota(jnp.int32, (SUB, C), 1)
    row8 = lax.broadcasted_iota(jnp.int32, (8, C), 0)
    col8 = lax.broadcasted_iota(jnp.int32, (8, C), 1)

    def emit(rows, o):
        z = z_ref[rows, :].astype(F32)
        ms = jnp.mean(o * o, axis=-1, keepdims=True)
        o_ref[rows, :] = (o * lax.rsqrt(ms + RMS_EPS) * hg * (z * _sigmoid(z))).astype(o_ref.dtype)

    chunks = [slice(c * C, (c + 1) * C) for c in range(n_chunks)]
    ks, gsplit = [], []
    for rows in chunks:
        sig = _sigmoid(f_ref[rows, :].astype(F32))
        gsplit.append(_split3(jnp.log(lb + (1.0 - lb) * sig)))
        ks.append((1.0 - lb) * (1.0 - sig))
        k_scr[rows, :] = ks[-1]
    b2s = []
    for rows, (ghi, gmid, glo) in zip(chunks, gsplit):
        b = (jnp.dot(tril, ghi, preferred_element_type=F32)
             + jnp.dot(tril, gmid, preferred_element_type=F32)
             + jnp.dot(tril, glo, preferred_element_type=F32))
        b2s.append(b * LOG2E)
        b_scr[rows, :] = b2s[-1]
    span = jnp.zeros((1, HEAD_DIM), F32)
    for b2 in b2s:
        span = jnp.maximum(span, -b2[C - 1:C])
    safe = jnp.max(span) < HGRN_SAFE_LOG2

    qes, kes = [], []
    for rows, k, b2 in zip(chunks, ks, b2s):
        qes.append((q_ref[rows, :].astype(F32) * jnp.exp2(b2)).astype(BF16))
        kes.append(k * jnp.exp2(-b2))
        qe_scr[rows, :] = qes[-1]
    scores = [lax.dot_general(qe, ke.astype(BF16), NT_DIMS, preferred_element_type=F32)
              for qe, ke in zip(qes, kes)]
    vt = v_ref[...].astype(F32).T.astype(BF16)
    zero = jnp.zeros((C, HEAD_DIM), BF16)
    for c, (rows, sc, ke, b2) in enumerate(zip(chunks, scores, kes, b2s)):
        bl = b2[C - 1:C]
        oi_scr[rows, :] = jnp.dot(jnp.where(causal, sc, 0.0).astype(BF16), v_ref[rows, :],
                                  preferred_element_type=F32)
        kd = (ke * jnp.exp2(bl)).astype(BF16)
        pair = slice(c // 2 * 2 * C, (c // 2 + 1) * 2 * C)
        u_scr[c] = jnp.dot(vt[:, pair], jnp.concatenate([kd, zero] if c % 2 == 0 else [zero, kd]),
                           preferred_element_type=F32)
        d_scr[c:c + 1, :] = jnp.exp2(bl)

    @pl.when(safe)
    def _():
        st = st_scr[...]
        for c in range(n_chunks):
            stb_scr[c] = st.astype(BF16)
            st = st * d_scr[c:c + 1, :] + u_scr[c]
        st_scr[...] = st
        carried = [lax.dot_general(qe_scr[rows, :], stb_scr[c], NT_DIMS, preferred_element_type=F32)
                   for c, rows in enumerate(chunks)]
        for rows, oc in zip(chunks, carried):
            emit(rows, oi_scr[rows, :] + oc)

    def robust_chunk(c):
        rows = slice(c * C, (c + 1) * C)
        q = q_ref[rows, :].astype(F32)
        v = v_ref[rows, :]
        k, b2 = k_scr[rows, :], b_scr[rows, :]
        st = st_scr[...]
        o = lax.dot_general((q * jnp.exp2(b2)).astype(BF16), st.astype(BF16), NT_DIMS,
                            preferred_element_type=F32)
        a_rows = []
        for i in range(C // SUB):
            lo, hi = i * SUB, (i + 1) * SUB
            bq, qq, kk = b2[lo:hi], q[lo:hi], k[lo:hi]
            if i > 0:
                r = b2[lo - 1:lo]
                qi = (qq * jnp.exp2(bq - r)).astype(BF16)
                ki = (k * jnp.exp2(jnp.minimum(r - b2, 0.0))).astype(BF16)
                a = lax.dot_general(qi, ki, NT_DIMS, preferred_element_type=F32)
                a = jnp.where(sub_c < lo, a, 0.0)
            else:
                a = jnp.zeros((SUB, C), F32)
            a8 = [a[j * 8:(j + 1) * 8] for j in range(SUB // 8)]
            for s in range(SUB):
                for j in range(s // 8, SUB // 8):
                    grp = slice(j * 8, (j + 1) * 8)
                    ed = qq[grp] * jnp.exp2(bq[grp] - bq[s:s + 1]) * kk[s:s + 1]
                    col = jnp.sum(ed, axis=-1, keepdims=True)
                    hit = col8 == lo + s
                    if j == s // 8:
                        hit = hit & (row8 >= s - j * 8)
                    a8[j] = jnp.where(hit, col, a8[j])
            a_rows.extend(a8)
        scores = jnp.concatenate(a_rows, axis=0)
        o = o + jnp.dot(scores.astype(BF16), v, preferred_element_type=F32)
        bl = b2[C - 1:C]
        st_scr[...] = st * jnp.exp2(bl) + lax.dot_general(
            v, (k * jnp.exp2(bl - b2)).astype(BF16), TN_DIMS, preferred_element_type=F32)
        emit(rows, o)

    @pl.when(jnp.logical_not(safe))
    def _():
        for c in range(n_chunks):
            robust_chunk(c)


def _hgrn_recurrence(proj, lb_param, head_gain, batch, seq, *, t_blk=2048):
    n = proj.shape[0]
    nh = HGRN_HEADS
    t_blk = min(t_blk, seq)
    ns = seq // t_blk

    def sec(k):
        return pl.BlockSpec((t_blk, HEAD_DIM), lambda b, h, s, k=k: (b * ns + s, k * nh + h))

    n_chunks = t_blk // HGRN_CHUNK
    return pl.pallas_call(
        functools.partial(_hgrn_kernel, n_chunks=n_chunks),
        grid=(batch, nh, ns),
        in_specs=[sec(0), sec(1), sec(2), sec(3),
                  pl.BlockSpec((lb_param.shape[0], HEAD_DIM), lambda b, h, s: (0, h)),
                  pl.BlockSpec((1, HEAD_DIM), lambda b, h, s: (0, 0))],
        out_specs=pl.BlockSpec((t_blk, HEAD_DIM), lambda b, h, s: (b * ns + s, h)),
        out_shape=jax.ShapeDtypeStruct((n, nh * HEAD_DIM), BF16),
        scratch_shapes=[pltpu.VMEM((HEAD_DIM, HEAD_DIM), F32),
                        pltpu.VMEM((t_blk, HEAD_DIM), F32),
                        pltpu.VMEM((t_blk, HEAD_DIM), F32),
                        pltpu.VMEM((t_blk, HEAD_DIM), BF16),
                        pltpu.VMEM((t_blk, HEAD_DIM), F32),
                        pltpu.VMEM((n_chunks, HEAD_DIM, HEAD_DIM), F32),
                        pltpu.VMEM((n_chunks, HEAD_DIM), F32),
                        pltpu.VMEM((n_chunks, HEAD_DIM, HEAD_DIM), BF16)],
        compiler_params=pltpu.CompilerParams(
            dimension_semantics=("arbitrary", "arbitrary", "arbitrary")),
        name="hgrn_recurrence",
    )(proj, proj, proj, proj, lb_param, head_gain)


def _kv_prep_kernel(kc_ref, vc_ref, vs_ref, vw_ref, pek_ref, pev_ref, w1k_ref, w1v_ref,
                    b1k_ref, b1v_ref, w2k_ref, w2v_ref, ko_ref, vto_ref, vts_ref, vtw_ref, x_scr,
                    *, nc, ntile):
    half = CMP_BLOCK // 2
    TK = ATT_TK

    def phi(src_ref, pe_ref, w1_ref, b1_ref, w2_ref):
        x_scr[...] = src_ref[...].astype(F32)
        top = jnp.zeros((nc, HEAD_DIM), F32)
        bot = jnp.zeros((nc, HEAD_DIM), F32)
        for p in range(CMP_STRIDE):
            xp = x_scr[pl.ds(p, nc, stride=CMP_STRIDE), :]
            xa = (xp + pe_ref[p:p + 1, :]).astype(BF16)
            xb = (xp + pe_ref[half + p:half + p + 1, :]).astype(BF16)
            top = top + jnp.dot(xa, w1_ref[p * HEAD_DIM:(p + 1) * HEAD_DIM, :],
                                preferred_element_type=F32)
            bot = bot + jnp.dot(xb, w1_ref[(half + p) * HEAD_DIM:(half + p + 1) * HEAD_DIM, :],
                                preferred_element_type=F32)
        hid = top + pltpu.roll(bot, nc - 1, 0) + b1_ref[...]
        act = jax.nn.gelu(hid, approximate=True)
        return jnp.dot(act.astype(BF16), w2_ref[...], preferred_element_type=F32)

    ko_ref[...] = phi(kc_ref, pek_ref, w1k_ref, b1k_ref, w2k_ref).astype(BF16)
    vto_ref[...] = phi(vc_ref, pev_ref, w1v_ref, b1v_ref, w2v_ref).T.astype(BF16)

    ones_row = (lax.broadcasted_iota(jnp.int32, (V_PAD, TK), 0) == 0).astype(BF16)

    def tr(u, carry):
        rows = pl.ds(pl.multiple_of(u * TK, TK), TK)
        vts_ref[u] = jnp.concatenate([vs_ref[rows, :].astype(F32).T.astype(BF16), ones_row], axis=0)
        vtw_ref[u] = jnp.concatenate([vw_ref[rows, :].astype(F32).T.astype(BF16), ones_row], axis=0)
        return carry
    lax.fori_loop(0, ntile, tr, 0)


def _kv_prep(proj, blk, pe_k, pe_v, w1k, w1v, b1k, b1v, w2k, w2v, batch, seq):
    g = NSA_GROUPS
    nc = seq // CMP_STRIDE
    ntile = seq // ATT_TK

    def full(a):
        return pl.BlockSpec(a.shape, lambda b, gg: (0,) * a.ndim)

    def seq_block(col0):
        return pl.BlockSpec((seq, HEAD_DIM), lambda b, gg: (b, col0 + gg))

    def out(*shape):
        spec = pl.BlockSpec((None, None) + shape, lambda b, gg: (b, gg) + (0,) * len(shape))
        return spec, jax.ShapeDtypeStruct((batch, g) + shape, BF16)

    specs, shapes = zip(out(nc, HEAD_DIM), out(HEAD_DIM, nc),
                        out(ntile, HEAD_DIM + V_PAD, ATT_TK), out(ntile, HEAD_DIM + V_PAD, ATT_TK))
    return pl.pallas_call(
        functools.partial(_kv_prep_kernel, nc=nc, ntile=ntile),
        grid=(batch, g),
        in_specs=[seq_block(blk["k_c"]), seq_block(blk["v_c"]), seq_block(blk["v_s"]),
                  seq_block(blk["v_w"]),
                  full(pe_k), full(pe_v), full(w1k), full(w1v), full(b1k), full(b1v),
                  full(w2k), full(w2v)],
        out_specs=list(specs),
        out_shape=list(shapes),
        scratch_shapes=[pltpu.VMEM((seq, HEAD_DIM), F32)],
        compiler_params=pltpu.CompilerParams(dimension_semantics=("arbitrary", "arbitrary")),
        name="nsa_kv_prep",
    )(proj, proj, proj, proj, pe_k, pe_v, w1k, w1v, b1k, b1v, w2k, w2v)


def _nsa_attn_kernel(q_ref, kc_ref, vtc_ref, ks_ref, vts_ref, kw_ref, vtw_ref, z_ref, gt_ref,
                     win_ref, cmpb_ref, ovt_ref, o_ref,
                     q_scr, psum_scr, rank_scr, selm_scr, s_scr, m8_scr, acc_scr, tot_scr,
                     *, seq):
    TQ, TK = ATT_TQ, ATT_TK
    L = NSA_HPG * TQ
    nc = seq // CMP_STRIDE
    nsel = seq // SEL_BLOCK
    blk_per_tile = TK // SEL_BLOCK
    n_win = (WINDOW + TQ) // TK
    g = pl.program_id(1)
    i = pl.program_id(2)
    t0 = i * TQ

    for h in range(NSA_HPG):
        q_scr[h * TQ:(h + 1) * TQ, :] = q_ref[:, h * HEAD_DIM:(h + 1) * HEAD_DIM]

    def gate_row(c):
        rows = [(g * NSA_HPG + h) * N_BRANCH + c for h in range(NSA_HPG)]
        return jnp.concatenate([_sigmoid(gt_ref[pl.ds(r, 1), :]) for r in rows], axis=1)

    cmp_row0 = pl.multiple_of(nc - i * (TQ // CMP_STRIDE), 8)

    def cmp_branch(rows):
        s = lax.dot_general(kc_ref[0:rows, :], q_scr[...], NT_DIMS, preferred_element_type=F32)
        s = s + cmpb_ref[pl.ds(cmp_row0, rows), :]
        mc = jnp.max(s, axis=0, keepdims=True)
        p = jnp.exp2(s - mc)
        inv = jnp.where(mc > 0.1 * NEG_INF, 1.0 / jnp.sum(p, axis=0, keepdims=True), 0.0)
        p = p * inv
        psum = p[:, 0:TQ]
        for h in range(1, NSA_HPG):
            psum = psum + p[:, h * TQ:(h + 1) * TQ]
        if rows < nc:
            psum = jnp.concatenate([psum, jnp.zeros((nc - rows, TQ), F32)], axis=0)
        psum_scr[...] = psum
        oc = jnp.dot(vtc_ref[:, 0:rows], p.astype(BF16), preferred_element_type=F32)
        tot_scr[...] = oc * gate_row(0)

    qblk_per_rows = CMP_ROWS // (TQ // CMP_STRIDE)
    for k in range(1, nc // CMP_ROWS + 1):
        pl.when(i // qblk_per_rows + 1 == k)(functools.partial(cmp_branch, k * CMP_ROWS))

    ovt = ovt_ref[...]
    phi, pmid, plo = _split3(psum_scr[...])
    imp = (jnp.dot(ovt, phi, preferred_element_type=F32)
           + jnp.dot(ovt, pmid, preferred_element_type=F32)
           + jnp.dot(ovt, plo, preferred_element_type=F32))

    jrow = lax.broadcasted_iota(jnp.int32, (nsel, TQ), 0)
    tcol = t0 + lax.broadcasted_iota(jnp.int32, (nsel, TQ), 1)
    cur = lax.shift_right_arithmetic(tcol, int(math.log2(SEL_BLOCK)))
    forced = (jrow == 0) | (jrow == cur) | (jrow == cur - 1)
    visible = jrow * SEL_BLOCK <= tcol
    score = jnp.where(forced, FORCE_SCORE, jnp.where(visible, imp, NEG_INF))
    row8 = lax.broadcasted_iota(jnp.int32, (8, TQ), 0)
    score8 = [score[r * 8:(r + 1) * 8] for r in range(nsel // 8)]
    rank_scr[...] = jnp.zeros_like(rank_scr)
    n_visible = (i + 1) * (TQ // SEL_BLOCK)

    def count_ahead(grp):
        cnt = [jnp.zeros((8, TQ), jnp.int32) for _ in score8]
        for jp in range(grp * 8, grp * 8 + 8):
            row = score[jp:jp + 1, :]
            for r, sc in enumerate(score8):
                if r * 8 > jp:
                    ahead = row >= sc
                elif r * 8 + 7 < jp:
                    ahead = row > sc
                else:
                    ahead = (row > sc) | ((row == sc) & (row8 > jp - r * 8))
                cnt[r] = cnt[r] + ahead.astype(jnp.int32)
        for r, c in enumerate(cnt):
            rank_scr[r * 8:(r + 1) * 8, :] += c

    for grp in range(nsel // 8):
        pl.when(grp * 8 < n_visible)(functools.partial(count_ahead, grp))
    rank = rank_scr[...]
    selm = jnp.where(rank < min(N_SELECT, nsel), 0.0, NEG_INF)
    selm = jnp.concatenate([selm] * NSA_HPG, axis=1)
    selm_scr[...] = selm

    def step_rows(u0, nt):
        return pl.ds(pl.multiple_of(u0 * TK, TK), nt * TK)

    def logits_step(k_ref, u0, nt, win_row0):
        s = lax.dot_general(k_ref[step_rows(u0, nt), :], q_scr[...], NT_DIMS,
                            preferred_element_type=F32)
        if win_row0 is not None:
            s = s + win_ref[win_row0:win_row0 + nt * TK, :]
        else:
            parts = []
            for j in range(nt):
                near = u0 + j - (i - (n_win - 2))
                row0 = pl.multiple_of(jnp.where(near < 0, n_win, near + n_win - 2) * TK, TK)
                for r in range(blk_per_tile):
                    lo = j * TK + r * SEL_BLOCK
                    parts.append(s[lo:lo + SEL_BLOCK]
                                 + win_ref[pl.ds(row0 + r * SEL_BLOCK, SEL_BLOCK), :]
                                 + selm_scr[pl.ds((u0 + j) * blk_per_tile + r, 1), :])
            s = jnp.concatenate(parts, axis=0)
        s_scr[step_rows(u0, nt), :] = s
        m8_scr[...] = jnp.maximum(m8_scr[...], jnp.max(s.reshape(nt * TK // 8, 8, L), axis=0))

    def pv_step(vt_ref, m, u0, nt):
        p = jnp.exp2((s_scr[step_rows(u0, nt), :] - m).astype(BF16))
        vt = jnp.concatenate([vt_ref[u0 + j] for j in range(nt)], axis=1)
        acc_scr[...] += jnp.dot(vt, p, preferred_element_type=F32)

    def over_tiles(u_lo, n, step):
        big = ATT_STEP_TILES
        nbig = lax.shift_right_logical(n, int(math.log2(big)))

        def body(j, carry):
            step(u_lo + big * j, big)
            return carry
        lax.fori_loop(0, nbig, body, 0)
        u = u_lo + big * nbig
        nt = big // 2
        while nt >= 1:
            pl.when((n & nt) != 0)(functools.partial(step, u, nt))
            u = u + (n & nt)
            nt //= 2

    def near_diagonal(step):
        for nt in range(1, n_win + 1):
            cond = (i == nt - 1) if nt < n_win else (i >= nt - 1)
            pl.when(cond)(functools.partial(step, i - (nt - 1), nt, (n_win - nt) * TK))

    def finish(c):
        w = gate_row(c) * (1.0 / acc_scr[HEAD_DIM:HEAD_DIM + 1, :])
        tot_scr[...] = tot_scr[...] + acc_scr[0:HEAD_DIM, :] * w

    def reset():
        m8_scr[...] = jnp.full_like(m8_scr, NEG_INF)
        acc_scr[...] = jnp.zeros_like(acc_scr)

    reset()
    over_tiles(0, i + 1, lambda u0, nt: logits_step(ks_ref, u0, nt, None))
    m_sel = jnp.max(m8_scr[...], axis=0, keepdims=True)
    over_tiles(0, i + 1, functools.partial(pv_step, vts_ref, m_sel))
    finish(1)

    reset()
    near_diagonal(lambda u0, nt, row0: logits_step(kw_ref, u0, nt, row0))
    m_win = jnp.max(m8_scr[...], axis=0, keepdims=True)
    near_diagonal(lambda u0, nt, row0: pv_step(vtw_ref, m_win, u0, nt))
    finish(2)

    for h in range(NSA_HPG):
        cols = slice(h * HEAD_DIM, (h + 1) * HEAD_DIM)
        z = z_ref[:, cols].astype(F32)
        o_ref[:, cols] = (tot_scr[:, h * TQ:(h + 1) * TQ].T * (z * _sigmoid(z))).astype(o_ref.dtype)


def _nsa_attention(proj, kcmp, vtc, vts, vtw, gt, win, cmpb, ovt, cols, batch, seq):
    TQ, TK = ATT_TQ, ATT_TK
    n = proj.shape[0]
    ng = NSA_GROUPS
    nq = seq // TQ
    L = NSA_HPG * TQ
    gw = NSA_HPG * HEAD_DIM

    def seq_block(col0):
        return pl.BlockSpec((seq, HEAD_DIM), lambda b, g, i: (b, col0 + g))

    def per_bg(a):
        return pl.BlockSpec((None, None) + a.shape[2:], lambda b, g, i: (b, g) + (0,) * (a.ndim - 2))

    def per_g(a):
        return pl.BlockSpec((None,) + a.shape[1:], lambda b, g, i: (g,) + (0,) * (a.ndim - 1))

    return pl.pallas_call(
        functools.partial(_nsa_attn_kernel, seq=seq),
        grid=(batch, ng, nq),
        in_specs=[pl.BlockSpec((TQ, gw), lambda b, g, i: (b * nq + i, cols["q"] + g)),
                  per_bg(kcmp), per_bg(vtc),
                  seq_block(cols["k_s"]), per_bg(vts),
                  seq_block(cols["k_w"]), per_bg(vtw),
                  pl.BlockSpec((TQ, gw), lambda b, g, i: (b * nq + i, cols["z"] + g)),
                  pl.BlockSpec((gt.shape[0], TQ), lambda b, g, i: (0, b * nq + i)),
                  per_g(win), per_g(cmpb),
                  pl.BlockSpec(ovt.shape, lambda b, g, i: (0, 0))],
        out_specs=pl.BlockSpec((TQ, gw), lambda b, g, i: (b * nq + i, g)),
        out_shape=jax.ShapeDtypeStruct((n, ng * gw), BF16),
        scratch_shapes=[pltpu.VMEM((L, HEAD_DIM), BF16),
                        pltpu.VMEM((seq // CMP_STRIDE, TQ), F32),
                        pltpu.VMEM((seq // SEL_BLOCK, TQ), jnp.int32),
                        pltpu.VMEM((seq // SEL_BLOCK, L), F32),
                        pltpu.VMEM((seq, L), F32),
                        pltpu.VMEM((8, L), F32),
                        pltpu.VMEM((HEAD_DIM + V_PAD, L), F32),
                        pltpu.VMEM((HEAD_DIM, L), F32)],
        compiler_params=pltpu.CompilerParams(
            dimension_semantics=("arbitrary", "arbitrary", "arbitrary"),
            vmem_limit_bytes=ATT_VMEM_LIMIT),
        name="nsa_attention",
    )(proj, kcmp, vtc, proj, vts, proj, vtw, proj, gt, win, cmpb, ovt)


def _bucket_of_distance():
    d = np.arange(MAX_DISTANCE)
    max_exact = NUM_BUCKETS // 2
    large = max_exact + (np.log(np.maximum(d, 1).astype(np.float32) / max_exact)
                         / math.log(MAX_DISTANCE / max_exact) * (NUM_BUCKETS - max_exact)).astype(np.int32)
    large = np.minimum(large, NUM_BUCKETS - 1)
    return np.where(d < max_exact, d, large)


CMP_NEAR = 16


def _bias_builder_kernel(vw_ref, vc_ref, far_ref, win_ref, cmpb_ref, *, nc):
    TQ, TK = ATT_TQ, ATT_TK
    n_win = (WINDOW + TQ) // TK
    for h in range(NSA_HPG):
        lanes = slice(h * TQ, (h + 1) * TQ)
        for w in range(n_win):
            v = vw_ref[h * n_win + w:h * n_win + w + 1, :]
            r = pltpu.roll(jnp.broadcast_to(v, (TK, 2 * TQ)), 0, 1, stride=1, stride_axis=0)
            win_ref[w * TK:(w + 1) * TK, lanes] = r[:, TQ:2 * TQ]
        win_ref[n_win * TK:(n_win + 1) * TK, lanes] = jnp.broadcast_to(far_ref[:, lanes], (TK, TQ))
        cmpb_ref[0:nc - CMP_NEAR, lanes] = jnp.broadcast_to(far_ref[:, lanes], (nc - CMP_NEAR, TQ))
        vc = vc_ref[h:h + 1, :]
        rc = pltpu.roll(jnp.broadcast_to(vc, (2 * CMP_NEAR, 4 * TQ)), 0, 1,
                        stride=CMP_STRIDE, stride_axis=0)
        cmpb_ref[nc - CMP_NEAR:nc + CMP_NEAR, lanes] = rc[:, 2 * TQ:3 * TQ]
        cmpb_ref[nc + CMP_NEAR:2 * nc, lanes] = jnp.full((nc - CMP_NEAR, TQ), NEG_INF, F32)


def _bias_tiles(rel_bias, seq):
    TQ, TK = ATT_TQ, ATT_TK
    assert TQ == TK and TQ >= MAX_DISTANCE
    assert CMP_STRIDE * CMP_NEAR >= TQ and 2 * TQ >= CMP_STRIDE * 2 * CMP_NEAR
    assert CMP_STRIDE * (CMP_NEAR + 1) - (CMP_BLOCK - 1) >= MAX_DISTANCE - 1
    nc = seq // CMP_STRIDE
    ng, hpg = NSA_GROUPS, NSA_HPG
    n_win = (WINDOW + TQ) // TK
    tb = (rel_bias.astype(F32) * LOG2E)[_bucket_of_distance()].T.reshape(ng, hpg, MAX_DISTANCE)

    def by_distance(dist, valid):
        vals = tb[:, :, np.clip(dist, 0, MAX_DISTANCE - 1)]
        return jnp.where(valid, vals, NEG_INF)

    x = np.arange(2 * TQ)[None, :]
    dw = WINDOW - TK * np.arange(n_win)[:, None] - TQ + x
    vw = by_distance(dw, (dw >= 0) & (dw < WINDOW)).reshape(ng, hpg * n_win, 2 * TQ)
    xc = np.arange(4 * TQ)
    dc = xc - 2 * TQ + CMP_STRIDE * CMP_NEAR - (CMP_BLOCK - 1)
    vc = by_distance(dc, dc >= 0)
    L = hpg * TQ
    far = jnp.broadcast_to(tb[:, :, MAX_DISTANCE - 1:], (ng, hpg, TQ)).reshape(ng, 1, L)

    win, cmpb = pl.pallas_call(
        functools.partial(_bias_builder_kernel, nc=nc),
        grid=(ng,),
        in_specs=[pl.BlockSpec((None,) + vw.shape[1:], lambda g: (g, 0, 0)),
                  pl.BlockSpec((None,) + vc.shape[1:], lambda g: (g, 0, 0)),
                  pl.BlockSpec((None,) + far.shape[1:], lambda g: (g, 0, 0))],
        out_specs=[pl.BlockSpec((None, WINDOW + TQ + TK, L), lambda g: (g, 0, 0)),
                   pl.BlockSpec((None, 2 * nc, L), lambda g: (g, 0, 0))],
        out_shape=[jax.ShapeDtypeStruct((ng, WINDOW + TQ + TK, L), F32),
                   jax.ShapeDtypeStruct((ng, 2 * nc, L), F32)],
        compiler_params=pltpu.CompilerParams(dimension_semantics=("arbitrary",)),
        name="nsa_bias_tiles",
    )(vw, vc, far)
    return win, cmpb


def _overlap_t(seq):
    nc = seq // CMP_STRIDE
    nsel = seq // SEL_BLOCK
    ci = np.arange(nc)[None, :]
    sj = np.arange(nsel)[:, None]
    ov = (CMP_STRIDE * ci < SEL_BLOCK * (sj + 1)) & (CMP_STRIDE * ci + CMP_BLOCK > SEL_BLOCK * sj)
    ov = ov & (ci < nc - 1)
    return jnp.asarray(ov, BF16)


def kernel(x, norm_gains, final_gain, rel_bias, hgrn_lb, hgrn_w_in, hgrn_head_gain, hgrn_w_out,
           nsa_w_in, nsa_pe_k, nsa_pe_v, nsa_phi_k_w1, nsa_phi_k_b1, nsa_phi_k_w2,
           nsa_phi_v_w1, nsa_phi_v_b1, nsa_phi_v_w2, nsa_w_out):
    batch, seq, d = x.shape
    n = batch * seq
    x2d = x.reshape(n, d)
    qscale = HEAD_DIM ** -0.5

    def scale_q_columns(w, scale):
        col = np.ones((1, w.shape[1]), np.float32)
        col[:, :d] = scale
        return (w * col).astype(BF16)

    proj = _norm_proj(x2d, norm_gains[0:1], scale_q_columns(hgrn_w_in[0], qscale))
    o = _hgrn_recurrence(proj, hgrn_lb, hgrn_head_gain[0:1], batch, seq)
    x1, h1 = _out_proj(o, hgrn_w_out[0].astype(BF16), x2d, norm_gains[1:2], final=False)

    kvw = NSA_GROUPS * HEAD_DIM
    gate0 = d + 6 * kvw
    gate1 = gate0 + NSA_HEADS * N_BRANCH
    w_main, wg = _nsa_weight_prep(nsa_w_in, d, gate0, gate1, qscale * LOG2E)
    proj, gt = _proj_gate(h1, w_main, wg)

    blk = {}
    for idx, name in enumerate(("k_c", "v_c", "k_s", "v_s", "k_w", "v_w")):
        blk[name] = (d + idx * kvw) // HEAD_DIM
    gw = NSA_HPG * HEAD_DIM
    cols = dict(blk, q=0, z=gate0 // gw)

    kcmp, vtc, vts, vtw = _kv_prep(
        proj, blk, nsa_pe_k[0], nsa_pe_v[0],
        nsa_phi_k_w1[0].astype(BF16), nsa_phi_v_w1[0].astype(BF16),
        nsa_phi_k_b1[0:1], nsa_phi_v_b1[0:1],
        nsa_phi_k_w2[0].astype(BF16), nsa_phi_v_w2[0].astype(BF16), batch, seq)

    win, cmpb = _bias_tiles(rel_bias, seq)
    o = _nsa_attention(proj, kcmp, vtc, vts, vtw, gt, win, cmpb, _overlap_t(seq), cols, batch, seq)
    out, = _out_proj(o, nsa_w_out[0].astype(BF16), x1, final_gain[None, :], final=True)
    return out.reshape(batch, seq, d)
```

```python
import functools
import math

import numpy as np
import jax
import jax.numpy as jnp
from jax import lax
from jax.experimental import pallas as pl
from jax.experimental.pallas import tpu as pltpu

F32 = jnp.float32
BF16 = jnp.bfloat16

D_MODEL = 2048
RMS_EPS = 1e-6
HEAD_DIM = 128

HGRN_HEADS = D_MODEL // HEAD_DIM
HGRN_CHUNK = 64
HGRN_SUB = 16
HGRN_SAFE_LOG2 = 96.0

NSA_HEADS = D_MODEL // HEAD_DIM
NSA_GROUPS = 4
NSA_HPG = NSA_HEADS // NSA_GROUPS
N_BRANCH = 3
CMP_BLOCK = 32
CMP_STRIDE = 16
SEL_BLOCK = 64
N_SELECT = 16
WINDOW = 512
NUM_BUCKETS = 32
MAX_DISTANCE = 128
NEG_INF = -1e30
FORCE_SCORE = 1e9

ATT_TQ = 256
ATT_TK = 256
V_PAD = 16
ATT_STEP_TILES = 16
CMP_ROWS = 64
WPREP_TAIL = 64
LOG2E = math.log2(math.e)
V7X_VMEM_BYTES = 64 * 1024 * 1024
VMEM_LIMIT = V7X_VMEM_BYTES * 3 // 4
ATT_VMEM_LIMIT = V7X_VMEM_BYTES * 7 // 8

NT_DIMS = (((1,), (1,)), ((), ()))
TN_DIMS = (((0,), (0,)), ((), ()))


def _sigmoid(x):
    return 1.0 / (1.0 + jnp.exp(-x))


def _split3(a):
    hi = a.astype(BF16)
    r1 = a - hi.astype(F32)
    mid = r1.astype(BF16)
    lo = (r1 - mid.astype(F32)).astype(BF16)
    return hi, mid, lo


def _norm_proj_kernel(x_ref, g_ref, w_ref, o_ref, h_scr):
    @pl.when(pl.program_id(1) == 0)
    def _():
        x = x_ref[...]
        ms = jnp.mean(x * x, axis=-1, keepdims=True)
        h_scr[...] = (x * lax.rsqrt(ms + RMS_EPS) * g_ref[...]).astype(BF16)

    o_ref[...] = jnp.dot(h_scr[...], w_ref[...], preferred_element_type=F32).astype(o_ref.dtype)


def _norm_proj(x, gain, w, *, tm=1024, tn=2048):
    n, d = x.shape
    nout = w.shape[1]
    return pl.pallas_call(
        _norm_proj_kernel,
        grid=(n // tm, nout // tn),
        in_specs=[pl.BlockSpec((tm, d), lambda i, j: (i, 0)),
                  pl.BlockSpec((1, d), lambda i, j: (0, 0)),
                  pl.BlockSpec((d, tn), lambda i, j: (0, j))],
        out_specs=pl.BlockSpec((tm, tn), lambda i, j: (i, j)),
        out_shape=jax.ShapeDtypeStruct((n, nout), BF16),
        scratch_shapes=[pltpu.VMEM((tm, d), BF16)],
        compiler_params=pltpu.CompilerParams(
            dimension_semantics=("arbitrary", "arbitrary"), vmem_limit_bytes=VMEM_LIMIT),
        name="norm_proj",
    )(x, gain, w)


def _proj_gate_kernel(h_ref, wt_ref, wgt_ref, o_ref, gt_ref):
    @pl.when(pl.program_id(1) == 0)
    def _():
        gt_ref[...] = lax.dot_general(wgt_ref[...], h_ref[...], NT_DIMS, preferred_element_type=F32)

    o_ref[...] = lax.dot_general(h_ref[...], wt_ref[...], NT_DIMS,
                                 preferred_element_type=F32).astype(o_ref.dtype)


def _proj_gate(h, wt, wgt, *, tm=1024, tn=1792):
    n, d = h.shape
    nout = wt.shape[0]
    ng = wgt.shape[0]
    return pl.pallas_call(
        _proj_gate_kernel,
        grid=(n // tm, nout // tn),
        in_specs=[pl.BlockSpec((tm, d), lambda i, j: (i, 0)),
                  pl.BlockSpec((tn, d), lambda i, j: (j, 0)),
                  pl.BlockSpec((ng, d), lambda i, j: (0, 0))],
        out_specs=[pl.BlockSpec((tm, tn), lambda i, j: (i, j)),
                   pl.BlockSpec((ng, tm), lambda i, j: (0, i))],
        out_shape=[jax.ShapeDtypeStruct((n, nout), BF16),
                   jax.ShapeDtypeStruct((ng, n), F32)],
        compiler_params=pltpu.CompilerParams(
            dimension_semantics=("arbitrary", "arbitrary"), vmem_limit_bytes=VMEM_LIMIT),
        name="proj_gate",
    )(h, wt, wgt)


def _nsa_weight_prep_kernel(wa_ref, wb_ref, o_ref, g_ref, *, q_blocks, shifted_from, shift, scale):
    i = pl.program_id(0)
    tr, d = o_ref.shape

    @pl.when(i < shifted_from)
    def _():
        o_ref[...] = (wa_ref[...] * jnp.where(i < q_blocks, scale, 1.0)).astype(BF16)

    @pl.when(i >= shifted_from)
    def _():
        both = jnp.concatenate([wa_ref[...], wb_ref[0:WPREP_TAIL, :]], axis=0)
        o_ref[...] = both[shift:shift + tr, :].astype(BF16)

    @pl.when(i == shifted_from)
    def _():
        g_ref[...] = jnp.concatenate(
            [wa_ref[0:shift, :], jnp.zeros((HEAD_DIM - shift, d), F32)], axis=0).astype(BF16)


def _nsa_weight_prep(w3, d, gate0, gate1, scale, *, tr=512):
    wt = jnp.swapaxes(w3, 1, 2)
    n_proj = wt.shape[1]
    shift = gate1 - gate0
    n_out = n_proj - shift
    assert gate0 % tr == 0 and n_out % tr == 0 and d % tr == 0
    assert shift % 16 == 0 and shift <= WPREP_TAIL <= tr
    last_src = (n_proj - 1) // tr
    first_b = gate0 // tr + 1
    return pl.pallas_call(
        functools.partial(_nsa_weight_prep_kernel, q_blocks=d // tr, shifted_from=gate0 // tr,
                          shift=shift, scale=scale),
        grid=(n_out // tr,),
        in_specs=[pl.BlockSpec((None, tr, d), lambda i: (0, i, 0)),
                  pl.BlockSpec((None, tr, d),
                               lambda i: (0, jnp.minimum(jnp.maximum(i + 1, first_b), last_src), 0))],
        out_specs=[pl.BlockSpec((tr, d), lambda i: (i, 0)),
                   pl.BlockSpec((HEAD_DIM, d), lambda i: (0, 0))],
        out_shape=[jax.ShapeDtypeStruct((n_out, d), BF16),
                   jax.ShapeDtypeStruct((HEAD_DIM, d), BF16)],
        compiler_params=pltpu.CompilerParams(
            dimension_semantics=("arbitrary",), vmem_limit_bytes=VMEM_LIMIT),
        name="nsa_weight_prep",
    )(wt, wt)


def _out_proj_kernel(o_ref, w_ref, res_ref, g_ref, *out_refs, final):
    xn = res_ref[...] + jnp.dot(o_ref[...], w_ref[...], preferred_element_type=F32)
    ms = jnp.mean(xn * xn, axis=-1, keepdims=True)
    y = xn * lax.rsqrt(ms + RMS_EPS) * g_ref[...]
    if final:
        out_refs[0][...] = y
    else:
        out_refs[0][...] = xn
        out_refs[1][...] = y.astype(BF16)


def _out_proj(o, w, res, gain, *, final, tm=512):
    n, d = res.shape
    row = pl.BlockSpec((tm, d), lambda i: (i, 0))
    if final:
        out_specs = [row]
        out_shape = [jax.ShapeDtypeStruct((n, d), F32)]
    else:
        out_specs = [row, row]
        out_shape = [jax.ShapeDtypeStruct((n, d), F32), jax.ShapeDtypeStruct((n, d), BF16)]
    return pl.pallas_call(
        functools.partial(_out_proj_kernel, final=final),
        grid=(n // tm,),
        in_specs=[pl.BlockSpec((tm, o.shape[1]), lambda i: (i, 0)),
                  pl.BlockSpec(w.shape, lambda i: (0, 0)),
                  row,
                  pl.BlockSpec((1, d), lambda i: (0, 0))],
        out_specs=out_specs,
        out_shape=out_shape,
        compiler_params=pltpu.CompilerParams(
            dimension_semantics=("arbitrary",), vmem_limit_bytes=VMEM_LIMIT),
        name="out_proj_final" if final else "out_proj",
    )(o, w, res, gain)


def _hgrn_kernel(q_ref, f_ref, v_ref, z_ref, lbp_ref, hg_ref, o_ref,
                 st_scr, k_scr, b_scr, qe_scr, oi_scr, u_scr, d_scr, stb_scr, *, n_chunks):
    C, SUB = HGRN_CHUNK, HGRN_SUB

    @pl.when(pl.program_id(2) == 0)
    def _():
        st_scr[...] = jnp.zeros_like(st_scr)

    lbp = lbp_ref[...]
    e = jnp.exp(lbp - jnp.max(lbp, axis=0, keepdims=True))
    lb = e[0:1, :] / jnp.sum(e, axis=0, keepdims=True)
    hg = hg_ref[...]

    rr = lax.broadcasted_iota(jnp.int32, (C, C), 0)
    cc = lax.broadcasted_iota(jnp.int32, (C, C), 1)
    causal = rr >= cc
    tril = causal.astype(BF16)
    sub_c = lax.broadcasted_iota(jnp.int32, (SUB, C), 1)
    row8 = lax.broadcasted_iota(jnp.int32, (8, C), 0)
    col8 = lax.broadcasted_iota(jnp.int32, (8, C), 1)

    def emit(rows, o):
        z = z_ref[rows, :].astype(F32)
        ms = jnp.mean(o * o, axis=-1, keepdims=True)
        o_ref[rows, :] = (o * lax.rsqrt(ms + RMS_EPS) * hg * (z * _sigmoid(z))).astype(o_ref.dtype)

    chunks = [slice(c * C, (c + 1) * C) for c in range(n_chunks)]
    ks, gsplit = [], []
    for rows in chunks:
        sig = _sigmoid(f_ref[rows, :].astype(F32))
        gsplit.append(_split3(jnp.log(lb + (1.0 - lb) * sig)))
        ks.append((1.0 - lb) * (1.0 - sig))
        k_scr[rows, :] = ks[-1]
    b2s = []
    for rows, (ghi, gmid, glo) in zip(chunks, gsplit):
        b = (jnp.dot(tril, ghi, preferred_element_type=F32)
             + jnp.dot(tril, gmid, preferred_element_type=F32)
             + jnp.dot(tril, glo, preferred_element_type=F32))
        b2s.append(b * LOG2E)
        b_scr[rows, :] = b2s[-1]
    span = jnp.zeros((1, HEAD_DIM), F32)
    for b2 in b2s:
        span = jnp.maximum(span, -b2[C - 1:C])
    safe = jnp.max(span) < HGRN_SAFE_LOG2

    qes, kes = [], []
    for rows, k, b2 in zip(chunks, ks, b2s):
        qes.append((q_ref[rows, :].astype(F32) * jnp.exp2(b2)).astype(BF16))
        kes.append(k * jnp.exp2(-b2))
        qe_scr[rows, :] = qes[-1]
    scores = [lax.dot_general(qe, ke.astype(BF16), NT_DIMS, preferred_element_type=F32)
              for qe, ke in zip(qes, kes)]
    vt = v_ref[...].astype(F32).T.astype(BF16)
    zero = jnp.zeros((C, HEAD_DIM), BF16)
    for c, (rows, sc, ke, b2) in enumerate(zip(chunks, scores, kes, b2s)):
        bl = b2[C - 1:C]
        oi_scr[rows, :] = jnp.dot(jnp.where(causal, sc, 0.0).astype(BF16), v_ref[rows, :],
                                  preferred_element_type=F32)
        kd = (ke * jnp.exp2(bl)).astype(BF16)
        pair = slice(c // 2 * 2 * C, (c // 2 + 1) * 2 * C)
        u_scr[c] = jnp.dot(vt[:, pair], jnp.concatenate([kd, zero] if c % 2 == 0 else [zero, kd]),
                           preferred_element_type=F32)
        d_scr[c:c + 1, :] = jnp.exp2(bl)

    @pl.when(safe)
    def _():
        st = st_scr[...]
        for c in range(n_chunks):
            stb_scr[c] = st.astype(BF16)
            st = st * d_scr[c:c + 1, :] + u_scr[c]
        st_scr[...] = st
        carried = [lax.dot_general(qe_scr[rows, :], stb_scr[c], NT_DIMS, preferred_element_type=F32)
                   for c, rows in enumerate(chunks)]
        for rows, oc in zip(chunks, carried):
            emit(rows, oi_scr[rows, :] + oc)

    def robust_chunk(c):
        rows = slice(c * C, (c + 1) * C)
        q = q_ref[rows, :].astype(F32)
        v = v_ref[rows, :]
        k, b2 = k_scr[rows, :], b_scr[rows, :]
        st = st_scr[...]
        o = lax.dot_general((q * jnp.exp2(b2)).astype(BF16), st.astype(BF16), NT_DIMS,
                            preferred_element_type=F32)
        a_rows = []
        for i in range(C // SUB):
            lo, hi = i * SUB, (i + 1) * SUB
            bq, qq, kk = b2[lo:hi], q[lo:hi], k[lo:hi]
            if i > 0:
                r = b2[lo - 1:lo]
                qi = (qq * jnp.exp2(bq - r)).astype(BF16)
                ki = (k * jnp.exp2(jnp.minimum(r - b2, 0.0))).astype(BF16)
                a = lax.dot_general(qi, ki, NT_DIMS, preferred_element_type=F32)
                a = jnp.where(sub_c < lo, a, 0.0)
            else:
                a = jnp.zeros((SUB, C), F32)
            a8 = [a[j * 8:(j + 1) * 8] for j in range(SUB // 8)]
            for s in range(SUB):
                for j in range(s // 8, SUB // 8):
                    grp = slice(j * 8, (j + 1) * 8)
                    ed = qq[grp] * jnp.exp2(bq[grp] - bq[s:s + 1]) * kk[s:s + 1]
                    col = jnp.sum(ed, axis=-1, keepdims=True)
                    hit = col8 == lo + s
                    if j == s // 8:
                        hit = hit & (row8 >= s - j * 8)
                    a8[j] = jnp.where(hit, col, a8[j])
            a_rows.extend(a8)
        scores = jnp.concatenate(a_rows, axis=0)
        o = o + jnp.dot(scores.astype(BF16), v, preferred_element_type=F32)
        bl = b2[C - 1:C]
        st_scr[...] = st * jnp.exp2(bl) + lax.dot_general(
            v, (k * jnp.exp2(bl - b2)).astype(BF16), TN_DIMS, preferred_element_type=F32)
        emit(rows, o)

    @pl.when(jnp.logical_not(safe))
    def _():
        for c in range(n_chunks):
            robust_chunk(c)


def _hgrn_recurrence(proj, lb_param, head_gain, batch, seq, *, t_blk=2048):
    n = proj.shape[0]
    nh = HGRN_HEADS
    t_blk = min(t_blk, seq)
    ns = seq // t_blk

    def sec(k):
        return pl.BlockSpec((t_blk, HEAD_DIM), lambda b, h, s, k=k: (b * ns + s, k * nh + h))

    n_chunks = t_blk // HGRN_CHUNK
    return pl.pallas_call(
        functools.partial(_hgrn_kernel, n_chunks=n_chunks),
        grid=(batch, nh, ns),
        in_specs=[sec(0), sec(1), sec(2), sec(3),
                  pl.BlockSpec((lb_param.shape[0], HEAD_DIM), lambda b, h, s: (0, h)),
                  pl.BlockSpec((1, HEAD_DIM), lambda b, h, s: (0, 0))],
        out_specs=pl.BlockSpec((t_blk, HEAD_DIM), lambda b, h, s: (b * ns + s, h)),
        out_shape=jax.ShapeDtypeStruct((n, nh * HEAD_DIM), BF16),
        scratch_shapes=[pltpu.VMEM((HEAD_DIM, HEAD_DIM), F32),
                        pltpu.VMEM((t_blk, HEAD_DIM), F32),
                        pltpu.VMEM((t_blk, HEAD_DIM), F32),
                        pltpu.VMEM((t_blk, HEAD_DIM), BF16),
                        pltpu.VMEM((t_blk, HEAD_DIM), F32),
                        pltpu.VMEM((n_chunks, HEAD_DIM, HEAD_DIM), F32),
                        pltpu.VMEM((n_chunks, HEAD_DIM), F32),
                        pltpu.VMEM((n_chunks, HEAD_DIM, HEAD_DIM), BF16)],
        compiler_params=pltpu.CompilerParams(
            dimension_semantics=("arbitrary", "arbitrary", "arbitrary")),
        name="hgrn_recurrence",
    )(proj, proj, proj, proj, lb_param, head_gain)


def _kv_prep_kernel(kc_ref, vc_ref, vs_ref, vw_ref, pek_ref, pev_ref, w1k_ref, w1v_ref,
                    b1k_ref, b1v_ref, w2k_ref, w2v_ref, ko_ref, vto_ref, vts_ref, vtw_ref, x_scr,
                    *, nc, ntile):
    half = CMP_BLOCK // 2
    TK = ATT_TK

    def phi(src_ref, pe_ref, w1_ref, b1_ref, w2_ref):
        x_scr[...] = src_ref[...].astype(F32)
        top = jnp.zeros((nc, HEAD_DIM), F32)
        bot = jnp.zeros((nc, HEAD_DIM), F32)
        for p in range(CMP_STRIDE):
            xp = x_scr[pl.ds(p, nc, stride=CMP_STRIDE), :]
            xa = (xp + pe_ref[p:p + 1, :]).astype(BF16)
            xb = (xp + pe_ref[half + p:half + p + 1, :]).astype(BF16)
            top = top + jnp.dot(xa, w1_ref[p * HEAD_DIM:(p + 1) * HEAD_DIM, :],
                                preferred_element_type=F32)
            bot = bot + jnp.dot(xb, w1_ref[(half + p) * HEAD_DIM:(half + p + 1) * HEAD_DIM, :],
                                preferred_element_type=F32)
        hid = top + pltpu.roll(bot, nc - 1, 0) + b1_ref[...]
        act = jax.nn.gelu(hid, approximate=True)
        return jnp.dot(act.astype(BF16), w2_ref[...], preferred_element_type=F32)

    ko_ref[...] = phi(kc_ref, pek_ref, w1k_ref, b1k_ref, w2k_ref).astype(BF16)
    vto_ref[...] = phi(vc_ref, pev_ref, w1v_ref, b1v_ref, w2v_ref).T.astype(BF16)

    ones_row = (lax.broadcasted_iota(jnp.int32, (V_PAD, TK), 0) == 0).astype(BF16)

    for u in range(ntile):
        rows = slice(u * TK, (u + 1) * TK)
        vts_ref[u] = jnp.concatenate([vs_ref[rows, :].astype(F32).T.astype(BF16), ones_row], axis=0)
        vtw_ref[u] = jnp.concatenate([vw_ref[rows, :].astype(F32).T.astype(BF16), ones_row], axis=0)


def _kv_prep(proj, blk, pe_k, pe_v, w1k, w1v, b1k, b1v, w2k, w2v, batch, seq):
    g = NSA_GROUPS
    nc = seq // CMP_STRIDE
    ntile = seq // ATT_TK

    def full(a):
        return pl.BlockSpec(a.shape, lambda b, gg: (0,) * a.ndim)

    def seq_block(col0):
        return pl.BlockSpec((seq, HEAD_DIM), lambda b, gg: (b, col0 + gg))

    def out(*shape):
        spec = pl.BlockSpec((None, None) + shape, lambda b, gg: (b, gg) + (0,) * len(shape))
        return spec, jax.ShapeDtypeStruct((batch, g) + shape, BF16)

    specs, shapes = zip(out(nc, HEAD_DIM), out(HEAD_DIM, nc),
                        out(ntile, HEAD_DIM + V_PAD, ATT_TK), out(ntile, HEAD_DIM + V_PAD, ATT_TK))
    return pl.pallas_call(
        functools.partial(_kv_prep_kernel, nc=nc, ntile=ntile),
        grid=(batch, g),
        in_specs=[seq_block(blk["k_c"]), seq_block(blk["v_c"]), seq_block(blk["v_s"]),
                  seq_block(blk["v_w"]),
                  full(pe_k), full(pe_v), full(w1k), full(w1v), full(b1k), full(b1v),
                  full(w2k), full(w2v)],
        out_specs=list(specs),
        out_shape=list(shapes),
        scratch_shapes=[pltpu.VMEM((seq, HEAD_DIM), F32)],
        compiler_params=pltpu.CompilerParams(dimension_semantics=("arbitrary", "arbitrary")),
        name="nsa_kv_prep",
    )(proj, proj, proj, proj, pe_k, pe_v, w1k, w1v, b1k, b1v, w2k, w2v)


def _nsa_attn_kernel(q_ref, kc_ref, vtc_ref, ks_ref, vts_ref, kw_ref, vtw_ref, z_ref, gt_ref,
                     win_ref, cmpb_ref, ovt_ref, o_ref,
                     q_scr, psum_scr, rank_scr, selm_scr, s_scr, m8_scr, acc_scr, tot_scr,
                     *, seq):
    TQ, TK = ATT_TQ, ATT_TK
    L = NSA_HPG * TQ
    nc = seq // CMP_STRIDE
    nsel = seq // SEL_BLOCK
    blk_per_tile = TK // SEL_BLOCK
    n_win = (WINDOW + TQ) // TK
    g = pl.program_id(1)
    i = pl.program_id(2)
    t0 = i * TQ

    for h in range(NSA_HPG):
        q_scr[h * TQ:(h + 1) * TQ, :] = q_ref[:, h * HEAD_DIM:(h + 1) * HEAD_DIM]

    def gate_row(c):
        rows = [(g * NSA_HPG + h) * N_BRANCH + c for h in range(NSA_HPG)]
        return jnp.concatenate([_sigmoid(gt_ref[pl.ds(r, 1), :]) for r in rows], axis=1)

    cmp_row0 = pl.multiple_of(nc - i * (TQ // CMP_STRIDE), 8)

    def cmp_branch(rows):
        s = lax.dot_general(kc_ref[0:rows, :], q_scr[...], NT_DIMS, preferred_element_type=F32)
        s = s + cmpb_ref[pl.ds(cmp_row0, rows), :]
        mc = jnp.max(s, axis=0, keepdims=True)
        p = jnp.exp2(s - mc)
        inv = jnp.where(mc > 0.1 * NEG_INF, 1.0 / jnp.sum(p, axis=0, keepdims=True), 0.0)
        p = p * inv
        psum = p[:, 0:TQ]
        for h in range(1, NSA_HPG):
            psum = psum + p[:, h * TQ:(h + 1) * TQ]
        if rows < nc:
            psum = jnp.concatenate([psum, jnp.zeros((nc - rows, TQ), F32)], axis=0)
        psum_scr[...] = psum
        oc = jnp.dot(vtc_ref[:, 0:rows], p.astype(BF16), preferred_element_type=F32)
        tot_scr[...] = oc * gate_row(0)

    qblk_per_rows = CMP_ROWS // (TQ // CMP_STRIDE)
    for k in range(1, nc // CMP_ROWS + 1):
        pl.when(i // qblk_per_rows + 1 == k)(functools.partial(cmp_branch, k * CMP_ROWS))

    ovt = ovt_ref[...]
    phi, pmid, plo = _split3(psum_scr[...])
    imp = (jnp.dot(ovt, phi, preferred_element_type=F32)
           + jnp.dot(ovt, pmid, preferred_element_type=F32)
           + jnp.dot(ovt, plo, preferred_element_type=F32))

    jrow = lax.broadcasted_iota(jnp.int32, (nsel, TQ), 0)
    tcol = t0 + lax.broadcasted_iota(jnp.int32, (nsel, TQ), 1)
    cur = lax.shift_right_arithmetic(tcol, int(math.log2(SEL_BLOCK)))
    forced = (jrow == 0) | (jrow == cur) | (jrow == cur - 1)
    visible = jrow * SEL_BLOCK <= tcol
    score = jnp.where(forced, FORCE_SCORE, jnp.where(visible, imp, NEG_INF))
    row8 = lax.broadcasted_iota(jnp.int32, (8, TQ), 0)
    score8 = [score[r * 8:(r + 1) * 8] for r in range(nsel // 8)]
    rank_scr[...] = jnp.zeros_like(rank_scr)
    n_visible = (i + 1) * (TQ // SEL_BLOCK)

    def count_ahead(grp):
        cnt = [jnp.zeros((8, TQ), jnp.int32) for _ in score8]
        for jp in range(grp * 8, grp * 8 + 8):
            row = score[jp:jp + 1, :]
            for r, sc in enumerate(score8):
                if r * 8 > jp:
                    ahead = row >= sc
                elif r * 8 + 7 < jp:
                    ahead = row > sc
                else:
                    ahead = (row > sc) | ((row == sc) & (row8 > jp - r * 8))
                cnt[r] = cnt[r] + ahead.astype(jnp.int32)
        for r, c in enumerate(cnt):
            rank_scr[r * 8:(r + 1) * 8, :] += c

    for grp in range(nsel // 8):
        pl.when(grp * 8 < n_visible)(functools.partial(count_ahead, grp))
    rank = rank_scr[...]
    selm = jnp.where(rank < min(N_SELECT, nsel), 0.0, NEG_INF)
    selm = jnp.concatenate([selm] * NSA_HPG, axis=1)
    selm_scr[...] = selm

    def step_rows(u0, nt):
        return pl.ds(pl.multiple_of(u0 * TK, TK), nt * TK)

    def logits_step(k_ref, u0, nt, win_row0):
        s = lax.dot_general(k_ref[step_rows(u0, nt), :], q_scr[...], NT_DIMS,
                            preferred_element_type=F32)
        if win_row0 is not None:
            s = s + win_ref[win_row0:win_row0 + nt * TK, :]
        else:
            parts = []
            for j in range(nt):
                near = u0 + j - (i - (n_win - 2))
                row0 = pl.multiple_of(jnp.where(near < 0, n_win, near + n_win - 2) * TK, TK)
                for r in range(blk_per_tile):
                    lo = j * TK + r * SEL_BLOCK
                    parts.append(s[lo:lo + SEL_BLOCK]
                                 + win_ref[pl.ds(row0 + r * SEL_BLOCK, SEL_BLOCK), :]
                                 + selm_scr[pl.ds((u0 + j) * blk_per_tile + r, 1), :])
            s = jnp.concatenate(parts, axis=0)
        s_scr[step_rows(u0, nt), :] = s
        m8_scr[...] = jnp.maximum(m8_scr[...], jnp.max(s.reshape(nt * TK // 8, 8, L), axis=0))

    def pv_step(vt_ref, m, u0, nt):
        p = jnp.exp2((s_scr[step_rows(u0, nt), :] - m).astype(BF16))
        vt = jnp.concatenate([vt_ref[u0 + j] for j in range(nt)], axis=1)
        acc_scr[...] += jnp.dot(vt, p, preferred_element_type=F32)

    def over_tiles(u_lo, n, step):
        big = ATT_STEP_TILES
        nbig = lax.shift_right_logical(n, int(math.log2(big)))

        def body(j, carry):
            step(u_lo + big * j, big)
            return carry
        lax.fori_loop(0, nbig, body, 0)
        u = u_lo + big * nbig
        nt = big // 2
        while nt >= 1:
            pl.when((n & nt) != 0)(functools.partial(step, u, nt))
            u = u + (n & nt)
            nt //= 2

    def near_diagonal(step):
        for nt in range(1, n_win + 1):
            cond = (i == nt - 1) if nt < n_win else (i >= nt - 1)
            pl.when(cond)(functools.partial(step, i - (nt - 1), nt, (n_win - nt) * TK))

    def finish(c):
        w = gate_row(c) * (1.0 / acc_scr[HEAD_DIM:HEAD_DIM + 1, :])
        tot_scr[...] = tot_scr[...] + acc_scr[0:HEAD_DIM, :] * w

    def reset():
        m8_scr[...] = jnp.full_like(m8_scr, NEG_INF)
        acc_scr[...] = jnp.zeros_like(acc_scr)

    reset()
    over_tiles(0, i + 1, lambda u0, nt: logits_step(ks_ref, u0, nt, None))
    m_sel = jnp.max(m8_scr[...], axis=0, keepdims=True)
    over_tiles(0, i + 1, functools.partial(pv_step, vts_ref, m_sel))
    finish(1)

    reset()
    near_diagonal(lambda u0, nt, row0: logits_step(kw_ref, u0, nt, row0))
    m_win = jnp.max(m8_scr[...], axis=0, keepdims=True)
    near_diagonal(lambda u0, nt, row0: pv_step(vtw_ref, m_win, u0, nt))
    finish(2)

    for h in range(NSA_HPG):
        cols = slice(h * HEAD_DIM, (h + 1) * HEAD_DIM)
        z = z_ref[:, cols].astype(F32)
        o_ref[:, cols] = (tot_scr[:, h * TQ:(h + 1) * TQ].T * (z * _sigmoid(z))).astype(o_ref.dtype)


def _nsa_attention(proj, kcmp, vtc, vts, vtw, gt, win, cmpb, ovt, cols, batch, seq):
    TQ, TK = ATT_TQ, ATT_TK
    n = proj.shape[0]
    ng = NSA_GROUPS
    nq = seq // TQ
    L = NSA_HPG * TQ
    gw = NSA_HPG * HEAD_DIM

    def seq_block(col0):
        return pl.BlockSpec((seq, HEAD_DIM), lambda b, g, i: (b, col0 + g))

    def per_bg(a):
        return pl.BlockSpec((None, None) + a.shape[2:], lambda b, g, i: (b, g) + (0,) * (a.ndim - 2))

    def per_g(a):
        return pl.BlockSpec((None,) + a.shape[1:], lambda b, g, i: (g,) + (0,) * (a.ndim - 1))

    return pl.pallas_call(
        functools.partial(_nsa_attn_kernel, seq=seq),
        grid=(batch, ng, nq),
        in_specs=[pl.BlockSpec((TQ, gw), lambda b, g, i: (b * nq + i, cols["q"] + g)),
                  per_bg(kcmp), per_bg(vtc),
                  seq_block(cols["k_s"]), per_bg(vts),
                  seq_block(cols["k_w"]), per_bg(vtw),
                  pl.BlockSpec((TQ, gw), lambda b, g, i: (b * nq + i, cols["z"] + g)),
                  pl.BlockSpec((gt.shape[0], TQ), lambda b, g, i: (0, b * nq + i)),
                  per_g(win), per_g(cmpb),
                  pl.BlockSpec(ovt.shape, lambda b, g, i: (0, 0))],
        out_specs=pl.BlockSpec((TQ, gw), lambda b, g, i: (b * nq + i, g)),
        out_shape=jax.ShapeDtypeStruct((n, ng * gw), BF16),
        scratch_shapes=[pltpu.VMEM((L, HEAD_DIM), BF16),
                        pltpu.VMEM((seq // CMP_STRIDE, TQ), F32),
                        pltpu.VMEM((seq // SEL_BLOCK, TQ), jnp.int32),
                        pltpu.VMEM((seq // SEL_BLOCK, L), F32),
                        pltpu.VMEM((seq, L), F32),
                        pltpu.VMEM((8, L), F32),
                        pltpu.VMEM((HEAD_DIM + V_PAD, L), F32),
                        pltpu.VMEM((HEAD_DIM, L), F32)],
        compiler_params=pltpu.CompilerParams(
            dimension_semantics=("arbitrary", "arbitrary", "arbitrary"),
            vmem_limit_bytes=ATT_VMEM_LIMIT),
        name="nsa_attention",
    )(proj, kcmp, vtc, proj, vts, proj, vtw, proj, gt, win, cmpb, ovt)


def _bucket_of_distance():
    d = np.arange(MAX_DISTANCE)
    max_exact = NUM_BUCKETS // 2
    large = max_exact + (np.log(np.maximum(d, 1).astype(np.float32) / max_exact)
                         / math.log(MAX_DISTANCE / max_exact) * (NUM_BUCKETS - max_exact)).astype(np.int32)
    large = np.minimum(large, NUM_BUCKETS - 1)
    return np.where(d < max_exact, d, large)


CMP_NEAR = 16


def _bias_builder_kernel(vw_ref, vc_ref, far_ref, win_ref, cmpb_ref, *, nc):
    TQ, TK = ATT_TQ, ATT_TK
    n_win = (WINDOW + TQ) // TK
    for h in range(NSA_HPG):
        lanes = slice(h * TQ, (h + 1) * TQ)
        for w in range(n_win):
            v = vw_ref[h * n_win + w:h * n_win + w + 1, :]
            r = pltpu.roll(jnp.broadcast_to(v, (TK, 2 * TQ)), 0, 1, stride=1, stride_axis=0)
            win_ref[w * TK:(w + 1) * TK, lanes] = r[:, TQ:2 * TQ]
        win_ref[n_win * TK:(n_win + 1) * TK, lanes] = jnp.broadcast_to(far_ref[:, lanes], (TK, TQ))
        cmpb_ref[0:nc - CMP_NEAR, lanes] = jnp.broadcast_to(far_ref[:, lanes], (nc - CMP_NEAR, TQ))
        vc = vc_ref[h:h + 1, :]
        rc = pltpu.roll(jnp.broadcast_to(vc, (2 * CMP_NEAR, 4 * TQ)), 0, 1,
                        stride=CMP_STRIDE, stride_axis=0)
        cmpb_ref[nc - CMP_NEAR:nc + CMP_NEAR, lanes] = rc[:, 2 * TQ:3 * TQ]
        cmpb_ref[nc + CMP_NEAR:2 * nc, lanes] = jnp.full((nc - CMP_NEAR, TQ), NEG_INF, F32)


def _bias_tiles(rel_bias, seq):
    TQ, TK = ATT_TQ, ATT_TK
    assert TQ == TK and TQ >= MAX_DISTANCE
    assert CMP_STRIDE * CMP_NEAR >= TQ and 2 * TQ >= CMP_STRIDE * 2 * CMP_NEAR
    assert CMP_STRIDE * (CMP_NEAR + 1) - (CMP_BLOCK - 1) >= MAX_DISTANCE - 1
    nc = seq // CMP_STRIDE
    ng, hpg = NSA_GROUPS, NSA_HPG
    n_win = (WINDOW + TQ) // TK
    tb = (rel_bias.astype(F32) * LOG2E)[_bucket_of_distance()].T.reshape(ng, hpg, MAX_DISTANCE)

    def by_distance(dist, valid):
        vals = tb[:, :, np.clip(dist, 0, MAX_DISTANCE - 1)]
        return jnp.where(valid, vals, NEG_INF)

    x = np.arange(2 * TQ)[None, :]
    dw = WINDOW - TK * np.arange(n_win)[:, None] - TQ + x
    vw = by_distance(dw, (dw >= 0) & (dw < WINDOW)).reshape(ng, hpg * n_win, 2 * TQ)
    xc = np.arange(4 * TQ)
    dc = xc - 2 * TQ + CMP_STRIDE * CMP_NEAR - (CMP_BLOCK - 1)
    vc = by_distance(dc, dc >= 0)
    L = hpg * TQ
    far = jnp.broadcast_to(tb[:, :, MAX_DISTANCE - 1:], (ng, hpg, TQ)).reshape(ng, 1, L)

    win, cmpb = pl.pallas_call(
        functools.partial(_bias_builder_kernel, nc=nc),
        grid=(ng,),
        in_specs=[pl.BlockSpec((None,) + vw.shape[1:], lambda g: (g, 0, 0)),
                  pl.BlockSpec((None,) + vc.shape[1:], lambda g: (g, 0, 0)),
                  pl.BlockSpec((None,) + far.shape[1:], lambda g: (g, 0, 0))],
        out_specs=[pl.BlockSpec((None, WINDOW + TQ + TK, L), lambda g: (g, 0, 0)),
                   pl.BlockSpec((None, 2 * nc, L), lambda g: (g, 0, 0))],
        out_shape=[jax.ShapeDtypeStruct((ng, WINDOW + TQ + TK, L), F32),
                   jax.ShapeDtypeStruct((ng, 2 * nc, L), F32)],
        compiler_params=pltpu.CompilerParams(dimension_semantics=("arbitrary",)),
        name="nsa_bias_tiles",
    )(vw, vc, far)
    return win, cmpb


def _overlap_t(seq):
    nc = seq // CMP_STRIDE
    nsel = seq // SEL_BLOCK
    ci = np.arange(nc)[None, :]
    sj = np.arange(nsel)[:, None]
    ov = (CMP_STRIDE * ci < SEL_BLOCK * (sj + 1)) & (CMP_STRIDE * ci + CMP_BLOCK > SEL_BLOCK * sj)
    ov = ov & (ci < nc - 1)
    return jnp.asarray(ov, BF16)


def kernel(x, norm_gains, final_gain, rel_bias, hgrn_lb, hgrn_w_in, hgrn_head_gain, hgrn_w_out,
           nsa_w_in, nsa_pe_k, nsa_pe_v, nsa_phi_k_w1, nsa_phi_k_b1, nsa_phi_k_w2,
           nsa_phi_v_w1, nsa_phi_v_b1, nsa_phi_v_w2, nsa_w_out):
    batch, seq, d = x.shape
    n = batch * seq
    x2d = x.reshape(n, d)
    qscale = HEAD_DIM ** -0.5

    def scale_q_columns(w, scale):
        col = np.ones((1, w.shape[1]), np.float32)
        col[:, :d] = scale
        return (w * col).astype(BF16)

    proj = _norm_proj(x2d, norm_gains[0:1], scale_q_columns(hgrn_w_in[0], qscale))
    o = _hgrn_recurrence(proj, hgrn_lb, hgrn_head_gain[0:1], batch, seq)
    x1, h1 = _out_proj(o, hgrn_w_out[0].astype(BF16), x2d, norm_gains[1:2], final=False)

    kvw = NSA_GROUPS * HEAD_DIM
    gate0 = d + 6 * kvw
    gate1 = gate0 + NSA_HEADS * N_BRANCH
    w_main, wg = _nsa_weight_prep(nsa_w_in, d, gate0, gate1, qscale * LOG2E)
    proj, gt = _proj_gate(h1, w_main, wg)

    blk = {}
    for idx, name in enumerate(("k_c", "v_c", "k_s", "v_s", "k_w", "v_w")):
        blk[name] = (d + idx * kvw) // HEAD_DIM
    gw = NSA_HPG * HEAD_DIM
    cols = dict(blk, q=0, z=gate0 // gw)

    kcmp, vtc, vts, vtw = _kv_prep(
        proj, blk, nsa_pe_k[0], nsa_pe_v[0],
        nsa_phi_k_w1[0].astype(BF16), nsa_phi_v_w1[0].astype(BF16),
        nsa_phi_k_b1[0:1], nsa_phi_v_b1[0:1],
        nsa_phi_k_w2[0].astype(BF16), nsa_phi_v_w2[0].astype(BF16), batch, seq)

    win, cmpb = _bias_tiles(rel_bias, seq)
    o = _nsa_attention(proj, kcmp, vtc, vts, vtw, gt, win, cmpb, _overlap_t(seq), cols, batch, seq)
    out, = _out_proj(o, nsa_w_out[0].astype(BF16), x1, final_gain[None, :], final=True)
    return out.reshape(batch, seq, d)
```

```python
import functools
import math

import numpy as np
import jax
import jax.numpy as jnp
from jax import lax
from jax.experimental import pallas as pl
from jax.experimental.pallas import tpu as pltpu

F32 = jnp.float32
BF16 = jnp.bfloat16

D_MODEL = 2048
RMS_EPS = 1e-6
HEAD_DIM = 128

HGRN_HEADS = D_MODEL // HEAD_DIM
HGRN_CHUNK = 64
HGRN_SUB = 16
HGRN_SAFE_LOG2 = 96.0

NSA_HEADS = D_MODEL // HEAD_DIM
NSA_GROUPS = 4
NSA_HPG = NSA_HEADS // NSA_GROUPS
N_BRANCH = 3
CMP_BLOCK = 32
CMP_STRIDE = 16
SEL_BLOCK = 64
N_SELECT = 16
WINDOW = 512
NUM_BUCKETS = 32
MAX_DISTANCE = 128
NEG_INF = -1e30
FORCE_SCORE = 1e9

ATT_TQ = 256
ATT_TK = 256
V_PAD = 16
ATT_STEP_TILES = 16
CMP_ROWS = 64
WPREP_TAIL = 64
LOG2E = math.log2(math.e)
V7X_VMEM_BYTES = 64 * 1024 * 1024
VMEM_LIMIT = V7X_VMEM_BYTES * 3 // 4
ATT_VMEM_LIMIT = V7X_VMEM_BYTES * 7 // 8

NT_DIMS = (((1,), (1,)), ((), ()))
TN_DIMS = (((0,), (0,)), ((), ()))


def _sigmoid(x):
    return 1.0 / (1.0 + jnp.exp(-x))


def _split3(a):
    hi = a.astype(BF16)
    r1 = a - hi.astype(F32)
    mid = r1.astype(BF16)
    lo = (r1 - mid.astype(F32)).astype(BF16)
    return hi, mid, lo


def _norm_proj_kernel(x_ref, g_ref, w_ref, o_ref, h_scr):
    @pl.when(pl.program_id(1) == 0)
    def _():
        x = x_ref[...]
        ms = jnp.mean(x * x, axis=-1, keepdims=True)
        h_scr[...] = (x * lax.rsqrt(ms + RMS_EPS) * g_ref[...]).astype(BF16)

    o_ref[...] = jnp.dot(h_scr[...], w_ref[...], preferred_element_type=F32).astype(o_ref.dtype)


def _norm_proj(x, gain, w, *, tm=1024, tn=2048):
    n, d = x.shape
    nout = w.shape[1]
    return pl.pallas_call(
        _norm_proj_kernel,
        grid=(n // tm, nout // tn),
        in_specs=[pl.BlockSpec((tm, d), lambda i, j: (i, 0)),
                  pl.BlockSpec((1, d), lambda i, j: (0, 0)),
                  pl.BlockSpec((d, tn), lambda i, j: (0, j))],
        out_specs=pl.BlockSpec((tm, tn), lambda i, j: (i, j)),
        out_shape=jax.ShapeDtypeStruct((n, nout), BF16),
        scratch_shapes=[pltpu.VMEM((tm, d), BF16)],
        compiler_params=pltpu.CompilerParams(
            dimension_semantics=("arbitrary", "arbitrary"), vmem_limit_bytes=VMEM_LIMIT),
        name="norm_proj",
    )(x, gain, w)


def _proj_gate_kernel(h_ref, wt_ref, wgt_ref, o_ref, gt_ref):
    @pl.when(pl.program_id(1) == 0)
    def _():
        gt_ref[...] = lax.dot_general(wgt_ref[...], h_ref[...], NT_DIMS, preferred_element_type=F32)

    o_ref[...] = lax.dot_general(h_ref[...], wt_ref[...], NT_DIMS,
                                 preferred_element_type=F32).astype(o_ref.dtype)


def _proj_gate(h, wt, wgt, *, tm=1024, tn=1792):
    n, d = h.shape
    nout = wt.shape[0]
    ng = wgt.shape[0]
    return pl.pallas_call(
        _proj_gate_kernel,
        grid=(n // tm, nout // tn),
        in_specs=[pl.BlockSpec((tm, d), lambda i, j: (i, 0)),
                  pl.BlockSpec((tn, d), lambda i, j: (j, 0)),
                  pl.BlockSpec((ng, d), lambda i, j: (0, 0))],
        out_specs=[pl.BlockSpec((tm, tn), lambda i, j: (i, j)),
                   pl.BlockSpec((ng, tm), lambda i, j: (0, i))],
        out_shape=[jax.ShapeDtypeStruct((n, nout), BF16),
                   jax.ShapeDtypeStruct((ng, n), F32)],
        compiler_params=pltpu.CompilerParams(
            dimension_semantics=("arbitrary", "arbitrary"), vmem_limit_bytes=VMEM_LIMIT),
        name="proj_gate",
    )(h, wt, wgt)


def _nsa_weight_prep_kernel(wa_ref, wb_ref, o_ref, g_ref, *, q_blocks, shifted_from, shift, scale):
    i = pl.program_id(0)
    tr, d = o_ref.shape

    @pl.when(i < shifted_from)
    def _():
        o_ref[...] = (wa_ref[...] * jnp.where(i < q_blocks, scale, 1.0)).astype(BF16)

    @pl.when(i >= shifted_from)
    def _():
        both = jnp.concatenate([wa_ref[...], wb_ref[0:WPREP_TAIL, :]], axis=0)
        o_ref[...] = both[shift:shift + tr, :].astype(BF16)

    @pl.when(i == shifted_from)
    def _():
        g_ref[...] = jnp.concatenate(
            [wa_ref[0:shift, :], jnp.zeros((HEAD_DIM - shift, d), F32)], axis=0).astype(BF16)


def _nsa_weight_prep(w3, d, gate0, gate1, scale, *, tr=512):
    wt = jnp.swapaxes(w3, 1, 2)
    n_proj = wt.shape[1]
    shift = gate1 - gate0
    n_out = n_proj - shift
    assert gate0 % tr == 0 and n_out % tr == 0 and d % tr == 0
    assert shift % 16 == 0 and shift <= WPREP_TAIL <= tr
    last_src = (n_proj - 1) // tr
    first_b = gate0 // tr + 1
    return pl.pallas_call(
        functools.partial(_nsa_weight_prep_kernel, q_blocks=d // tr, shifted_from=gate0 // tr,
                          shift=shift, scale=scale),
        grid=(n_out // tr,),
        in_specs=[pl.BlockSpec((None, tr, d), lambda i: (0, i, 0)),
                  pl.BlockSpec((None, tr, d),
                               lambda i: (0, jnp.minimum(jnp.maximum(i + 1, first_b), last_src), 0))],
        out_specs=[pl.BlockSpec((tr, d), lambda i: (i, 0)),
                   pl.BlockSpec((HEAD_DIM, d), lambda i: (0, 0))],
        out_shape=[jax.ShapeDtypeStruct((n_out, d), BF16),
                   jax.ShapeDtypeStruct((HEAD_DIM, d), BF16)],
        compiler_params=pltpu.CompilerParams(
            dimension_semantics=("arbitrary",), vmem_limit_bytes=VMEM_LIMIT),
        name="nsa_weight_prep",
    )(wt, wt)


def _out_proj_kernel(o_ref, w_ref, res_ref, g_ref, *out_refs, final):
    xn = res_ref[...] + jnp.dot(o_ref[...], w_ref[...], preferred_element_type=F32)
    ms = jnp.mean(xn * xn, axis=-1, keepdims=True)
    y = xn * lax.rsqrt(ms + RMS_EPS) * g_ref[...]
    if final:
        out_refs[0][...] = y
    else:
        out_refs[0][...] = xn
        out_refs[1][...] = y.astype(BF16)


def _out_proj(o, w, res, gain, *, final, tm=512):
    n, d = res.shape
    row = pl.BlockSpec((tm, d), lambda i: (i, 0))
    if final:
        out_specs = [row]
        out_shape = [jax.ShapeDtypeStruct((n, d), F32)]
    else:
        out_specs = [row, row]
        out_shape = [jax.ShapeDtypeStruct((n, d), F32), jax.ShapeDtypeStruct((n, d), BF16)]
    return pl.pallas_call(
        functools.partial(_out_proj_kernel, final=final),
        grid=(n // tm,),
        in_specs=[pl.BlockSpec((tm, o.shape[1]), lambda i: (i, 0)),
                  pl.BlockSpec(w.shape, lambda i: (0, 0)),
                  row,
                  pl.BlockSpec((1, d), lambda i: (0, 0))],
        out_specs=out_specs,
        out_shape=out_shape,
        compiler_params=pltpu.CompilerParams(
            dimension_semantics=("arbitrary",), vmem_limit_bytes=VMEM_LIMIT),
        name="out_proj_final" if final else "out_proj",
    )(o, w, res, gain)


def _hgrn_kernel(q_ref, f_ref, v_ref, z_ref, lbp_ref, hg_ref, o_ref,
                 st_scr, k_scr, b_scr, qe_scr, oi_scr, u_scr, d_scr, stb_scr, *, n_chunks):
    C, SUB = HGRN_CHUNK, HGRN_SUB

    @pl.when(pl.program_id(2) == 0)
    def _():
        st_scr[...] = jnp.zeros_like(st_scr)

    lbp = lbp_ref[...]
    e = jnp.exp(lbp - jnp.max(lbp, axis=0, keepdims=True))
    lb = e[0:1, :] / jnp.sum(e, axis=0, keepdims=True)
    hg = hg_ref[...]

    rr = lax.broadcasted_iota(jnp.int32, (C, C), 0)
    cc = lax.broadcasted_iota(jnp.int32, (C, C), 1)
    causal = rr >= cc
    tril = causal.astype(BF16)
    sub_c = lax.broadcasted_iota(jnp.int32, (SUB, C), 1)
    row8 = lax.broadcasted_iota(jnp.int32, (8, C), 0)
    col8 = lax.broadcasted_iota(jnp.int32, (8, C), 1)

    def emit(rows, o):
        z = z_ref[rows, :].astype(F32)
        ms = jnp.mean(o * o, axis=-1, keepdims=True)
        o_ref[rows, :] = (o * lax.rsqrt(ms + RMS_EPS) * hg * (z * _sigmoid(z))).astype(o_ref.dtype)

    chunks = [slice(c * C, (c + 1) * C) for c in range(n_chunks)]
    ks, gsplit = [], []
    for rows in chunks:
        sig = _sigmoid(f_ref[rows, :].astype(F32))
        gsplit.append(_split3(jnp.log(lb + (1.0 - lb) * sig)))
        ks.append((1.0 - lb) * (1.0 - sig))
        k_scr[rows, :] = ks[-1]
    b2s = []
    for rows, (ghi, gmid, glo) in zip(chunks, gsplit):
        b = (jnp.dot(tril, ghi, preferred_element_type=F32)
             + jnp.dot(tril, gmid, preferred_element_type=F32)
             + jnp.dot(tril, glo, preferred_element_type=F32))
        b2s.append(b * LOG2E)
        b_scr[rows, :] = b2s[-1]
    span = jnp.zeros((1, HEAD_DIM), F32)
    for b2 in b2s:
        span = jnp.maximum(span, -b2[C - 1:C])
    safe = jnp.max(span) < HGRN_SAFE_LOG2

    qes, kes = [], []
    for rows, k, b2 in zip(chunks, ks, b2s):
        qes.append((q_ref[rows, :].astype(F32) * jnp.exp2(b2)).astype(BF16))
        kes.append(k * jnp.exp2(-b2))
        qe_scr[rows, :] = qes[-1]
    scores = [lax.dot_general(qe, ke.astype(BF16), NT_DIMS, preferred_element_type=F32)
              for qe, ke in zip(qes, kes)]
    vt = v_ref[...].astype(F32).T.astype(BF16)
    zero = jnp.zeros((C, HEAD_DIM), BF16)
    for c, (rows, sc, ke, b2) in enumerate(zip(chunks, scores, kes, b2s)):
        bl = b2[C - 1:C]
        oi_scr[rows, :] = jnp.dot(jnp.where(causal, sc, 0.0).astype(BF16), v_ref[rows, :],
                                  preferred_element_type=F32)
        kd = (ke * jnp.exp2(bl)).astype(BF16)
        pair = slice(c // 2 * 2 * C, (c // 2 + 1) * 2 * C)
        u_scr[c] = jnp.dot(vt[:, pair], jnp.concatenate([kd, zero] if c % 2 == 0 else [zero, kd]),
                           preferred_element_type=F32)
        d_scr[c:c + 1, :] = jnp.exp2(bl)

    @pl.when(safe)
    def _():
        st = st_scr[...]
        for c in range(n_chunks):
            stb_scr[c] = st.astype(BF16)
            st = st * d_scr[c:c + 1, :] + u_scr[c]
        st_scr[...] = st
        carried = [lax.dot_general(qe_scr[rows, :], stb_scr[c], NT_DIMS, preferred_element_type=F32)
                   for c, rows in enumerate(chunks)]
        for rows, oc in zip(chunks, carried):
            emit(rows, oi_scr[rows, :] + oc)

    def robust_chunk(c):
        rows = slice(c * C, (c + 1) * C)
        q = q_ref[rows, :].astype(F32)
        v = v_ref[rows, :]
        k, b2 = k_scr[rows, :], b_scr[rows, :]
        st = st_scr[...]
        o = lax.dot_general((q * jnp.exp2(b2)).astype(BF16), st.astype(BF16), NT_DIMS,
                            preferred_element_type=F32)
        a_rows = []
        for i in range(C // SUB):
            lo, hi = i * SUB, (i + 1) * SUB
            bq, qq, kk = b2[lo:hi], q[lo:hi], k[lo:hi]
            if i > 0:
                r = b2[lo - 1:lo]
                qi = (qq * jnp.exp2(bq - r)).astype(BF16)
                ki = (k * jnp.exp2(jnp.minimum(r - b2, 0.0))).astype(BF16)
                a = lax.dot_general(qi, ki, NT_DIMS, preferred_element_type=F32)
                a = jnp.where(sub_c < lo, a, 0.0)
            else:
                a = jnp.zeros((SUB, C), F32)
            a8 = [a[j * 8:(j + 1) * 8] for j in range(SUB // 8)]
            for s in range(SUB):
                for j in range(s // 8, SUB // 8):
                    grp = slice(j * 8, (j + 1) * 8)
                    ed = qq[grp] * jnp.exp2(bq[grp] - bq[s:s + 1]) * kk[s:s + 1]
                    col = jnp.sum(ed, axis=-1, keepdims=True)
                    hit = col8 == lo + s
                    if j == s // 8:
                        hit = hit & (row8 >= s - j * 8)
                    a8[j] = jnp.where(hit, col, a8[j])
            a_rows.extend(a8)
        scores = jnp.concatenate(a_rows, axis=0)
        o = o + jnp.dot(scores.astype(BF16), v, preferred_element_type=F32)
        bl = b2[C - 1:C]
        st_scr[...] = st * jnp.exp2(bl) + lax.dot_general(
            v, (k * jnp.exp2(bl - b2)).astype(BF16), TN_DIMS, preferred_element_type=F32)
        emit(rows, o)

    @pl.when(jnp.logical_not(safe))
    def _():
        for c in range(n_chunks):
            robust_chunk(c)


def _hgrn_recurrence(proj, lb_param, head_gain, batch, seq, *, t_blk=2048):
    n = proj.shape[0]
    nh = HGRN_HEADS
    t_blk = min(t_blk, seq)
    ns = seq // t_blk

    def sec(k):
        return pl.BlockSpec((t_blk, HEAD_DIM), lambda b, h, s, k=k: (b * ns + s, k * nh + h))

    n_chunks = t_blk // HGRN_CHUNK
    return pl.pallas_call(
        functools.partial(_hgrn_kernel, n_chunks=n_chunks),
        grid=(batch, nh, ns),
        in_specs=[sec(0), sec(1), sec(2), sec(3),
                  pl.BlockSpec((lb_param.shape[0], HEAD_DIM), lambda b, h, s: (0, h)),
                  pl.BlockSpec((1, HEAD_DIM), lambda b, h, s: (0, 0))],
        out_specs=pl.BlockSpec((t_blk, HEAD_DIM), lambda b, h, s: (b * ns + s, h)),
        out_shape=jax.ShapeDtypeStruct((n, nh * HEAD_DIM), BF16),
        scratch_shapes=[pltpu.VMEM((HEAD_DIM, HEAD_DIM), F32),
                        pltpu.VMEM((t_blk, HEAD_DIM), F32),
                        pltpu.VMEM((t_blk, HEAD_DIM), F32),
                        pltpu.VMEM((t_blk, HEAD_DIM), BF16),
                        pltpu.VMEM((t_blk, HEAD_DIM), F32),
                        pltpu.VMEM((n_chunks, HEAD_DIM, HEAD_DIM), F32),
                        pltpu.VMEM((n_chunks, HEAD_DIM), F32),
                        pltpu.VMEM((n_chunks, HEAD_DIM, HEAD_DIM), BF16)],
        compiler_params=pltpu.CompilerParams(
            dimension_semantics=("arbitrary", "arbitrary", "arbitrary")),
        name="hgrn_recurrence",
    )(proj, proj, proj, proj, lb_param, head_gain)


def _kv_prep_kernel(kc_ref, vc_ref, vs_ref, vw_ref, pek_ref, pev_ref, w1k_ref, w1v_ref,
                    b1k_ref, b1v_ref, w2k_ref, w2v_ref, ko_ref, vto_ref, vts_ref, vtw_ref, x_scr,
                    *, nc, ntile):
    half = CMP_BLOCK // 2
    TK = ATT_TK

    def phi(src_ref, pe_ref, w1_ref, b1_ref, w2_ref):
        x_scr[...] = src_ref[...].astype(F32)
        top = jnp.zeros((nc, HEAD_DIM), F32)
        bot = jnp.zeros((nc, HEAD_DIM), F32)
        for p in range(CMP_STRIDE):
            xp = x_scr[pl.ds(p, nc, stride=CMP_STRIDE), :]
            xa = (xp + pe_ref[p:p + 1, :]).astype(BF16)
            xb = (xp + pe_ref[half + p:half + p + 1, :]).astype(BF16)
            top = top + jnp.dot(xa, w1_ref[p * HEAD_DIM:(p + 1) * HEAD_DIM, :],
                                preferred_element_type=F32)
            bot = bot + jnp.dot(xb, w1_ref[(half + p) * HEAD_DIM:(half + p + 1) * HEAD_DIM, :],
                                preferred_element_type=F32)
        hid = top + pltpu.roll(bot, nc - 1, 0) + b1_ref[...]
        act = jax.nn.gelu(hid, approximate=True)
        return jnp.dot(act.astype(BF16), w2_ref[...], preferred_element_type=F32)

    ko_ref[...] = phi(kc_ref, pek_ref, w1k_ref, b1k_ref, w2k_ref).astype(BF16)
    vto_ref[...] = phi(vc_ref, pev_ref, w1v_ref, b1v_ref, w2v_ref).T.astype(BF16)

    ones_row = (lax.broadcasted_iota(jnp.int32, (V_PAD, TK), 0) == 0).astype(BF16)

    for u in range(ntile):
        rows = slice(u * TK, (u + 1) * TK)
        vts_ref[u] = jnp.concatenate([vs_ref[rows, :].astype(F32).T.astype(BF16), ones_row], axis=0)
        vtw_ref[u] = jnp.concatenate([vw_ref[rows, :].astype(F32).T.astype(BF16), ones_row], axis=0)


def _kv_prep(proj, blk, pe_k, pe_v, w1k, w1v, b1k, b1v, w2k, w2v, batch, seq):
    g = NSA_GROUPS
    nc = seq // CMP_STRIDE
    ntile = seq // ATT_TK

    def full(a):
        return pl.BlockSpec(a.shape, lambda b, gg: (0,) * a.ndim)

    def seq_block(col0):
        return pl.BlockSpec((seq, HEAD_DIM), lambda b, gg: (b, col0 + gg))

    def out(*shape):
        spec = pl.BlockSpec((None, None) + shape, lambda b, gg: (b, gg) + (0,) * len(shape))
        return spec, jax.ShapeDtypeStruct((batch, g) + shape, BF16)

    specs, shapes = zip(out(nc, HEAD_DIM), out(HEAD_DIM, nc),
                        out(ntile, HEAD_DIM + V_PAD, ATT_TK), out(ntile, HEAD_DIM + V_PAD, ATT_TK))
    return pl.pallas_call(
        functools.partial(_kv_prep_kernel, nc=nc, ntile=ntile),
        grid=(batch, g),
        in_specs=[seq_block(blk["k_c"]), seq_block(blk["v_c"]), seq_block(blk["v_s"]),
                  seq_block(blk["v_w"]),
                  full(pe_k), full(pe_v), full(w1k), full(w1v), full(b1k), full(b1v),
                  full(w2k), full(w2v)],
        out_specs=list(specs),
        out_shape=list(shapes),
        scratch_shapes=[pltpu.VMEM((seq, HEAD_DIM), F32)],
        compiler_params=pltpu.CompilerParams(dimension_semantics=("arbitrary", "arbitrary")),
        name="nsa_kv_prep",
    )(proj, proj, proj, proj, pe_k, pe_v, w1k, w1v, b1k, b1v, w2k, w2v)


def _nsa_attn_kernel(q_ref, kc_ref, vtc_ref, ks_ref, vts_ref, kw_ref, vtw_ref, z_ref, gt_ref,
                     win_ref, cmpb_ref, ovt_ref, o_ref,
                     q_scr, psum_scr, rank_scr, selm_scr, s_scr, m8_scr, acc_scr, tot_scr,
                     *, seq):
    TQ, TK = ATT_TQ, ATT_TK
    L = NSA_HPG * TQ
    nc = seq // CMP_STRIDE
    nsel = seq // SEL_BLOCK
    blk_per_tile = TK // SEL_BLOCK
    n_win = (WINDOW + TQ) // TK
    g = pl.program_id(1)
    i = pl.program_id(2)
    t0 = i * TQ

    for h in range(NSA_HPG):
        q_scr[h * TQ:(h + 1) * TQ, :] = q_ref[:, h * HEAD_DIM:(h + 1) * HEAD_DIM]

    def gate_row(c):
        rows = [(g * NSA_HPG + h) * N_BRANCH + c for h in range(NSA_HPG)]
        return jnp.concatenate([_sigmoid(gt_ref[pl.ds(r, 1), :]) for r in rows], axis=1)

    cmp_row0 = pl.multiple_of(nc - i * (TQ // CMP_STRIDE), 8)

    def cmp_branch(rows):
        s = lax.dot_general(kc_ref[0:rows, :], q_scr[...], NT_DIMS, preferred_element_type=F32)
        s = s + cmpb_ref[pl.ds(cmp_row0, rows), :]
        mc = jnp.max(s, axis=0, keepdims=True)
        p = jnp.exp2(s - mc)
        inv = jnp.where(mc > 0.1 * NEG_INF, 1.0 / jnp.sum(p, axis=0, keepdims=True), 0.0)
        p = p * inv
        psum = p[:, 0:TQ]
        for h in range(1, NSA_HPG):
            psum = psum + p[:, h * TQ:(h + 1) * TQ]
        if rows < nc:
            psum = jnp.concatenate([psum, jnp.zeros((nc - rows, TQ), F32)], axis=0)
        psum_scr[...] = psum
        oc = jnp.dot(vtc_ref[:, 0:rows], p.astype(BF16), preferred_element_type=F32)
        tot_scr[...] = oc * gate_row(0)

    qblk_per_rows = CMP_ROWS // (TQ // CMP_STRIDE)
    for k in range(1, nc // CMP_ROWS + 1):
        pl.when(i // qblk_per_rows + 1 == k)(functools.partial(cmp_branch, k * CMP_ROWS))

    ovt = ovt_ref[...]
    phi, pmid, plo = _split3(psum_scr[...])
    imp = (jnp.dot(ovt, phi, preferred_element_type=F32)
           + jnp.dot(ovt, pmid, preferred_element_type=F32)
           + jnp.dot(ovt, plo, preferred_element_type=F32))

    jrow = lax.broadcasted_iota(jnp.int32, (nsel, TQ), 0)
    tcol = t0 + lax.broadcasted_iota(jnp.int32, (nsel, TQ), 1)
    cur = lax.shift_right_arithmetic(tcol, int(math.log2(SEL_BLOCK)))
    forced = (jrow == 0) | (jrow == cur) | (jrow == cur - 1)
    visible = jrow * SEL_BLOCK <= tcol
    score = jnp.where(forced, FORCE_SCORE, jnp.where(visible, imp, NEG_INF))
    row8 = lax.broadcasted_iota(jnp.int32, (8, TQ), 0)
    score8 = [score[r * 8:(r + 1) * 8] for r in range(nsel // 8)]
    rank_scr[...] = jnp.zeros_like(rank_scr)
    n_visible = (i + 1) * (TQ // SEL_BLOCK)

    def count_ahead(groups):
        cnt = [jnp.zeros((8, TQ), jnp.int32) for _ in range(groups)]
        for jp in range(groups * 8):
            row = score[jp:jp + 1, :]
            for r in range(groups):
                sc = score8[r]
                if r * 8 > jp:
                    ahead = row >= sc
                elif r * 8 + 7 < jp:
                    ahead = row > sc
                else:
                    ahead = (row > sc) | ((row == sc) & (row8 > jp - r * 8))
                cnt[r] = cnt[r] + ahead.astype(jnp.int32)
        rank_scr[0:groups * 8, :] = jnp.concatenate(cnt, axis=0)

    for groups in range(1, nsel // 8 + 1):
        pl.when((n_visible + 7) // 8 == groups)(functools.partial(count_ahead, groups))
    rank = rank_scr[...]
    selm = jnp.where(rank < min(N_SELECT, nsel), 0.0, NEG_INF)
    selm = jnp.concatenate([selm] * NSA_HPG, axis=1)
    selm_scr[...] = selm

    def step_rows(u0, nt):
        return pl.ds(pl.multiple_of(u0 * TK, TK), nt * TK)

    def logits_step(k_ref, u0, nt, win_row0):
        s = lax.dot_general(k_ref[step_rows(u0, nt), :], q_scr[...], NT_DIMS,
                            preferred_element_type=F32)
        if win_row0 is not None:
            s = s + win_ref[win_row0:win_row0 + nt * TK, :]
        else:
            parts = []
            for j in range(nt):
                near = u0 + j - (i - (n_win - 2))
                row0 = pl.multiple_of(jnp.where(near < 0, n_win, near + n_win - 2) * TK, TK)
                for r in range(blk_per_tile):
                    lo = j * TK + r * SEL_BLOCK
                    parts.append(s[lo:lo + SEL_BLOCK]
                                 + win_ref[pl.ds(row0 + r * SEL_BLOCK, SEL_BLOCK), :]
                                 + selm_scr[pl.ds((u0 + j) * blk_per_tile + r, 1), :])
            s = jnp.concatenate(parts, axis=0)
        s_scr[step_rows(u0, nt), :] = s
        m8_scr[...] = jnp.maximum(m8_scr[...], jnp.max(s.reshape(nt * TK // 8, 8, L), axis=0))

    def pv_step(vt_ref, m, u0, nt):
        p = jnp.exp2((s_scr[step_rows(u0, nt), :] - m).astype(BF16))
        vt = jnp.concatenate([vt_ref[u0 + j] for j in range(nt)], axis=1)
        acc_scr[...] += jnp.dot(vt, p, preferred_element_type=F32)

    def over_tiles(u_lo, n, step):
        big = ATT_STEP_TILES
        nbig = lax.shift_right_logical(n, int(math.log2(big)))

        def body(j, carry):
            step(u_lo + big * j, big)
            return carry
        lax.fori_loop(0, nbig, body, 0)
        u = u_lo + big * nbig
        nt = big // 2
        while nt >= 1:
            pl.when((n & nt) != 0)(functools.partial(step, u, nt))
            u = u + (n & nt)
            nt //= 2

    def near_diagonal(step):
        for nt in range(1, n_win + 1):
            cond = (i == nt - 1) if nt < n_win else (i >= nt - 1)
            pl.when(cond)(functools.partial(step, i - (nt - 1), nt, (n_win - nt) * TK))

    def finish(c):
        w = gate_row(c) * (1.0 / acc_scr[HEAD_DIM:HEAD_DIM + 1, :])
        tot_scr[...] = tot_scr[...] + acc_scr[0:HEAD_DIM, :] * w

    def reset():
        m8_scr[...] = jnp.full_like(m8_scr, NEG_INF)
        acc_scr[...] = jnp.zeros_like(acc_scr)

    reset()
    over_tiles(0, i + 1, lambda u0, nt: logits_step(ks_ref, u0, nt, None))
    m_sel = jnp.max(m8_scr[...], axis=0, keepdims=True)
    over_tiles(0, i + 1, functools.partial(pv_step, vts_ref, m_sel))
    finish(1)

    reset()
    near_diagonal(lambda u0, nt, row0: logits_step(kw_ref, u0, nt, row0))
    m_win = jnp.max(m8_scr[...], axis=0, keepdims=True)
    near_diagonal(lambda u0, nt, row0: pv_step(vtw_ref, m_win, u0, nt))
    finish(2)

    for h in range(NSA_HPG):
        cols = slice(h * HEAD_DIM, (h + 1) * HEAD_DIM)
        z = z_ref[:, cols].astype(F32)
        o_ref[:, cols] = (tot_scr[:, h * TQ:(h + 1) * TQ].T * (z * _sigmoid(z))).astype(o_ref.dtype)


def _nsa_attention(proj, kcmp, vtc, vts, vtw, gt, win, cmpb, ovt, cols, batch, seq):
    TQ, TK = ATT_TQ, ATT_TK
    n = proj.shape[0]
    ng = NSA_GROUPS
    nq = seq // TQ
    L = NSA_HPG * TQ
    gw = NSA_HPG * HEAD_DIM

    def seq_block(col0):
        return pl.BlockSpec((seq, HEAD_DIM), lambda b, g, i: (b, col0 + g))

    def per_bg(a):
        return pl.BlockSpec((None, None) + a.shape[2:], lambda b, g, i: (b, g) + (0,) * (a.ndim - 2))

    def per_g(a):
        return pl.BlockSpec((None,) + a.shape[1:], lambda b, g, i: (g,) + (0,) * (a.ndim - 1))

    return pl.pallas_call(
        functools.partial(_nsa_attn_kernel, seq=seq),
        grid=(batch, ng, nq),
        in_specs=[pl.BlockSpec((TQ, gw), lambda b, g, i: (b * nq + i, cols["q"] + g)),
                  per_bg(kcmp), per_bg(vtc),
                  seq_block(cols["k_s"]), per_bg(vts),
                  seq_block(cols["k_w"]), per_bg(vtw),
                  pl.BlockSpec((TQ, gw), lambda b, g, i: (b * nq + i, cols["z"] + g)),
                  pl.BlockSpec((gt.shape[0], TQ), lambda b, g, i: (0, b * nq + i)),
                  per_g(win), per_g(cmpb),
                  pl.BlockSpec(ovt.shape, lambda b, g, i: (0, 0))],
        out_specs=pl.BlockSpec((TQ, gw), lambda b, g, i: (b * nq + i, g)),
        out_shape=jax.ShapeDtypeStruct((n, ng * gw), BF16),
        scratch_shapes=[pltpu.VMEM((L, HEAD_DIM), BF16),
                        pltpu.VMEM((seq // CMP_STRIDE, TQ), F32),
                        pltpu.VMEM((seq // SEL_BLOCK, TQ), jnp.int32),
                        pltpu.VMEM((seq // SEL_BLOCK, L), F32),
                        pltpu.VMEM((seq, L), F32),
                        pltpu.VMEM((8, L), F32),
                        pltpu.VMEM((HEAD_DIM + V_PAD, L), F32),
                        pltpu.VMEM((HEAD_DIM, L), F32)],
        compiler_params=pltpu.CompilerParams(
            dimension_semantics=("arbitrary", "arbitrary", "arbitrary"),
            vmem_limit_bytes=ATT_VMEM_LIMIT),
        name="nsa_attention",
    )(proj, kcmp, vtc, proj, vts, proj, vtw, proj, gt, win, cmpb, ovt)


def _bucket_of_distance():
    d = np.arange(MAX_DISTANCE)
    max_exact = NUM_BUCKETS // 2
    large = max_exact + (np.log(np.maximum(d, 1).astype(np.float32) / max_exact)
                         / math.log(MAX_DISTANCE / max_exact) * (NUM_BUCKETS - max_exact)).astype(np.int32)
    large = np.minimum(large, NUM_BUCKETS - 1)
    return np.where(d < max_exact, d, large)


CMP_NEAR = 16


def _bias_builder_kernel(vw_ref, vc_ref, far_ref, win_ref, cmpb_ref, *, nc):
    TQ, TK = ATT_TQ, ATT_TK
    n_win = (WINDOW + TQ) // TK
    for h in range(NSA_HPG):
        lanes = slice(h * TQ, (h + 1) * TQ)
        for w in range(n_win):
            v = vw_ref[h * n_win + w:h * n_win + w + 1, :]
            r = pltpu.roll(jnp.broadcast_to(v, (TK, 2 * TQ)), 0, 1, stride=1, stride_axis=0)
            win_ref[w * TK:(w + 1) * TK, lanes] = r[:, TQ:2 * TQ]
        win_ref[n_win * TK:(n_win + 1) * TK, lanes] = jnp.broadcast_to(far_ref[:, lanes], (TK, TQ))
        cmpb_ref[0:nc - CMP_NEAR, lanes] = jnp.broadcast_to(far_ref[:, lanes], (nc - CMP_NEAR, TQ))
        vc = vc_ref[h:h + 1, :]
        rc = pltpu.roll(jnp.broadcast_to(vc, (2 * CMP_NEAR, 4 * TQ)), 0, 1,
                        stride=CMP_STRIDE, stride_axis=0)
        cmpb_ref[nc - CMP_NEAR:nc + CMP_NEAR, lanes] = rc[:, 2 * TQ:3 * TQ]
        cmpb_ref[nc + CMP_NEAR:2 * nc, lanes] = jnp.full((nc - CMP_NEAR, TQ), NEG_INF, F32)


def _bias_tiles(rel_bias, seq):
    TQ, TK = ATT_TQ, ATT_TK
    assert TQ == TK and TQ >= MAX_DISTANCE
    assert CMP_STRIDE * CMP_NEAR >= TQ and 2 * TQ >= CMP_STRIDE * 2 * CMP_NEAR
    assert CMP_STRIDE * (CMP_NEAR + 1) - (CMP_BLOCK - 1) >= MAX_DISTANCE - 1
    nc = seq // CMP_STRIDE
    ng, hpg = NSA_GROUPS, NSA_HPG
    n_win = (WINDOW + TQ) // TK
    tb = (rel_bias.astype(F32) * LOG2E)[_bucket_of_distance()].T.reshape(ng, hpg, MAX_DISTANCE)

    def by_distance(dist, valid):
        vals = tb[:, :, np.clip(dist, 0, MAX_DISTANCE - 1)]
        return jnp.where(valid, vals, NEG_INF)

    x = np.arange(2 * TQ)[None, :]
    dw = WINDOW - TK * np.arange(n_win)[:, None] - TQ + x
    vw = by_distance(dw, (dw >= 0) & (dw < WINDOW)).reshape(ng, hpg * n_win, 2 * TQ)
    xc = np.arange(4 * TQ)
    dc = xc - 2 * TQ + CMP_STRIDE * CMP_NEAR - (CMP_BLOCK - 1)
    vc = by_distance(dc, dc >= 0)
    L = hpg * TQ
    far = jnp.broadcast_to(tb[:, :, MAX_DISTANCE - 1:], (ng, hpg, TQ)).reshape(ng, 1, L)

    win, cmpb = pl.pallas_call(
        functools.partial(_bias_builder_kernel, nc=nc),
        grid=(ng,),
        in_specs=[pl.BlockSpec((None,) + vw.shape[1:], lambda g: (g, 0, 0)),
                  pl.BlockSpec((None,) + vc.shape[1:], lambda g: (g, 0, 0)),
                  pl.BlockSpec((None,) + far.shape[1:], lambda g: (g, 0, 0))],
        out_specs=[pl.BlockSpec((None, WINDOW + TQ + TK, L), lambda g: (g, 0, 0)),
                   pl.BlockSpec((None, 2 * nc, L), lambda g: (g, 0, 0))],
        out_shape=[jax.ShapeDtypeStruct((ng, WINDOW + TQ + TK, L), F32),
                   jax.ShapeDtypeStruct((ng, 2 * nc, L), F32)],
        compiler_params=pltpu.CompilerParams(dimension_semantics=("arbitrary",)),
        name="nsa_bias_tiles",
    )(vw, vc, far)
    return win, cmpb


def _overlap_t(seq):
    nc = seq // CMP_STRIDE
    nsel = seq // SEL_BLOCK
    ci = np.arange(nc)[None, :]
    sj = np.arange(nsel)[:, None]
    ov = (CMP_STRIDE * ci < SEL_BLOCK * (sj + 1)) & (CMP_STRIDE * ci + CMP_BLOCK > SEL_BLOCK * sj)
    ov = ov & (ci < nc - 1)
    return jnp.asarray(ov, BF16)


def kernel(x, norm_gains, final_gain, rel_bias, hgrn_lb, hgrn_w_in, hgrn_head_gain, hgrn_w_out,
           nsa_w_in, nsa_pe_k, nsa_pe_v, nsa_phi_k_w1, nsa_phi_k_b1, nsa_phi_k_w2,
           nsa_phi_v_w1, nsa_phi_v_b1, nsa_phi_v_w2, nsa_w_out):
    batch, seq, d = x.shape
    n = batch * seq
    x2d = x.reshape(n, d)
    qscale = HEAD_DIM ** -0.5

    def scale_q_columns(w, scale):
        col = np.ones((1, w.shape[1]), np.float32)
        col[:, :d] = scale
        return (w * col).astype(BF16)

    proj = _norm_proj(x2d, norm_gains[0:1], scale_q_columns(hgrn_w_in[0], qscale))
    o = _hgrn_recurrence(proj, hgrn_lb, hgrn_head_gain[0:1], batch, seq)
    x1, h1 = _out_proj(o, hgrn_w_out[0].astype(BF16), x2d, norm_gains[1:2], final=False)

    kvw = NSA_GROUPS * HEAD_DIM
    gate0 = d + 6 * kvw
    gate1 = gate0 + NSA_HEADS * N_BRANCH
    w_main, wg = _nsa_weight_prep(nsa_w_in, d, gate0, gate1, qscale * LOG2E)
    proj, gt = _proj_gate(h1, w_main, wg)

    blk = {}
    for idx, name in enumerate(("k_c", "v_c", "k_s", "v_s", "k_w", "v_w")):
        blk[name] = (d + idx * kvw) // HEAD_DIM
    gw = NSA_HPG * HEAD_DIM
    cols = dict(blk, q=0, z=gate0 // gw)

    kcmp, vtc, vts, vtw = _kv_prep(
        proj, blk, nsa_pe_k[0], nsa_pe_v[0],
        nsa_phi_k_w1[0].astype(BF16), nsa_phi_v_w1[0].astype(BF16),
        nsa_phi_k_b1[0:1], nsa_phi_v_b1[0:1],
        nsa_phi_k_w2[0].astype(BF16), nsa_phi_v_w2[0].astype(BF16), batch, seq)

    win, cmpb = _bias_tiles(rel_bias, seq)
    o = _nsa_attention(proj, kcmp, vtc, vts, vtw, gt, win, cmpb, _overlap_t(seq), cols, batch, seq)
    out, = _out_proj(o, nsa_w_out[0].astype(BF16), x1, final_gain[None, :], final=True)
    return out.reshape(batch, seq, d)
```
